```python
import math
import jax
import jax.numpy as jnp
from jax import lax
import numpy as np

D_MODEL = 1024
BATCH = 8
SEQ = 4096
DEPTH = 4

CTX_LEN = 256
GRID_W = 64
EPS = 1e-6
ROPE_THETA = 10000.0
BLOCK = 128
N_MOD = 6

CONV_WIDTH = 512
CONV_K = 3
SWA_HEADS = 8
SWA_KV_HEADS = 2
SWA_HEAD_DIM = 64
WINDOW = 128
MLA_HEADS = 8
MLA_Q_RANK = 256
MLA_KV_RANK = 128
MLA_NOPE = 64
MLA_ROPE = 32
MLA_V = 64
DIFF_HEADS = 4
DIFF_DIM = 64
DIFF_V = 2 * DIFF_DIM
FF_DENSE = 2816
N_EXPERTS = 8
TOP_K = 2
FF_EXPERT = 3584
MOE_BLOCK = 512

EVEN_Q_WIDTHS = (CONV_WIDTH, CONV_WIDTH, CONV_WIDTH, SWA_HEADS * SWA_HEAD_DIM)
EVEN_KV_WIDTHS = (SWA_KV_HEADS * SWA_HEAD_DIM, SWA_KV_HEADS * SWA_HEAD_DIM)
EVEN_Q_COLS = sum(EVEN_Q_WIDTHS)
EVEN_IN = EVEN_Q_COLS + sum(EVEN_KV_WIDTHS)
EVEN_MIX = CONV_WIDTH + SWA_HEADS * SWA_HEAD_DIM
ODD_Q_WIDTHS = (MLA_Q_RANK, DIFF_HEADS * 2 * DIFF_DIM)
ODD_KV_WIDTHS = (MLA_KV_RANK, MLA_ROPE, DIFF_HEADS * 2 * DIFF_DIM, DIFF_HEADS * DIFF_V)
ODD_Q_COLS = sum(ODD_Q_WIDTHS)
ODD_IN = ODD_Q_COLS + sum(ODD_KV_WIDTHS)
ODD_MIX = MLA_HEADS * MLA_V + DIFF_HEADS * DIFF_V

kernel_name = 'hybrid_flow_backbone'


def rms_norm(x, g):
    xf = x.astype(jnp.float32)
    y = xf * lax.rsqrt(jnp.mean(xf * xf, axis=-1, keepdims=True) + EPS)
    return (y * g.astype(jnp.float32)).astype(x.dtype)


def modulate(h, shift, scale):
    return h * (1 + scale) + shift


def split_cols(p, widths):
    idx, acc = [], 0
    for w in widths[:-1]:
        acc += w
        idx.append(acc)
    return jnp.split(p, idx, axis=-1)


def axial_rope_tables(rows, cols, rot_dim):
    axis_dim = rot_dim // 2
    inv_freq = 1.0 / (ROPE_THETA ** (jnp.arange(0, axis_dim, 2, dtype=jnp.float32) / axis_dim))
    ang_r = rows.astype(jnp.float32)[:, None] * inv_freq[None, :]
    ang_c = cols.astype(jnp.float32)[:, None] * inv_freq[None, :]
    return (jnp.cos(ang_r), jnp.sin(ang_r), jnp.cos(ang_c), jnp.sin(ang_c))


def _rotate(x, cos, sin):
    n = x.shape[-1] // 2
    shape = (cos.shape[0],) + (1,) * (x.ndim - 3) + (n,)
    cs, sn = cos.reshape(shape), sin.reshape(shape)
    x1, x2 = x[..., :n], x[..., n:]
    return jnp.concatenate([x1 * cs - x2 * sn, x2 * cs + x1 * sn], axis=-1)


def axial_rope(x, tabs):
    cr, sr, cc, sc = tabs
    h = x.shape[-1] // 2
    xf = x.astype(jnp.float32)
    out = jnp.concatenate([_rotate(xf[..., :h], cr, sr), _rotate(xf[..., h:], cc, sc)], axis=-1)
    return out.astype(x.dtype)


def short_conv(u, w):
    up = jnp.pad(u, ((0, 0), (1, 1), (0, 0)))
    return up[:, :-2] * w[0] + up[:, 1:-1] * w[1] + up[:, 2:] * w[2]


def banded_window_attn(q, k, v, k_ctx, v_ctx, sink):
    Bn, S, H, d = q.shape
    G = k.shape[2]
    R = H // G
    nb = S // BLOCK
    scale = d ** -0.5

    def band(t):
        tp = jnp.pad(t, ((0, 0), (BLOCK, BLOCK), (0, 0), (0, 0))).reshape(Bn, nb + 2, BLOCK, G, d)
        return jnp.moveaxis(jnp.concatenate([tp[:, :-2], tp[:, 1:-1], tp[:, 2:]], axis=2), 1, 0)

    qb = jnp.moveaxis(q.reshape(Bn, nb, BLOCK, G, R, d), 1, 0)
    kb, vb = band(k), band(v)
    rel = jnp.arange(3 * BLOCK)[None, :] - BLOCK - jnp.arange(BLOCK)[:, None]
    sink_l = sink.reshape(1, G, R, 1, 1).astype(jnp.float32)
    n_win = 3 * BLOCK
    n_ctx = k_ctx.shape[1]

    def one_block(args):
        n, qn, kn, vn = args
        k_pos = n * BLOCK - BLOCK + jnp.arange(3 * BLOCK)
        ok = (jnp.abs(rel) <= WINDOW) & (k_pos >= 0)[None, :] & (k_pos < S)[None, :]
        s_win = jnp.einsum('bqgrd,bkgd->bgrqk', qn, kn).astype(jnp.float32) * scale
        s_win = jnp.where(ok, s_win, -jnp.inf)
        s_ctx = jnp.einsum('bqgrd,bcgd->bgrqc', qn, k_ctx).astype(jnp.float32) * scale
        s_snk = jnp.broadcast_to(sink_l, s_win.shape[:-1] + (1,))
        p = jax.nn.softmax(jnp.concatenate([s_win, s_ctx, s_snk], axis=-1), axis=-1)
        p_win = p[..., :n_win].astype(vn.dtype)
        p_ctx = p[..., n_win:n_win + n_ctx].astype(vn.dtype)
        return jnp.einsum('bgrqk,bkgd->bqgrd', p_win, vn) + jnp.einsum('bgrqc,bcgd->bqgrd', p_ctx, v_ctx)

    o = lax.map(one_block, (jnp.arange(nb), qb, kb, vb))
    return jnp.moveaxis(o, 0, 1).reshape(Bn, S, H * d)


def sink_gqa_attn(q, k, v, sink):
    Bn, L, H, d = q.shape
    G = k.shape[2]
    R = H // G
    s = jnp.einsum('blgrd,bmgd->bgrlm', q.reshape(Bn, L, G, R, d), k).astype(jnp.float32) * (d ** -0.5)
    s_snk = jnp.broadcast_to(sink.reshape(1, G, R, 1, 1).astype(jnp.float32), s.shape[:-1] + (1,))
    p = jax.nn.softmax(jnp.concatenate([s, s_snk], axis=-1), axis=-1)[..., :-1]
    o = jnp.einsum('bgrlm,bmgd->blgrd', p.astype(v.dtype), v)
    return o.reshape(Bn, L, H * d)


def softmax_attn(q, k, v, scale):
    s = jnp.einsum('bqhd,bkhd->bhqk', q, k).astype(jnp.float32) * scale
    p = jax.nn.softmax(s, axis=-1).astype(v.dtype)
    return jnp.einsum('bhqk,bkhd->bqhd', p, v)


def diff_softmax_attn(q, k, v, lam, scale):
    s = jnp.einsum('bqhcd,bkhcd->bhcqk', q, k).astype(jnp.float32) * scale
    p = jax.nn.softmax(s, axis=-1)
    pd = p[:, :, 0] - lam * p[:, :, 1]
    return jnp.einsum('bhqk,bkhd->bqhd', pd.astype(v.dtype), v)


def sweep_query_blocks(fn, q):
    Bn, S = q.shape[:2]
    nb = S // BLOCK
    qb = jnp.moveaxis(q.reshape((Bn, nb, BLOCK) + q.shape[2:]), 1, 0)
    o = jnp.moveaxis(lax.map(fn, qb), 0, 1)
    return o.reshape((Bn, S) + o.shape[3:])


def diff_lambda_init(layer):
    return 0.8 - 0.6 * math.exp(-0.3 * layer)


def conv_window_mixer(h_lat, h_ctx, w_in, conv_w, sink, w_out, rope, need_ctx):
    Bn, S, _ = h_lat.shape
    L = h_ctx.shape[1]
    H, G, d = SWA_HEADS, SWA_KV_HEADS, SWA_HEAD_DIM
    p = h_lat @ w_in
    b_gate, c_gate, u, q = split_cols(p[..., :EVEN_Q_COLS], EVEN_Q_WIDTHS)
    k, v = split_cols(p[..., EVEN_Q_COLS:], EVEN_KV_WIDTHS)
    k_c, v_c = split_cols(h_ctx @ w_in[:, EVEN_Q_COLS:], EVEN_KV_WIDTHS)
    k_c = k_c.reshape(Bn, L, G, d)
    v_c = v_c.reshape(Bn, L, G, d)
    conv_out = b_gate * short_conv(c_gate * u, conv_w)
    q = axial_rope(q.reshape(Bn, S, H, d), rope)
    k = axial_rope(k.reshape(Bn, S, G, d), rope)
    attn = banded_window_attn(q, k, v.reshape(Bn, S, G, d), k_c, v_c, sink)
    y_lat = jnp.concatenate([conv_out, attn], axis=-1) @ w_out
    y_ctx = None
    if need_ctx:
        b_c, c_c, u_c, q_c = split_cols(h_ctx @ w_in[:, :EVEN_Q_COLS], EVEN_Q_WIDTHS)
        conv_c = b_c * short_conv(c_c * u_c, conv_w)
        attn_c = sink_gqa_attn(q_c.reshape(Bn, L, H, d), k_c, v_c, sink)
        y_ctx = jnp.concatenate([conv_c, attn_c], axis=-1) @ w_out
    return y_lat, y_ctx


def mla_diff_mixer(h_lat, h_ctx, w_in, q_norm_g, kv_norm_g, w_q_up, w_kv_up, lam_p, subln_g, w_out,
                   rope_mla, rope_diff, lam_init, need_ctx):
    Bn, S, _ = h_lat.shape
    L = h_ctx.shape[1]
    lp = lam_p.astype(jnp.float32)
    lam = jnp.exp(jnp.sum(lp[0] * lp[1])) - jnp.exp(jnp.sum(lp[2] * lp[3])) + lam_init
    mla_scale = (MLA_NOPE + MLA_ROPE) ** -0.5
    diff_scale = DIFF_DIM ** -0.5

    def mla_query(q_down, rope):
        n = q_down.shape[1]
        q = (rms_norm(q_down, q_norm_g) @ w_q_up).reshape(Bn, n, MLA_HEADS, MLA_NOPE + MLA_ROPE)
        q_pe = q[..., MLA_NOPE:] if rope is None else axial_rope(q[..., MLA_NOPE:], rope)
        return jnp.concatenate([q[..., :MLA_NOPE], q_pe], axis=-1)

    def mla_key_value(kv_down, k_pe, rope):
        n = kv_down.shape[1]
        kv = (rms_norm(kv_down, kv_norm_g) @ w_kv_up).reshape(Bn, n, MLA_HEADS, MLA_NOPE + MLA_V)
        k_pe = k_pe[:, :, None, :]
        if rope is not None:
            k_pe = axial_rope(k_pe, rope)
        k = jnp.concatenate([kv[..., :MLA_NOPE], jnp.broadcast_to(k_pe, (Bn, n, MLA_HEADS, MLA_ROPE))], axis=-1)
        return k, kv[..., MLA_NOPE:]

    def diff_qk(t, rope):
        t = t.reshape(Bn, t.shape[1], DIFF_HEADS, 2, DIFF_DIM)
        return t if rope is None else axial_rope(t, rope)

    def merge(o_m, o_d):
        n = o_m.shape[1]
        o_d = rms_norm(o_d, subln_g) * (1.0 - lam_init)
        return jnp.concatenate([o_m.reshape(Bn, n, -1), o_d.reshape(Bn, n, -1)], axis=-1) @ w_out

    p = h_lat @ w_in
    mq_down, dq = split_cols(p[..., :ODD_Q_COLS], ODD_Q_WIDTHS)
    kv_down, k_pe, dk, dv = split_cols(p[..., ODD_Q_COLS:], ODD_KV_WIDTHS)
    kv_down_c, k_pe_c, dk_c, dv_c = split_cols(h_ctx @ w_in[:, ODD_Q_COLS:], ODD_KV_WIDTHS)
    mk_lat, mv_lat = mla_key_value(kv_down, k_pe, rope_mla)
    mk_c, mv_c = mla_key_value(kv_down_c, k_pe_c, None)
    mk_all = jnp.concatenate([mk_lat, mk_c], axis=1)
    mv_all = jnp.concatenate([mv_lat, mv_c], axis=1)
    dk_c = diff_qk(dk_c, None)
    dv_c = dv_c.reshape(Bn, L, DIFF_HEADS, DIFF_V)
    dk_all = jnp.concatenate([diff_qk(dk, rope_diff), dk_c], axis=1)
    dv_all = jnp.concatenate([dv.reshape(Bn, S, DIFF_HEADS, DIFF_V), dv_c], axis=1)
    o_m = sweep_query_blocks(lambda qb: softmax_attn(qb, mk_all, mv_all, mla_scale), mla_query(mq_down, rope_mla))
    o_d = sweep_query_blocks(lambda qb: diff_softmax_attn(qb, dk_all, dv_all, lam, diff_scale), diff_qk(dq, rope_diff))
    y_lat = merge(o_m, o_d)
    y_ctx = None
    if need_ctx:
        mq_c, dq_c = split_cols(h_ctx @ w_in[:, :ODD_Q_COLS], ODD_Q_WIDTHS)
        o_mc = softmax_attn(mla_query(mq_c, None), mk_c, mv_c, mla_scale)
        o_dc = diff_softmax_attn(diff_qk(dq_c, None), dk_c, dv_c, lam, diff_scale)
        y_ctx = merge(o_mc, o_dc)
    return y_lat, y_ctx


def swiglu(h, w_gate_up, w_down):
    g, u = jnp.split(h @ w_gate_up, 2, axis=-1)
    return (jax.nn.silu(g) * u) @ w_down


def moe_swiglu(h, w_router, w_gate_up, w_down):
    T, D = h.shape
    logits = (h @ w_router).astype(jnp.float32)
    top_logit, top_e = lax.top_k(logits, TOP_K)
    weights = jax.nn.softmax(top_logit, axis=-1)
    A = T * TOP_K
    flat_e = top_e.reshape(A)
    order = jnp.argsort(flat_e)
    sorted_e = flat_e[order]
    sorted_tok = order // TOP_K
    counts = jnp.bincount(flat_e, length=N_EXPERTS)
    starts = jnp.cumsum(counts) - counts
    padded = (counts + MOE_BLOCK - 1) // MOE_BLOCK * MOE_BLOCK
    ends = jnp.cumsum(padded)
    pstarts = ends - padded
    dest = pstarts[sorted_e] + jnp.arange(A) - starts[sorted_e]
    n_blocks = -(-(A + N_EXPERTS * (MOE_BLOCK - 1)) // MOE_BLOCK)
    buf = jnp.zeros((n_blocks * MOE_BLOCK, D), h.dtype).at[dest].set(h[sorted_tok])
    block_e = jnp.clip(jnp.searchsorted(ends, jnp.arange(n_blocks) * MOE_BLOCK, side='right'), 0, N_EXPERTS - 1)

    def expert_block(args):
        rows, e = args
        g, u = jnp.split(rows @ w_gate_up[e], 2, axis=-1)
        return (jax.nn.silu(g) * u) @ w_down[e]

    out = lax.map(expert_block, (buf.reshape(n_blocks, MOE_BLOCK, D), block_e)).reshape(n_blocks * MOE_BLOCK, D)
    y = out[dest] * weights.reshape(A)[order][:, None].astype(h.dtype)
    return jax.ops.segment_sum(y, sorted_tok, num_segments=T)


def setup_inputs(seed: int = 0) -> dict:
    key = jax.random.key(seed)
    ks = jax.random.split(key, 24)
    n_even = (DEPTH + 1) // 2
    n_odd = DEPTH // 2
    f32 = jnp.float32

    def nrm(k, shape, fan_in, gain=1.0):
        return jax.random.normal(k, shape, f32) * (gain * fan_in ** -0.5)

    def gain(k, shape):
        return 1.0 + 0.05 * jax.random.normal(k, shape, f32)

    return {
        'x': jax.random.normal(ks[0], (BATCH, SEQ, D_MODEL), f32),
        'c': jax.random.normal(ks[1], (BATCH, D_MODEL), f32),
        'ctx': jax.random.normal(ks[2], (BATCH, CTX_LEN, D_MODEL), f32),
        'c_ctx': jax.random.normal(ks[3], (D_MODEL,), f32),
        'w_mod': nrm(ks[4], (DEPTH, D_MODEL, N_MOD * D_MODEL), D_MODEL, 0.5),
        'b_mod': 0.02 * jax.random.normal(ks[5], (DEPTH, N_MOD * D_MODEL), f32),
        'norm_g': gain(ks[6], (DEPTH, 4, D_MODEL)),
        'w_in_even': nrm(ks[7], (n_even, D_MODEL, EVEN_IN), D_MODEL),
        'conv_w': nrm(ks[8], (n_even, CONV_K, CONV_WIDTH), CONV_K),
        'sink': 0.5 * jax.random.normal(ks[9], (n_even, SWA_HEADS), f32),
        'w_out_even': nrm(ks[10], (n_even, EVEN_MIX, D_MODEL), EVEN_MIX),
        'w_in_odd': nrm(ks[11], (n_odd, D_MODEL, ODD_IN), D_MODEL),
        'mla_q_norm_g': gain(ks[12], (n_odd, MLA_Q_RANK)),
        'mla_kv_norm_g': gain(ks[13], (n_odd, MLA_KV_RANK)),
        'w_q_up': nrm(ks[14], (n_odd, MLA_Q_RANK, MLA_HEADS * (MLA_NOPE + MLA_ROPE)), MLA_Q_RANK),
        'w_kv_up': nrm(ks[15], (n_odd, MLA_KV_RANK, MLA_HEADS * (MLA_NOPE + MLA_V)), MLA_KV_RANK),
        'diff_lambda': 0.1 * jax.random.normal(ks[16], (n_odd, 4, DIFF_DIM), f32),
        'diff_subln_g': gain(ks[17], (n_odd, DIFF_V)),
        'w_out_odd': nrm(ks[18], (n_odd, ODD_MIX, D_MODEL), ODD_MIX),
        'w_ff_gu': nrm(ks[19], (n_even, D_MODEL, 2 * FF_DENSE), D_MODEL),
        'w_ff_down': nrm(ks[20], (n_even, FF_DENSE, D_MODEL), FF_DENSE),
        'w_router': nrm(ks[21], (n_odd, D_MODEL, N_EXPERTS), D_MODEL),
        'w_exp_gu': nrm(ks[22], (n_odd, N_EXPERTS, D_MODEL, 2 * FF_EXPERT), D_MODEL),
        'w_exp_down': nrm(ks[23], (n_odd, N_EXPERTS, FF_EXPERT, D_MODEL), FF_EXPERT),
    }


def reference(x, c, ctx, c_ctx, w_mod, b_mod, norm_g, w_in_even, conv_w, sink, w_out_even,
              w_in_odd, mla_q_norm_g, mla_kv_norm_g, w_q_up, w_kv_up, diff_lambda, diff_subln_g,
              w_out_odd, w_ff_gu, w_ff_down, w_router, w_exp_gu, w_exp_down):
    Bn, S, D = x.shape
    L = ctx.shape[1]
    ROWS = S // GRID_W
    rows = jnp.repeat(jnp.arange(ROWS), GRID_W)
    cols = jnp.tile(jnp.arange(GRID_W), ROWS)
    rope64 = axial_rope_tables(rows, cols, SWA_HEAD_DIM)
    rope32 = axial_rope_tables(rows, cols, MLA_ROPE)
    x_ctx = ctx
    for layer in range(DEPTH):
        need_ctx = layer < DEPTH - 1
        i = layer // 2
        m_lat = jnp.split((jax.nn.silu(c) @ w_mod[layer] + b_mod[layer])[:, None, :], N_MOD, axis=-1)
        m_ctx = jnp.split(jax.nn.silu(c_ctx) @ w_mod[layer] + b_mod[layer], N_MOD, axis=-1)
        g = norm_g[layer]
        h_lat = modulate(rms_norm(x, g[0]), m_lat[0], m_lat[1])
        h_ctx = modulate(rms_norm(x_ctx, g[0]), m_ctx[0], m_ctx[1])
        if layer % 2 == 0:
            y_lat, y_ctx = conv_window_mixer(h_lat, h_ctx, w_in_even[i], conv_w[i], sink[i], w_out_even[i],
                                             rope64, need_ctx)
        else:
            y_lat, y_ctx = mla_diff_mixer(h_lat, h_ctx, w_in_odd[i], mla_q_norm_g[i], mla_kv_norm_g[i],
                                          w_q_up[i], w_kv_up[i], diff_lambda[i], diff_subln_g[i], w_out_odd[i],
                                          rope32, rope64, diff_lambda_init(layer), need_ctx)
        x = x + m_lat[2] * rms_norm(y_lat, g[1])
        f_in = modulate(rms_norm(x, g[2]), m_lat[3], m_lat[4]).reshape(Bn * S, D)
        if need_ctx:
            x_ctx = x_ctx + m_ctx[2] * rms_norm(y_ctx, g[1])
            f_ctx = modulate(rms_norm(x_ctx, g[2]), m_ctx[3], m_ctx[4]).reshape(Bn * L, D)
            f_in = jnp.concatenate([f_in, f_ctx], axis=0)
        if layer % 2 == 0:
            f = swiglu(f_in, w_ff_gu[i], w_ff_down[i])
        else:
            f = moe_swiglu(f_in, w_router[i], w_exp_gu[i], w_exp_down[i])
        x = x + m_lat[5] * rms_norm(f[:Bn * S].reshape(Bn, S, D), g[3])
        if need_ctx:
            x_ctx = x_ctx + m_ctx[5] * rms_norm(f[Bn * S:].reshape(Bn, L, D), g[3])
    return x
```

```python
import functools
import math

import jax
import jax.numpy as jnp
from jax import lax
from jax.experimental import pallas as pl
from jax.experimental.pallas import tpu as pltpu

F32 = jnp.float32
BF = jnp.bfloat16

EPS = 1e-6
ROPE_THETA = 10000.0
GRID_W = 64
BLOCK = 128
N_MOD = 6
LANES = 128
MOD_ROWS = 16

CONV_WIDTH = 512
SWA_HEADS = 8
SWA_KV_HEADS = 2
SWA_HEAD_DIM = 64
MLA_HEADS = 8
MLA_Q_RANK = 256
MLA_KV_RANK = 128
MLA_NOPE = 64
MLA_ROPE = 32
MLA_V = 64
DIFF_HEADS = 4
DIFF_DIM = 64
DIFF_V = 2 * DIFF_DIM
TOP_K = 2

NEG = -1e30


def _tile(n, pref):
    if n <= pref:
        return n
    t = pref - pref % LANES
    while t >= LANES:
        if n % t == 0:
            return t
        t -= LANES
    raise ValueError((n, pref))


def _cparams(sem, vmem_mb=None):
    kw = dict(dimension_semantics=sem)
    if vmem_mb is not None:
        kw["vmem_limit_bytes"] = vmem_mb << 20
    return pltpu.CompilerParams(**kw)


def _dot(a, b):
    return jnp.dot(a, b, preferred_element_type=F32)


def _dot_nt(a, b):
    return lax.dot_general(a, b, (((1,), (1,)), ((), ())), preferred_element_type=F32)


def _rms(x, g):
    return x * lax.rsqrt(jnp.mean(x * x, axis=-1, keepdims=True) + EPS) * g


def _norm_mod(x, g, shift, scale):
    return _rms(x, g) * (1.0 + scale) + shift


def _silu(x):
    return x / (1.0 + jnp.exp(-x))


def _rope(x, cos, sa, sb, half):
    out = []
    for c in range(x.shape[1] // LANES):
        xc = x[:, c * LANES:(c + 1) * LANES]
        out.append(xc * cos + pltpu.roll(xc, LANES - half, 1) * sa + pltpu.roll(xc, half, 1) * sb)
    return out[0] if len(out) == 1 else jnp.concatenate(out, axis=1)


def _mod_kernel(cc_ref, w_ref, b_ref, o_ref):
    a = _silu(cc_ref[...]).astype(BF)
    o_ref[0] = _dot(a, w_ref[0].astype(BF)) + b_ref[0]


def _modulation(cc, w_mod, b_mod):
    depth, d, n = w_mod.shape
    tn = _tile(n, 1536)
    return pl.pallas_call(
        _mod_kernel,
        grid=(depth, n // tn),
        in_specs=[
            pl.BlockSpec((MOD_ROWS, d), lambda l, j: (0, 0)),
            pl.BlockSpec((1, d, tn), lambda l, j: (l, 0, j)),
            pl.BlockSpec((1, 1, tn), lambda l, j: (l, 0, j)),
        ],
        out_specs=pl.BlockSpec((1, MOD_ROWS, tn), lambda l, j: (l, 0, j)),
        out_shape=jax.ShapeDtypeStruct((depth, MOD_ROWS, n), F32),
        compiler_params=_cparams(("arbitrary", "arbitrary"), 40),
    )(cc, w_mod, b_mod.reshape(depth, 1, n))


class _Dims:
    def __init__(self, bn, s, l, d):
        self.bn, self.s, self.l, self.d = bn, s, l, d
        self.t_lat = bn * s
        self.t_all = bn * s + bn * l

    def mod_row(self, i, tm):
        return jnp.minimum(i * tm // self.s, self.bn)


def _mod_spec(dims, tm, col):
    return pl.BlockSpec((1, 1, dims.d), lambda i, *_: (dims.mod_row(i, tm), 0, col))


def _gain_spec(dims, k):
    return pl.BlockSpec((1, 1, dims.d), lambda i, *_: (k, 0, 0))


def _rope_spec(dims, tm):
    n_lat, per_seq = dims.t_lat // tm, dims.s // tm
    return pl.BlockSpec((tm, LANES), lambda i: (jnp.where(i < n_lat, i % per_seq, per_seq), 0))


def _full_spec(shape):
    return pl.BlockSpec(shape, lambda *_: (0,) * len(shape))


def _rope_tables(s, pad_rows, rot_dim, period, lane_off):
    axis_dim = rot_dim // 2
    half = axis_dim // 2
    inv = 1.0 / (ROPE_THETA ** (jnp.arange(0, axis_dim, 2, dtype=F32) / axis_dim))
    pos = jnp.arange(s)
    rows = (pos // GRID_W).astype(F32)[:, None]
    cols = (pos % GRID_W).astype(F32)[:, None]
    lane = jnp.arange(LANES)
    dd = lane % period - lane_off
    active = (dd >= 0) & (dd < rot_dim)
    dd = jnp.clip(dd, 0, rot_dim - 1)
    j = dd % axis_dim
    ang = jnp.where((dd // axis_dim == 0)[None, :], rows, cols) * inv[j % half][None, :]
    first = (j < half)[None, :]
    act = active[None, :]
    cos = jnp.where(act, jnp.cos(ang), 1.0)
    sin = jnp.where(act, jnp.sin(ang), 0.0)
    sa = jnp.where(first, -sin, 0.0)
    sb = jnp.where(first, 0.0, sin)
    pad = lambda t, v: jnp.concatenate([t, jnp.full((pad_rows, LANES), v, F32)], axis=0)
    return pad(cos, 1.0), pad(sa, 0.0), pad(sb, 0.0), half


EVEN_BCU = 3 * CONV_WIDTH
EVEN_Q = SWA_HEADS * SWA_HEAD_DIM
EVEN_KD = SWA_KV_HEADS * LANES


def _even_proj_kernel(x_ref, g_ref, sh_ref, sc_ref, w_ref, cos_ref, sa_ref, sb_ref,
                      bcu_ref, q_ref, k_ref, v_ref, *, half, q_scale):
    h = _norm_mod(x_ref[...], g_ref[0], sh_ref[0], sc_ref[0]).astype(BF)
    c0, c1, c2 = EVEN_BCU, EVEN_BCU + EVEN_Q, EVEN_BCU + EVEN_Q + EVEN_KD
    bcu_ref[...] = _dot(h, w_ref[:, :c0]).astype(BF)
    qk = _rope(_dot(h, w_ref[:, c0:c2]), cos_ref[...], sa_ref[...], sb_ref[...], half)
    q_ref[...] = (qk[:, :EVEN_Q] * q_scale).astype(BF)
    k_ref[...] = qk[:, EVEN_Q:].astype(BF)
    v_ref[...] = _dot(h, w_ref[:, c2:]).astype(BF)


def _even_proj(dims, x, mods3, gains3, w, rope, tm):
    n = w.shape[1]
    cos, sa, sb, half = rope
    row = lambda width: pl.BlockSpec((tm, width), lambda i: (i, 0))
    out_w = (EVEN_BCU, EVEN_Q, EVEN_KD, EVEN_KD)
    return pl.pallas_call(
        functools.partial(_even_proj_kernel, half=half, q_scale=SWA_HEAD_DIM ** -0.5),
        grid=(dims.t_all // tm,),
        in_specs=[row(dims.d), _gain_spec(dims, 0), _mod_spec(dims, tm, 0), _mod_spec(dims, tm, 1),
                  _full_spec((dims.d, n)), _rope_spec(dims, tm), _rope_spec(dims, tm), _rope_spec(dims, tm)],
        out_specs=[row(wd) for wd in out_w],
        out_shape=[jax.ShapeDtypeStruct((dims.t_all, wd), BF) for wd in out_w],
        compiler_params=_cparams(("arbitrary",), 48),
    )(x, gains3, mods3, mods3, w, cos, sa, sb)


def _conv_kernel(b_ref, c_ref, u_ref, cp_ref, up_ref, cn_ref, un_ref, w_ref, o_ref, *, n_lat_blocks, per_seq):
    i = pl.program_id(0)
    rows = b_ref.shape[0]
    hr = cp_ref.shape[0]
    pos = i % per_seq
    is_lat = i < n_lat_blocks
    has_prev = jnp.logical_and(is_lat, pos > 0).astype(F32)
    has_next = jnp.logical_and(is_lat, pos < per_seq - 1).astype(F32)
    cu = c_ref[...].astype(F32) * u_ref[...].astype(F32)
    cu_p = (cp_ref[...].astype(F32) * up_ref[...].astype(F32))[hr - 1:hr] * has_prev
    cu_n = (cn_ref[...].astype(F32) * un_ref[...].astype(F32))[0:1] * has_next
    r = lax.broadcasted_iota(jnp.int32, cu.shape, 0)
    prev = jnp.where(r == 0, cu_p, pltpu.roll(cu, 1, 0))
    nxt = jnp.where(r == rows - 1, cu_n, pltpu.roll(cu, rows - 1, 0))
    w = w_ref[...]
    o_ref[...] = (b_ref[...].astype(F32) * (prev * w[0:1] + cu * w[1:2] + nxt * w[2:3])).astype(BF)


def _gated_conv(dims, bcu, conv_w):
    rows = dims.l
    hr = 16
    nblk = dims.t_all // rows
    per = rows // hr
    last = dims.t_all // hr - 1
    cw = CONV_WIDTH
    main = lambda col: pl.BlockSpec((rows, cw), lambda i: (i, col))
    prev = lambda col: pl.BlockSpec((hr, cw), lambda i: (jnp.maximum(i * per - 1, 0), col))
    nxt = lambda col: pl.BlockSpec((hr, cw), lambda i: (jnp.minimum((i + 1) * per, last), col))
    return pl.pallas_call(
        functools.partial(_conv_kernel, n_lat_blocks=dims.t_lat // rows, per_seq=dims.s // rows),
        grid=(nblk,),
        in_specs=[main(0), main(1), main(2), prev(1), prev(2), nxt(1), nxt(2), _full_spec(conv_w.shape)],
        out_specs=pl.BlockSpec((rows, cw), lambda i: (i, 0)),
        out_shape=jax.ShapeDtypeStruct((dims.t_all, cw), BF),
        compiler_params=_cparams(("arbitrary",)),
    )(bcu, bcu, bcu, bcu, bcu, bcu, bcu, conv_w)


def _win_attn_kernel(sink_ref, q_ref, kp_ref, ko_ref, kn_ref, kc_ref, vp_ref, vo_ref, vn_ref, vc_ref,
                     o_ref, *, nb):
    n = pl.program_id(1)
    blk = q_ref.shape[0]
    n_ctx = kc_ref.shape[0]
    is_lat = n < nb
    lo_s = jnp.where(is_lat, jnp.where(n >= 1, 0, blk), 3 * blk)
    hi_s = jnp.where(is_lat, jnp.where(n + 1 < nb, 3 * blk, 2 * blk), 0)
    shape = (blk, 3 * blk + n_ctx)
    c = lax.broadcasted_iota(jnp.int32, shape, 1)
    r = lax.broadcasted_iota(jnp.int32, shape, 0)
    ok = (c >= 3 * blk) | ((c >= r) & (c - 2 * blk <= r) & (c >= lo_s) & (c < hi_s))
    bias = jnp.where(ok, 0.0, NEG)
    kall = jnp.concatenate([kp_ref[...], ko_ref[...], kn_ref[...], kc_ref[...]], axis=0)
    vall = jnp.concatenate([vp_ref[...], vo_ref[...], vn_ref[...], vc_ref[...]], axis=0)
    lane = lax.broadcasted_iota(jnp.int32, (1, LANES), 1)
    keep = ((lane < LANES // 2).astype(F32).astype(BF), (lane >= LANES // 2).astype(F32).astype(BF))
    lo = lax.broadcasted_iota(jnp.int32, (blk, LANES), 1) < LANES // 2
    per_group = SWA_HEADS // SWA_KV_HEADS
    for pair in range(SWA_HEADS // 2):
        q2 = q_ref[:, pair * LANES:(pair + 1) * LANES]
        outs = []
        for sub in range(2):
            head = 2 * pair + sub
            g = head // per_group
            s = _dot_nt(q2 * keep[sub], kall[:, g * LANES:(g + 1) * LANES]) + bias
            sk = sink_ref[head]
            m = jnp.maximum(jnp.max(s, axis=-1, keepdims=True), sk)
            p = jnp.exp(s - m)
            den = jnp.sum(p, axis=-1, keepdims=True) + jnp.exp(sk - m)
            outs.append(_dot(p.astype(BF), vall[:, g * LANES:(g + 1) * LANES]) / den)
        o_ref[:, pair * LANES:(pair + 1) * LANES] = jnp.where(lo, outs[0], outs[1]).astype(BF)


def _win_attn(dims, q, k, v, sink):
    nb = dims.s // BLOCK
    nc = dims.l // BLOCK
    lat_blocks = dims.t_lat // BLOCK
    ctx_blk0 = dims.t_lat // dims.l

    def own(b, n):
        return jnp.where(n < nb, b * nb + n, lat_blocks + b * nc + (n - nb))

    def prev(b, n):
        return jnp.where(n < nb, b * nb + jnp.maximum(n - 1, 0), own(b, n))

    def nxt(b, n):
        return jnp.where(n < nb, b * nb + jnp.minimum(n + 1, nb - 1), own(b, n))

    kv = lambda f: pl.BlockSpec((BLOCK, EVEN_KD), lambda b, n: (f(b, n), 0))
    ctx = pl.BlockSpec((dims.l, EVEN_KD), lambda b, n: (ctx_blk0 + b, 0))
    qo = pl.BlockSpec((BLOCK, EVEN_Q), lambda b, n: (own(b, n), 0))
    return pl.pallas_call(
        functools.partial(_win_attn_kernel, nb=nb),
        grid=(dims.bn, nb + nc),
        in_specs=[pl.BlockSpec(memory_space=pltpu.SMEM), qo, kv(prev), kv(own), kv(nxt), ctx,
                  kv(prev), kv(own), kv(nxt), ctx],
        out_specs=qo,
        out_shape=jax.ShapeDtypeStruct((dims.t_all, EVEN_Q), BF),
        compiler_params=_cparams(("arbitrary", "arbitrary")),
    )(sink, q, k, k, k, k, v, v, v, v)


def _out_proj_kernel(a1_ref, a2_ref, w1_ref, w2_ref, x_ref, gate_ref, g_ref, o_ref):
    y = _dot(a1_ref[...], w1_ref[...]) + _dot(a2_ref[...], w2_ref[...])
    o_ref[...] = x_ref[...] + gate_ref[0] * _rms(y, g_ref[0])


def _out_proj(dims, a1, a2, w1, w2, x, mods3, gains3, tm):
    row = lambda width: pl.BlockSpec((tm, width), lambda i: (i, 0))
    return pl.pallas_call(
        _out_proj_kernel,
        grid=(dims.t_all // tm,),
        in_specs=[row(a1.shape[1]), row(a2.shape[1]), _full_spec(w1.shape), _full_spec(w2.shape),
                  row(dims.d), _mod_spec(dims, tm, 2), _gain_spec(dims, 1)],
        out_specs=row(dims.d),
        out_shape=jax.ShapeDtypeStruct(x.shape, F32),
        input_output_aliases={4: 0},
        compiler_params=_cparams(("arbitrary",), 40),
    )(a1, a2, w1, w2, x, mods3, gains3)


def _ffn_kernel(x_ref, g_ref, sh_ref, sc_ref, wg_ref, wu_ref, wd_ref, gate_ref, g2_ref, o_ref, h_s, acc_s):
    j = pl.program_id(1)

    @pl.when(j == 0)
    def _():
        h_s[...] = _norm_mod(x_ref[...], g_ref[0], sh_ref[0], sc_ref[0]).astype(BF)
        acc_s[...] = jnp.zeros_like(acc_s)

    h = h_s[...]
    act = (_silu(_dot(h, wg_ref[...])) * _dot(h, wu_ref[...])).astype(BF)
    acc_s[...] += _dot(act, wd_ref[...])

    @pl.when(j == pl.num_programs(1) - 1)
    def _():
        o_ref[...] = x_ref[...] + gate_ref[0] * _rms(acc_s[...], g2_ref[0])


def _dense_ffn(dims, x, mods3, gains3, w_gu, w_down, tm, tn):
    ff = w_down.shape[0]
    nj = ff // tn
    row = pl.BlockSpec((tm, dims.d), lambda i, j: (i, 0))
    return pl.pallas_call(
        _ffn_kernel,
        grid=(dims.t_all // tm, nj),
        in_specs=[row, _gain_spec(dims, 2), _mod_spec(dims, tm, 3), _mod_spec(dims, tm, 4),
                  pl.BlockSpec((dims.d, tn), lambda i, j: (0, j)),
                  pl.BlockSpec((dims.d, tn), lambda i, j: (0, nj + j)),
                  pl.BlockSpec((tn, dims.d), lambda i, j: (j, 0)),
                  _mod_spec(dims, tm, 5), _gain_spec(dims, 3)],
        out_specs=row,
        out_shape=jax.ShapeDtypeStruct(x.shape, F32),
        scratch_shapes=[pltpu.VMEM((tm, dims.d), BF), pltpu.VMEM((tm, dims.d), F32)],
        input_output_aliases={0: 0},
        compiler_params=_cparams(("arbitrary", "arbitrary"), 48),
    )(x, gains3, mods3, mods3, w_gu, w_gu, w_down, mods3, gains3)


ODD_STAGE1 = MLA_Q_RANK + MLA_KV_RANK + LANES
MLA_QK = MLA_HEADS * LANES
MLA_VW = MLA_HEADS * MLA_V
DIFF_W = DIFF_HEADS * 2 * DIFF_DIM


def _odd_proj_kernel(x_ref, g_ref, sh_ref, sc_ref, w_ref, wq_ref, wkv_ref, qg_ref, kvg_ref,
                     c32_ref, a32_ref, b32_ref, c64_ref, a64_ref, b64_ref,
                     q_ref, k_ref, v_ref, dq_ref, dk_ref, dv_ref, *, half32, half64, mla_scale, diff_scale):
    h = _norm_mod(x_ref[...], g_ref[0], sh_ref[0], sc_ref[0]).astype(BF)
    r32 = (c32_ref[...], a32_ref[...], b32_ref[...], half32)
    r64 = (c64_ref[...], a64_ref[...], b64_ref[...], half64)
    s1 = _dot(h, w_ref[:, :ODD_STAGE1])
    qn = _rms(s1[:, :MLA_Q_RANK], qg_ref[...]).astype(BF)
    kvn = _rms(s1[:, MLA_Q_RANK:MLA_Q_RANK + MLA_KV_RANK], kvg_ref[...]).astype(BF)
    kpe = _rope(s1[:, MLA_Q_RANK + MLA_KV_RANK:], *r32)
    q_ref[...] = (_rope(_dot(qn, wq_ref[...]), *r32) * mla_scale).astype(BF)
    kn = _dot(kvn, wkv_ref[:, :MLA_QK])
    k_ref[...] = (kn + jnp.concatenate([kpe] * MLA_HEADS, axis=1)).astype(BF)
    v_ref[...] = _dot(kvn, wkv_ref[:, MLA_QK:]).astype(BF)
    c0 = ODD_STAGE1
    dq_ref[...] = (_rope(_dot(h, w_ref[:, c0:c0 + DIFF_W]), *r64) * diff_scale).astype(BF)
    dk_ref[...] = _rope(_dot(h, w_ref[:, c0 + DIFF_W:c0 + 2 * DIFF_W]), *r64).astype(BF)
    dv_ref[...] = _dot(h, w_ref[:, c0 + 2 * DIFF_W:]).astype(BF)


def _odd_proj(dims, x, mods3, gains3, w, wq, wkv, qg, kvg, rope32, rope64, tm):
    row = lambda width: pl.BlockSpec((tm, width), lambda i: (i, 0))
    out_w = (MLA_QK, MLA_QK, MLA_VW, DIFF_W, DIFF_W, DIFF_HEADS * DIFF_V)
    rs = _rope_spec(dims, tm)
    return pl.pallas_call(
        functools.partial(_odd_proj_kernel, half32=rope32[3], half64=rope64[3],
                          mla_scale=(MLA_NOPE + MLA_ROPE) ** -0.5, diff_scale=DIFF_DIM ** -0.5),
        grid=(dims.t_all // tm,),
        in_specs=[row(dims.d), _gain_spec(dims, 0), _mod_spec(dims, tm, 0), _mod_spec(dims, tm, 1),
                  _full_spec(w.shape), _full_spec(wq.shape), _full_spec(wkv.shape),
                  _full_spec(qg.shape), _full_spec(kvg.shape), rs, rs, rs, rs, rs, rs],
        out_specs=[row(wd) for wd in out_w],
        out_shape=[jax.ShapeDtypeStruct((dims.t_all, wd), BF) for wd in out_w],
        compiler_params=_cparams(("arbitrary",), 48),
    )(x, gains3, mods3, mods3, w, wq, wkv, qg, kvg, *rope32[:3], *rope64[:3])


def _softmax_pv(q, k_refs, v_refs, cols_k, cols_v):
    s = [_dot_nt(q, kr[:, cols_k]) for kr in k_refs]
    m = functools.reduce(jnp.maximum, [jnp.max(x, axis=-1, keepdims=True) for x in s])
    p = [jnp.exp(x - m) for x in s]
    den = functools.reduce(jnp.add, [jnp.sum(x, axis=-1, keepdims=True) for x in p])
    o = functools.reduce(jnp.add, [_dot(x.astype(BF), vr[:, cols_v]) for x, vr in zip(p, v_refs)])
    return o / den


def _by_query_kind(nq, fn, kl_ref, kc_ref, vl_ref, vc_ref):
    qi = pl.program_id(2)

    @pl.when(qi < nq)
    def _():
        fn((kl_ref, kc_ref), (vl_ref, vc_ref))

    @pl.when(qi == nq)
    def _():
        fn((kc_ref,), (vc_ref,))


def _mla_attn_kernel(q_ref, kl_ref, kc_ref, vl_ref, vc_ref, o_ref, *, nq):
    def run(k_refs, v_refs):
        outs = [_softmax_pv(q_ref[:, sub * LANES:(sub + 1) * LANES], k_refs, v_refs,
                            slice(sub * LANES, (sub + 1) * LANES), slice(None)) for sub in range(2)]
        lo = lax.broadcasted_iota(jnp.int32, outs[0].shape, 1) < MLA_V
        o_ref[...] = jnp.where(lo, outs[0], outs[1]).astype(BF)

    _by_query_kind(nq, run, kl_ref, kc_ref, vl_ref, vc_ref)


def _diff_attn_kernel(lam_ref, sg_ref, q_ref, kl_ref, kc_ref, vl_ref, vc_ref, o_ref, *, nq, lam_init):
    def run(k_refs, v_refs):
        lp = lam_ref[...]
        lam = (jnp.exp(jnp.sum(lp[0:1] * lp[1:2], axis=-1, keepdims=True))
               - jnp.exp(jnp.sum(lp[2:3] * lp[3:4], axis=-1, keepdims=True)) + lam_init)
        lane = lax.broadcasted_iota(jnp.int32, (1, LANES), 1)
        q = q_ref[...]
        full = slice(None)
        o1 = _softmax_pv(q * (lane < DIFF_DIM).astype(F32).astype(BF), k_refs, v_refs, full, full)
        o2 = _softmax_pv(q * (lane >= DIFF_DIM).astype(F32).astype(BF), k_refs, v_refs, full, full)
        o_ref[...] = (_rms(o1 - lam * o2, sg_ref[...]) * (1.0 - lam_init)).astype(BF)

    _by_query_kind(nq, run, kl_ref, kc_ref, vl_ref, vc_ref)


def _full_attn(dims, kind, q, k, v, extra=(), lam_init=0.0):
    if kind == "mla":
        n_h, qw, vw = MLA_HEADS // 2, 2 * LANES, LANES
    else:
        n_h, qw, vw = DIFF_HEADS, LANES, LANES
    tq = dims.l
    nq = dims.s // tq
    ctx_blk0 = dims.t_lat // dims.l
    q_row = lambda b, qi: jnp.where(qi < nq, b * nq + qi, ctx_blk0 + b)
    q_spec = pl.BlockSpec((tq, qw), lambda b, hh, qi: (q_row(b, qi), hh))
    o_spec = pl.BlockSpec((tq, vw), lambda b, hh, qi: (q_row(b, qi), hh))
    lat = lambda w: pl.BlockSpec((dims.s, w), lambda b, hh, qi: (b, hh))
    ctx = lambda w: pl.BlockSpec((dims.l, w), lambda b, hh, qi: (ctx_blk0 + b, hh))
    if kind == "mla":
        body = functools.partial(_mla_attn_kernel, nq=nq)
    else:
        body = functools.partial(_diff_attn_kernel, nq=nq, lam_init=lam_init)
    return pl.pallas_call(
        body,
        grid=(dims.bn, n_h, nq + 1),
        in_specs=[_full_spec(e.shape) for e in extra] + [q_spec, lat(qw), ctx(qw), lat(vw), ctx(vw)],
        out_specs=o_spec,
        out_shape=jax.ShapeDtypeStruct((dims.t_all, n_h * vw), BF),
        compiler_params=_cparams(("arbitrary", "arbitrary", "arbitrary"), 56),
    )(*extra, q, k, k, v, v)


def _router_kernel(x_ref, g_ref, sh_ref, sc_ref, wr_ref, f_ref, r_ref, *, n_exp):
    f = _norm_mod(x_ref[...], g_ref[0], sh_ref[0], sc_ref[0])
    f_ref[...] = f
    logits = jnp.dot(f, wr_ref[...], preferred_element_type=F32, precision=lax.Precision.HIGHEST)
    lane = lax.broadcasted_iota(jnp.int32, logits.shape, 1).astype(F32)
    logits = jnp.where(lane < n_exp, logits, NEG)
    m1 = jnp.max(logits, axis=-1, keepdims=True)
    i1 = jnp.min(jnp.where(logits == m1, lane, float(LANES)), axis=-1, keepdims=True)
    rest = jnp.where(lane == i1, NEG, logits)
    m2 = jnp.max(rest, axis=-1, keepdims=True)
    i2 = jnp.min(jnp.where(rest == m2, lane, float(LANES)), axis=-1, keepdims=True)
    e2 = jnp.exp(m2 - m1)
    w1 = 1.0 / (1.0 + e2)
    w2 = e2 / (1.0 + e2)
    r_ref[...] = jnp.where(lane == 0, i1, jnp.where(lane == 1, i2, jnp.where(lane == 2, w1,
                           jnp.where(lane == 3, w2, 0.0))))


def _router(dims, x, mods3, gains3, w_router_pad, n_exp, tm):
    row = lambda width: pl.BlockSpec((tm, width), lambda i: (i, 0))
    return pl.pallas_call(
        functools.partial(_router_kernel, n_exp=n_exp),
        grid=(dims.t_all // tm,),
        in_specs=[row(dims.d), _gain_spec(dims, 2), _mod_spec(dims, tm, 3), _mod_spec(dims, tm, 4),
                  _full_spec(w_router_pad.shape)],
        out_specs=[row(dims.d), row(LANES)],
        out_shape=[jax.ShapeDtypeStruct((dims.t_all, dims.d), F32),
                   jax.ShapeDtypeStruct((dims.t_all, LANES), F32)],
        compiler_params=_cparams(("arbitrary",), 40),
    )(x, gains3, mods3, mods3, w_router_pad)


def _row_copy(src_hbm, dst, s, d, sem):
    return pltpu.make_async_copy(src_hbm.at[pl.ds(s, 1)], dst.at[pl.ds(d, 1)], sem)


def _gather_kernel(src_ref, x_hbm, o_hbm, sem, *, rows):
    base = pl.program_id(0) * rows

    def issue(r, c):
        _row_copy(x_hbm, o_hbm, src_ref[0, 0, r], base + r, sem).start()
        return c

    def drain(r, c):
        _row_copy(x_hbm, o_hbm, src_ref[0, 0, r], base + r, sem).wait()
        return c

    lax.fori_loop(0, rows, issue, 0)
    lax.fori_loop(0, rows, drain, 0)


def _gather_rows(x, src, rows):
    n = src.shape[0]
    return pl.pallas_call(
        functools.partial(_gather_kernel, rows=rows),
        grid=(n // rows,),
        in_specs=[pl.BlockSpec((1, 1, rows), lambda i: (i, 0, 0), memory_space=pltpu.SMEM),
                  pl.BlockSpec(memory_space=pl.ANY)],
        out_specs=pl.BlockSpec(memory_space=pl.ANY),
        out_shape=jax.ShapeDtypeStruct((n, x.shape[1]), x.dtype),
        scratch_shapes=[pltpu.SemaphoreType.DMA(())],
        compiler_params=_cparams(("arbitrary",)),
    )(src.reshape(n // rows, 1, rows), x)


def _expert_kernel(be_ref, nu_ref, x_ref, wg_ref, wu_ref, wd_ref, o_ref, h_s, acc_s):
    blk, j = pl.program_id(0), pl.program_id(1)
    used = blk < nu_ref[0]

    @pl.when(j == 0)
    def _():
        h_s[...] = x_ref[...].astype(BF)
        acc_s[...] = jnp.zeros_like(acc_s)

    @pl.when(used)
    def _():
        h = h_s[...]
        act = (_silu(_dot(h, wg_ref[0])) * _dot(h, wu_ref[0])).astype(BF)
        acc_s[...] += _dot(act, wd_ref[0])

    @pl.when(j == pl.num_programs(1) - 1)
    def _():
        o_ref[...] = acc_s[...]


def _experts(buf, block_e, n_used, w_gu, w_down, tm, tn):
    d = buf.shape[1]
    ff = w_down.shape[1]
    nj = ff // tn
    jj = lambda blk, j, nu: jnp.where(blk < nu[0], j, nj - 1)
    return pl.pallas_call(
        _expert_kernel,
        grid_spec=pltpu.PrefetchScalarGridSpec(
            num_scalar_prefetch=2,
            grid=(buf.shape[0] // tm, nj),
            in_specs=[pl.BlockSpec((tm, d), lambda blk, j, be, nu: (blk, 0)),
                      pl.BlockSpec((1, d, tn), lambda blk, j, be, nu: (be[blk], 0, jj(blk, j, nu))),
                      pl.BlockSpec((1, d, tn), lambda blk, j, be, nu: (be[blk], 0, nj + jj(blk, j, nu))),
                      pl.BlockSpec((1, tn, d), lambda blk, j, be, nu: (be[blk], jj(blk, j, nu), 0))],
            out_specs=pl.BlockSpec((tm, d), lambda blk, j, be, nu: (blk, 0)),
            scratch_shapes=[pltpu.VMEM((tm, d), BF), pltpu.VMEM((tm, d), F32)]),
        out_shape=jax.ShapeDtypeStruct(buf.shape, F32),
        compiler_params=_cparams(("arbitrary", "arbitrary"), 48),
    )(block_e, n_used, buf, w_gu, w_gu, w_down)


def _combine_kernel(dest_ref, r_ref, x_ref, gate_ref, g_ref, y_hbm, o_ref, buf, sem, *, tm):
    def issue(t, c):
        for k in range(TOP_K):
            _row_copy(y_hbm, buf.at[k], dest_ref[0, 0, TOP_K * t + k], t, sem).start()
        return c

    def drain(t, c):
        for k in range(TOP_K):
            _row_copy(y_hbm, buf.at[k], dest_ref[0, 0, TOP_K * t + k], t, sem).wait()
        return c

    lax.fori_loop(0, tm, issue, 0)
    lax.fori_loop(0, tm, drain, 0)
    r = r_ref[...]
    y = buf[0] * r[:, 2:3] + buf[1] * r[:, 3:4]
    o_ref[...] = x_ref[...] + gate_ref[0] * _rms(y, g_ref[0])


def _combine(dims, y, dest, route, x, mods3, gains3, tm):
    nt = dims.t_all // tm
    row = lambda width: pl.BlockSpec((tm, width), lambda i: (i, 0))
    return pl.pallas_call(
        functools.partial(_combine_kernel, tm=tm),
        grid=(nt,),
        in_specs=[pl.BlockSpec((1, 1, TOP_K * tm), lambda i: (i, 0, 0), memory_space=pltpu.SMEM),
                  row(LANES), row(dims.d), _mod_spec(dims, tm, 5), _gain_spec(dims, 3),
                  pl.BlockSpec(memory_space=pl.ANY)],
        out_specs=row(dims.d),
        out_shape=jax.ShapeDtypeStruct(x.shape, F32),
        scratch_shapes=[pltpu.VMEM((TOP_K, tm, dims.d), F32), pltpu.SemaphoreType.DMA(())],
        input_output_aliases={2: 0},
        compiler_params=_cparams(("arbitrary",)),
    )(dest.reshape(nt, 1, TOP_K * tm), route, x, mods3, gains3, y)


def _routing_tables(route, n_exp, tm_e):
    t = route.shape[0]
    a = t * TOP_K
    flat_e = route[:, :TOP_K].astype(jnp.int32).reshape(a)
    onehot = (flat_e[:, None] == jnp.arange(n_exp)[None, :]).astype(jnp.int32)
    csum = jnp.cumsum(onehot, axis=0)
    rank = jnp.take_along_axis(csum, flat_e[:, None], axis=1)[:, 0] - 1
    counts = csum[-1]
    padded = (counts + tm_e - 1) // tm_e * tm_e
    ends = jnp.cumsum(padded)
    dest = (ends - padded)[flat_e] + rank
    n_blocks = -(-(a + n_exp * (tm_e - 1)) // tm_e)
    src = jnp.zeros((n_blocks * tm_e,), jnp.int32).at[dest].set(jnp.arange(a, dtype=jnp.int32) // TOP_K)
    block_e = jnp.clip(jnp.searchsorted(ends, jnp.arange(n_blocks) * tm_e, side="right"), 0, n_exp - 1)
    n_used = (ends[-1] // tm_e).reshape(1)
    return dest.astype(jnp.int32), src, block_e.astype(jnp.int32), n_used.astype(jnp.int32)


def _even_weights(w_in):
    kw = SWA_KV_HEADS * SWA_HEAD_DIM
    c0 = EVEN_BCU + EVEN_Q
    wk, wv = w_in[:, c0:c0 + kw], w_in[:, c0 + kw:c0 + 2 * kw]
    dup = lambda w: jnp.concatenate(
        [w[:, g * SWA_HEAD_DIM:(g + 1) * SWA_HEAD_DIM] for g in range(SWA_KV_HEADS) for _ in range(2)], axis=1)
    return jnp.concatenate([w_in[:, :c0], dup(wk), dup(wv)], axis=1).astype(BF)


def _odd_weights(w_in, w_q_up, w_kv_up):
    d = w_in.shape[0]
    c = 0
    mq = w_in[:, c:c + MLA_Q_RANK]; c += MLA_Q_RANK
    dq = w_in[:, c:c + DIFF_W]; c += DIFF_W
    kvd = w_in[:, c:c + MLA_KV_RANK]; c += MLA_KV_RANK
    kpe = w_in[:, c:c + MLA_ROPE]; c += MLA_ROPE
    dk = w_in[:, c:c + DIFF_W]; c += DIFF_W
    dv = w_in[:, c:]
    pad_tail = LANES - MLA_NOPE - MLA_ROPE
    kpe_chunk = jnp.concatenate([jnp.zeros((d, MLA_NOPE), F32), kpe, jnp.zeros((d, pad_tail), F32)], axis=1)
    w = jnp.concatenate([mq, kvd, kpe_chunk, dq, dk, dv], axis=1).astype(BF)
    qh = w_q_up.reshape(MLA_Q_RANK, MLA_HEADS, MLA_NOPE + MLA_ROPE)
    wq = jnp.pad(qh, ((0, 0), (0, 0), (0, pad_tail))).reshape(MLA_Q_RANK, MLA_QK).astype(BF)
    kvh = w_kv_up.reshape(MLA_KV_RANK, MLA_HEADS, MLA_NOPE + MLA_V)
    wk = jnp.pad(kvh[:, :, :MLA_NOPE], ((0, 0), (0, 0), (0, LANES - MLA_NOPE))).reshape(MLA_KV_RANK, MLA_QK)
    wv = kvh[:, :, MLA_NOPE:].reshape(MLA_KV_RANK, MLA_VW)
    return w, wq, jnp.concatenate([wk, wv], axis=1).astype(BF)


def kernel(x, c, ctx, c_ctx, w_mod, b_mod, norm_g, w_in_even, conv_w, sink, w_out_even, w_in_odd, mla_q_norm_g,
           mla_kv_norm_g, w_q_up, w_kv_up, diff_lambda, diff_subln_g, w_out_odd, w_ff_gu, w_ff_down, w_router,
           w_exp_gu, w_exp_down):
    bn, s, d = x.shape
    l = ctx.shape[1]
    depth = w_mod.shape[0]
    n_exp = w_router.shape[-1]
    dims = _Dims(bn, s, l, d)
    assert bn < MOD_ROWS and s % l == 0 and l % BLOCK == 0 and s % GRID_W == 0

    tm = _tile(math.gcd(s, bn * l), 512)
    tm_ffn = _tile(math.gcd(s, bn * l), 1024)
    tm_e = 1024 if dims.t_all >= 8192 else 256

    cc = jnp.zeros((MOD_ROWS, d), F32).at[:bn].set(c).at[bn].set(c_ctx)
    mods = _modulation(cc, w_mod, b_mod)
    rope64 = _rope_tables(s, tm, SWA_HEAD_DIM, SWA_HEAD_DIM, 0)
    rope32 = _rope_tables(s, tm, MLA_ROPE, LANES, MLA_NOPE)

    xa = jnp.concatenate([x.reshape(bn * s, d), ctx.reshape(bn * l, d)], axis=0)
    for layer in range(depth):
        i = layer // 2
        mods3 = mods[layer].reshape(MOD_ROWS, 1, N_MOD * d)
        gains3 = norm_g[layer].reshape(4, 1, d)
        if layer % 2 == 0:
            bcu, q, k, v = _even_proj(dims, xa, mods3, gains3, _even_weights(w_in_even[i]), rope64, tm)
            conv = _gated_conv(dims, bcu, conv_w[i])
            attn = _win_attn(dims, q, k, v, sink[i])
            wo = w_out_even[i].astype(BF)
            xa = _out_proj(dims, conv, attn, wo[:CONV_WIDTH], wo[CONV_WIDTH:], xa, mods3, gains3, tm)
            xa = _dense_ffn(dims, xa, mods3, gains3, w_ff_gu[i].astype(BF), w_ff_down[i].astype(BF),
                            tm_ffn, _tile(w_ff_down.shape[1], 256))
        else:
            w, wq, wkv = _odd_weights(w_in_odd[i], w_q_up[i], w_kv_up[i])
            qm, km, vm, dq, dk, dv = _odd_proj(dims, xa, mods3, gains3, w, wq, wkv,
                                               mla_q_norm_g[i].reshape(1, -1), mla_kv_norm_g[i].reshape(1, -1),
                                               rope32, rope64, tm)
            lam_init = 0.8 - 0.6 * math.exp(-0.3 * layer)
            extra = (diff_lambda[i], diff_subln_g[i].reshape(1, -1))
            o_m = _full_attn(dims, "mla", qm, km, vm)
            o_d = _full_attn(dims, "diff", dq, dk, dv, extra, lam_init)
            wo = w_out_odd[i].astype(BF)
            xa = _out_proj(dims, o_m, o_d, wo[:MLA_VW], wo[MLA_VW:], xa, mods3, gains3, tm)
            wr = jnp.pad(w_router[i], ((0, 0), (0, LANES - n_exp)))
            f_in, route = _router(dims, xa, mods3, gains3, wr, n_exp, tm)
            dest, src, block_e, n_used = _routing_tables(route, n_exp, tm_e)
            buf = _gather_rows(f_in, src, tm_e)
            y = _experts(buf, block_e, n_used, w_exp_gu[i].astype(BF), w_exp_down[i].astype(BF), tm_e,
                         _tile(w_exp_down.shape[2], 512))
            xa = _combine(dims, y, dest, route, xa, mods3, gains3, tm)
    return xa[:bn * s].reshape(bn, s, d)
```

```python
import functools
import math

import jax
import jax.numpy as jnp
from jax import lax
from jax.experimental import pallas as pl
from jax.experimental.pallas import tpu as pltpu

F32 = jnp.float32
BF = jnp.bfloat16

EPS = 1e-6
ROPE_THETA = 10000.0
GRID_W = 64
BLOCK = 128
N_MOD = 6
LANES = 128
MOD_ROWS = 16

CONV_WIDTH = 512
SWA_HEADS = 8
SWA_KV_HEADS = 2
SWA_HEAD_DIM = 64
MLA_HEADS = 8
MLA_Q_RANK = 256
MLA_KV_RANK = 128
MLA_NOPE = 64
MLA_ROPE = 32
MLA_V = 64
DIFF_HEADS = 4
DIFF_DIM = 64
DIFF_V = 2 * DIFF_DIM
TOP_K = 2

NEG = -1e30


def _tile(n, pref):
    if n <= pref:
        return n
    t = pref - pref % LANES
    while t >= LANES:
        if n % t == 0:
            return t
        t -= LANES
    raise ValueError((n, pref))


def _cparams(sem, vmem_mb=None):
    kw = dict(dimension_semantics=sem)
    if vmem_mb is not None:
        kw["vmem_limit_bytes"] = vmem_mb << 20
    return pltpu.CompilerParams(**kw)


def _dot(a, b):
    return jnp.dot(a, b, preferred_element_type=F32)


def _dot_nt(a, b):
    return lax.dot_general(a, b, (((1,), (1,)), ((), ())), preferred_element_type=F32)


def _rms(x, g):
    return x * lax.rsqrt(jnp.mean(x * x, axis=-1, keepdims=True) + EPS) * g


def _norm_mod(x, g, shift, scale):
    return _rms(x, g) * (1.0 + scale) + shift


def _silu(x):
    return x / (1.0 + jnp.exp(-x))


def _rope(x, cos, sa, sb, half):
    out = []
    for c in range(x.shape[1] // LANES):
        xc = x[:, c * LANES:(c + 1) * LANES]
        out.append(xc * cos + pltpu.roll(xc, LANES - half, 1) * sa + pltpu.roll(xc, half, 1) * sb)
    return out[0] if len(out) == 1 else jnp.concatenate(out, axis=1)


def _mod_kernel(cc_ref, w_ref, b_ref, o_ref):
    a = _silu(cc_ref[...]).astype(BF)
    o_ref[0] = _dot(a, w_ref[0].astype(BF)) + b_ref[0]


def _modulation(cc, w_mod, b_mod):
    depth, d, n = w_mod.shape
    tn = _tile(n, 1536)
    return pl.pallas_call(
        _mod_kernel,
        grid=(depth, n // tn),
        in_specs=[
            pl.BlockSpec((MOD_ROWS, d), lambda l, j: (0, 0)),
            pl.BlockSpec((1, d, tn), lambda l, j: (l, 0, j)),
            pl.BlockSpec((1, 1, tn), lambda l, j: (l, 0, j)),
        ],
        out_specs=pl.BlockSpec((1, MOD_ROWS, tn), lambda l, j: (l, 0, j)),
        out_shape=jax.ShapeDtypeStruct((depth, MOD_ROWS, n), F32),
        compiler_params=_cparams(("arbitrary", "arbitrary"), 40),
    )(cc, w_mod, b_mod.reshape(depth, 1, n))


class _Dims:
    def __init__(self, bn, s, l, d):
        self.bn, self.s, self.l, self.d = bn, s, l, d
        self.t_lat = bn * s
        self.t_all = bn * s + bn * l

    def mod_row(self, i, tm):
        return jnp.minimum(i * tm // self.s, self.bn)


def _mod_spec(dims, tm, col):
    return pl.BlockSpec((1, 1, dims.d), lambda i, *_: (dims.mod_row(i, tm), 0, col))


def _gain_spec(dims, k):
    return pl.BlockSpec((1, 1, dims.d), lambda i, *_: (k, 0, 0))


def _rope_spec(dims, tm):
    n_lat, per_seq = dims.t_lat // tm, dims.s // tm
    return pl.BlockSpec((tm, LANES), lambda i: (jnp.where(i < n_lat, i % per_seq, per_seq), 0))


def _full_spec(shape):
    return pl.BlockSpec(shape, lambda *_: (0,) * len(shape))


def _rope_tables(s, pad_rows, rot_dim, period, lane_off):
    axis_dim = rot_dim // 2
    half = axis_dim // 2
    inv = 1.0 / (ROPE_THETA ** (jnp.arange(0, axis_dim, 2, dtype=F32) / axis_dim))
    pos = jnp.arange(s)
    rows = (pos // GRID_W).astype(F32)[:, None]
    cols = (pos % GRID_W).astype(F32)[:, None]
    lane = jnp.arange(LANES)
    dd = lane % period - lane_off
    active = (dd >= 0) & (dd < rot_dim)
    dd = jnp.clip(dd, 0, rot_dim - 1)
    j = dd % axis_dim
    ang = jnp.where((dd // axis_dim == 0)[None, :], rows, cols) * inv[j % half][None, :]
    first = (j < half)[None, :]
    act = active[None, :]
    cos = jnp.where(act, jnp.cos(ang), 1.0)
    sin = jnp.where(act, jnp.sin(ang), 0.0)
    sa = jnp.where(first, -sin, 0.0)
    sb = jnp.where(first, 0.0, sin)
    pad = lambda t, v: jnp.concatenate([t, jnp.full((pad_rows, LANES), v, F32)], axis=0)
    return pad(cos, 1.0), pad(sa, 0.0), pad(sb, 0.0), half


EVEN_BCU = 3 * CONV_WIDTH
EVEN_Q = SWA_HEADS * SWA_HEAD_DIM
EVEN_KD = SWA_KV_HEADS * LANES


def _even_proj_kernel(x_ref, g_ref, sh_ref, sc_ref, w_ref, cos_ref, sa_ref, sb_ref,
                      bcu_ref, q_ref, k_ref, v_ref, *, half, q_scale):
    h = _norm_mod(x_ref[...], g_ref[0], sh_ref[0], sc_ref[0]).astype(BF)
    c0, c1, c2 = EVEN_BCU, EVEN_BCU + EVEN_Q, EVEN_BCU + EVEN_Q + EVEN_KD
    bcu_ref[...] = _dot(h, w_ref[:, :c0]).astype(BF)
    qk = _rope(_dot(h, w_ref[:, c0:c2]), cos_ref[...], sa_ref[...], sb_ref[...], half)
    q_ref[...] = (qk[:, :EVEN_Q] * q_scale).astype(BF)
    k_ref[...] = qk[:, EVEN_Q:].astype(BF)
    v_ref[...] = _dot(h, w_ref[:, c2:]).astype(BF)


def _even_proj(dims, x, mods3, gains3, w, rope, tm):
    n = w.shape[1]
    cos, sa, sb, half = rope
    row = lambda width: pl.BlockSpec((tm, width), lambda i: (i, 0))
    out_w = (EVEN_BCU, EVEN_Q, EVEN_KD, EVEN_KD)
    return pl.pallas_call(
        functools.partial(_even_proj_kernel, half=half, q_scale=SWA_HEAD_DIM ** -0.5),
        grid=(dims.t_all // tm,),
        in_specs=[row(dims.d), _gain_spec(dims, 0), _mod_spec(dims, tm, 0), _mod_spec(dims, tm, 1),
                  _full_spec((dims.d, n)), _rope_spec(dims, tm), _rope_spec(dims, tm), _rope_spec(dims, tm)],
        out_specs=[row(wd) for wd in out_w],
        out_shape=[jax.ShapeDtypeStruct((dims.t_all, wd), BF) for wd in out_w],
        compiler_params=_cparams(("arbitrary",), 48),
    )(x, gains3, mods3, mods3, w, cos, sa, sb)


def _conv_kernel(b_ref, c_ref, u_ref, cp_ref, up_ref, cn_ref, un_ref, w_ref, o_ref, *, n_lat_blocks, per_seq):
    i = pl.program_id(0)
    rows = b_ref.shape[0]
    hr = cp_ref.shape[0]
    pos = i % per_seq
    is_lat = i < n_lat_blocks
    has_prev = jnp.logical_and(is_lat, pos > 0).astype(F32)
    has_next = jnp.logical_and(is_lat, pos < per_seq - 1).astype(F32)
    cu = c_ref[...].astype(F32) * u_ref[...].astype(F32)
    cu_p = (cp_ref[...].astype(F32) * up_ref[...].astype(F32))[hr - 1:hr] * has_prev
    cu_n = (cn_ref[...].astype(F32) * un_ref[...].astype(F32))[0:1] * has_next
    r = lax.broadcasted_iota(jnp.int32, cu.shape, 0)
    prev = jnp.where(r == 0, cu_p, pltpu.roll(cu, 1, 0))
    nxt = jnp.where(r == rows - 1, cu_n, pltpu.roll(cu, rows - 1, 0))
    w = w_ref[...]
    o_ref[...] = (b_ref[...].astype(F32) * (prev * w[0:1] + cu * w[1:2] + nxt * w[2:3])).astype(BF)


def _gated_conv(dims, bcu, conv_w):
    rows = dims.l
    hr = 16
    nblk = dims.t_all // rows
    per = rows // hr
    last = dims.t_all // hr - 1
    cw = CONV_WIDTH
    main = lambda col: pl.BlockSpec((rows, cw), lambda i: (i, col))
    prev = lambda col: pl.BlockSpec((hr, cw), lambda i: (jnp.maximum(i * per - 1, 0), col))
    nxt = lambda col: pl.BlockSpec((hr, cw), lambda i: (jnp.minimum((i + 1) * per, last), col))
    return pl.pallas_call(
        functools.partial(_conv_kernel, n_lat_blocks=dims.t_lat // rows, per_seq=dims.s // rows),
        grid=(nblk,),
        in_specs=[main(0), main(1), main(2), prev(1), prev(2), nxt(1), nxt(2), _full_spec(conv_w.shape)],
        out_specs=pl.BlockSpec((rows, cw), lambda i: (i, 0)),
        out_shape=jax.ShapeDtypeStruct((dims.t_all, cw), BF),
        compiler_params=_cparams(("arbitrary",)),
    )(bcu, bcu, bcu, bcu, bcu, bcu, bcu, conv_w)


def _win_attn_kernel(sink_ref, q_ref, kp_ref, ko_ref, kn_ref, kc_ref, vp_ref, vo_ref, vn_ref, vc_ref,
                     o_ref, *, nb):
    n = pl.program_id(1)
    blk = q_ref.shape[0]
    n_ctx = kc_ref.shape[0]
    is_lat = n < nb
    lo_s = jnp.where(is_lat, jnp.where(n >= 1, 0, blk), 3 * blk)
    hi_s = jnp.where(is_lat, jnp.where(n + 1 < nb, 3 * blk, 2 * blk), 0)
    shape = (blk, 3 * blk + n_ctx)
    c = lax.broadcasted_iota(jnp.int32, shape, 1)
    r = lax.broadcasted_iota(jnp.int32, shape, 0)
    ok = (c >= 3 * blk) | ((c >= r) & (c - 2 * blk <= r) & (c >= lo_s) & (c < hi_s))
    bias = jnp.where(ok, 0.0, NEG)
    kall = jnp.concatenate([kp_ref[...], ko_ref[...], kn_ref[...], kc_ref[...]], axis=0)
    vall = jnp.concatenate([vp_ref[...], vo_ref[...], vn_ref[...], vc_ref[...]], axis=0)
    lane = lax.broadcasted_iota(jnp.int32, (1, LANES), 1)
    keep = ((lane < LANES // 2).astype(F32).astype(BF), (lane >= LANES // 2).astype(F32).astype(BF))
    lo = lax.broadcasted_iota(jnp.int32, (blk, LANES), 1) < LANES // 2
    per_group = SWA_HEADS // SWA_KV_HEADS
    for pair in range(SWA_HEADS // 2):
        q2 = q_ref[:, pair * LANES:(pair + 1) * LANES]
        outs = []
        for sub in range(2):
            head = 2 * pair + sub
            g = head // per_group
            s = _dot_nt(q2 * keep[sub], kall[:, g * LANES:(g + 1) * LANES]) + bias
            sk = sink_ref[head]
            m = jnp.maximum(jnp.max(s, axis=-1, keepdims=True), sk)
            p = jnp.exp(s - m)
            den = jnp.sum(p, axis=-1, keepdims=True) + jnp.exp(sk - m)
            outs.append(_dot(p.astype(BF), vall[:, g * LANES:(g + 1) * LANES]) / den)
        o_ref[:, pair * LANES:(pair + 1) * LANES] = jnp.where(lo, outs[0], outs[1]).astype(BF)


def _win_attn(dims, q, k, v, sink):
    nb = dims.s // BLOCK
    nc = dims.l // BLOCK
    lat_blocks = dims.t_lat // BLOCK
    ctx_blk0 = dims.t_lat // dims.l

    def own(b, n):
        return jnp.where(n < nb, b * nb + n, lat_blocks + b * nc + (n - nb))

    def prev(b, n):
        return jnp.where(n < nb, b * nb + jnp.maximum(n - 1, 0), own(b, n))

    def nxt(b, n):
        return jnp.where(n < nb, b * nb + jnp.minimum(n + 1, nb - 1), own(b, n))

    kv = lambda f: pl.BlockSpec((BLOCK, EVEN_KD), lambda b, n: (f(b, n), 0))
    ctx = pl.BlockSpec((dims.l, EVEN_KD), lambda b, n: (ctx_blk0 + b, 0))
    qo = pl.BlockSpec((BLOCK, EVEN_Q), lambda b, n: (own(b, n), 0))
    return pl.pallas_call(
        functools.partial(_win_attn_kernel, nb=nb),
        grid=(dims.bn, nb + nc),
        in_specs=[pl.BlockSpec(memory_space=pltpu.SMEM), qo, kv(prev), kv(own), kv(nxt), ctx,
                  kv(prev), kv(own), kv(nxt), ctx],
        out_specs=qo,
        out_shape=jax.ShapeDtypeStruct((dims.t_all, EVEN_Q), BF),
        compiler_params=_cparams(("arbitrary", "arbitrary")),
    )(sink, q, k, k, k, k, v, v, v, v)


def _out_proj_kernel(a1_ref, a2_ref, w1_ref, w2_ref, x_ref, gate_ref, g_ref, o_ref):
    y = _dot(a1_ref[...], w1_ref[...]) + _dot(a2_ref[...], w2_ref[...])
    o_ref[...] = x_ref[...] + gate_ref[0] * _rms(y, g_ref[0])


def _out_proj(dims, a1, a2, w1, w2, x, mods3, gains3, tm):
    row = lambda width: pl.BlockSpec((tm, width), lambda i: (i, 0))
    return pl.pallas_call(
        _out_proj_kernel,
        grid=(dims.t_all // tm,),
        in_specs=[row(a1.shape[1]), row(a2.shape[1]), _full_spec(w1.shape), _full_spec(w2.shape),
                  row(dims.d), _mod_spec(dims, tm, 2), _gain_spec(dims, 1)],
        out_specs=row(dims.d),
        out_shape=jax.ShapeDtypeStruct(x.shape, F32),
        input_output_aliases={4: 0},
        compiler_params=_cparams(("arbitrary",), 40),
    )(a1, a2, w1, w2, x, mods3, gains3)


def _ffn_kernel(x_ref, g_ref, sh_ref, sc_ref, wg_ref, wu_ref, wd_ref, gate_ref, g2_ref, o_ref, h_s, acc_s):
    j = pl.program_id(1)

    @pl.when(j == 0)
    def _():
        h_s[...] = _norm_mod(x_ref[...], g_ref[0], sh_ref[0], sc_ref[0]).astype(BF)
        acc_s[...] = jnp.zeros_like(acc_s)

    h = h_s[...]
    act = (_silu(_dot(h, wg_ref[...])) * _dot(h, wu_ref[...])).astype(BF)
    acc_s[...] += _dot(act, wd_ref[...])

    @pl.when(j == pl.num_programs(1) - 1)
    def _():
        o_ref[...] = x_ref[...] + gate_ref[0] * _rms(acc_s[...], g2_ref[0])


def _dense_ffn(dims, x, mods3, gains3, w_gu, w_down, tm, tn):
    ff = w_down.shape[0]
    nj = ff // tn
    row = pl.BlockSpec((tm, dims.d), lambda i, j: (i, 0))
    return pl.pallas_call(
        _ffn_kernel,
        grid=(dims.t_all // tm, nj),
        in_specs=[row, _gain_spec(dims, 2), _mod_spec(dims, tm, 3), _mod_spec(dims, tm, 4),
                  pl.BlockSpec((dims.d, tn), lambda i, j: (0, j)),
                  pl.BlockSpec((dims.d, tn), lambda i, j: (0, nj + j)),
                  pl.BlockSpec((tn, dims.d), lambda i, j: (j, 0)),
                  _mod_spec(dims, tm, 5), _gain_spec(dims, 3)],
        out_specs=row,
        out_shape=jax.ShapeDtypeStruct(x.shape, F32),
        scratch_shapes=[pltpu.VMEM((tm, dims.d), BF), pltpu.VMEM((tm, dims.d), F32)],
        input_output_aliases={0: 0},
        compiler_params=_cparams(("arbitrary", "arbitrary"), 48),
    )(x, gains3, mods3, mods3, w_gu, w_gu, w_down, mods3, gains3)


ODD_STAGE1 = MLA_Q_RANK + MLA_KV_RANK + LANES
MLA_QK = MLA_HEADS * LANES
MLA_VW = MLA_HEADS * MLA_V
DIFF_W = DIFF_HEADS * 2 * DIFF_DIM


def _odd_proj_kernel(x_ref, g_ref, sh_ref, sc_ref, w_ref, wq_ref, wkv_ref, qg_ref, kvg_ref,
                     c32_ref, a32_ref, b32_ref, c64_ref, a64_ref, b64_ref,
                     q_ref, k_ref, v_ref, dq_ref, dk_ref, dv_ref, *, half32, half64, mla_scale, diff_scale):
    h = _norm_mod(x_ref[...], g_ref[0], sh_ref[0], sc_ref[0]).astype(BF)
    r32 = (c32_ref[...], a32_ref[...], b32_ref[...], half32)
    r64 = (c64_ref[...], a64_ref[...], b64_ref[...], half64)
    s1 = _dot(h, w_ref[:, :ODD_STAGE1])
    qn = _rms(s1[:, :MLA_Q_RANK], qg_ref[...]).astype(BF)
    kvn = _rms(s1[:, MLA_Q_RANK:MLA_Q_RANK + MLA_KV_RANK], kvg_ref[...]).astype(BF)
    kpe = _rope(s1[:, MLA_Q_RANK + MLA_KV_RANK:], *r32)
    q_ref[...] = (_rope(_dot(qn, wq_ref[...]), *r32) * mla_scale).astype(BF)
    kn = _dot(kvn, wkv_ref[:, :MLA_QK])
    k_ref[...] = (kn + jnp.concatenate([kpe] * MLA_HEADS, axis=1)).astype(BF)
    v_ref[...] = _dot(kvn, wkv_ref[:, MLA_QK:]).astype(BF)
    c0 = ODD_STAGE1
    dq_ref[...] = (_rope(_dot(h, w_ref[:, c0:c0 + DIFF_W]), *r64) * diff_scale).astype(BF)
    dk_ref[...] = _rope(_dot(h, w_ref[:, c0 + DIFF_W:c0 + 2 * DIFF_W]), *r64).astype(BF)
    dv_ref[...] = _dot(h, w_ref[:, c0 + 2 * DIFF_W:]).astype(BF)


def _odd_proj(dims, x, mods3, gains3, w, wq, wkv, qg, kvg, rope32, rope64, tm):
    row = lambda width: pl.BlockSpec((tm, width), lambda i: (i, 0))
    out_w = (MLA_QK, MLA_QK, MLA_VW, DIFF_W, DIFF_W, DIFF_HEADS * DIFF_V)
    rs = _rope_spec(dims, tm)
    return pl.pallas_call(
        functools.partial(_odd_proj_kernel, half32=rope32[3], half64=rope64[3],
                          mla_scale=(MLA_NOPE + MLA_ROPE) ** -0.5, diff_scale=DIFF_DIM ** -0.5),
        grid=(dims.t_all // tm,),
        in_specs=[row(dims.d), _gain_spec(dims, 0), _mod_spec(dims, tm, 0), _mod_spec(dims, tm, 1),
                  _full_spec(w.shape), _full_spec(wq.shape), _full_spec(wkv.shape),
                  _full_spec(qg.shape), _full_spec(kvg.shape), rs, rs, rs, rs, rs, rs],
        out_specs=[row(wd) for wd in out_w],
        out_shape=[jax.ShapeDtypeStruct((dims.t_all, wd), BF) for wd in out_w],
        compiler_params=_cparams(("arbitrary",), 48),
    )(x, gains3, mods3, mods3, w, wq, wkv, qg, kvg, *rope32[:3], *rope64[:3])


def _softmax_pv(q, k_refs, v_refs, cols_k, cols_v):
    s = [_dot_nt(q, kr[:, cols_k]) for kr in k_refs]
    m = functools.reduce(jnp.maximum, [jnp.max(x, axis=-1, keepdims=True) for x in s])
    p = [jnp.exp(x - m) for x in s]
    den = functools.reduce(jnp.add, [jnp.sum(x, axis=-1, keepdims=True) for x in p])
    o = functools.reduce(jnp.add, [_dot(x.astype(BF), vr[:, cols_v]) for x, vr in zip(p, v_refs)])
    return o / den


def _by_query_kind(nq, fn, kl_ref, kc_ref, vl_ref, vc_ref):
    qi = pl.program_id(2)

    @pl.when(qi < nq)
    def _():
        fn((kl_ref, kc_ref), (vl_ref, vc_ref))

    @pl.when(qi == nq)
    def _():
        fn((kc_ref,), (vc_ref,))


def _mla_attn_kernel(q_ref, kl_ref, kc_ref, vl_ref, vc_ref, o_ref, *, nq):
    def run(k_refs, v_refs):
        outs = [_softmax_pv(q_ref[:, sub * LANES:(sub + 1) * LANES], k_refs, v_refs,
                            slice(sub * LANES, (sub + 1) * LANES), slice(None)) for sub in range(2)]
        lo = lax.broadcasted_iota(jnp.int32, outs[0].shape, 1) < MLA_V
        o_ref[...] = jnp.where(lo, outs[0], outs[1]).astype(BF)

    _by_query_kind(nq, run, kl_ref, kc_ref, vl_ref, vc_ref)


def _diff_attn_kernel(lam_ref, sg_ref, q_ref, kl_ref, kc_ref, vl_ref, vc_ref, o_ref, *, nq, lam_init):
    def run(k_refs, v_refs):
        lp = lam_ref[...]
        lam = (jnp.exp(jnp.sum(lp[0:1] * lp[1:2], axis=-1, keepdims=True))
               - jnp.exp(jnp.sum(lp[2:3] * lp[3:4], axis=-1, keepdims=True)) + lam_init)
        lane = lax.broadcasted_iota(jnp.int32, (1, LANES), 1)
        q = q_ref[...]
        full = slice(None)
        o1 = _softmax_pv(q * (lane < DIFF_DIM).astype(F32).astype(BF), k_refs, v_refs, full, full)
        o2 = _softmax_pv(q * (lane >= DIFF_DIM).astype(F32).astype(BF), k_refs, v_refs, full, full)
        o_ref[...] = (_rms(o1 - lam * o2, sg_ref[...]) * (1.0 - lam_init)).astype(BF)

    _by_query_kind(nq, run, kl_ref, kc_ref, vl_ref, vc_ref)


def _full_attn(dims, kind, q, k, v, extra=(), lam_init=0.0):
    if kind == "mla":
        n_h, qw, vw = MLA_HEADS // 2, 2 * LANES, LANES
    else:
        n_h, qw, vw = DIFF_HEADS, LANES, LANES
    tq = dims.l
    nq = dims.s // tq
    ctx_blk0 = dims.t_lat // dims.l
    q_row = lambda b, qi: jnp.where(qi < nq, b * nq + qi, ctx_blk0 + b)
    q_spec = pl.BlockSpec((tq, qw), lambda b, hh, qi: (q_row(b, qi), hh))
    o_spec = pl.BlockSpec((tq, vw), lambda b, hh, qi: (q_row(b, qi), hh))
    lat = lambda w: pl.BlockSpec((dims.s, w), lambda b, hh, qi: (b, hh))
    ctx = lambda w: pl.BlockSpec((dims.l, w), lambda b, hh, qi: (ctx_blk0 + b, hh))
    if kind == "mla":
        body = functools.partial(_mla_attn_kernel, nq=nq)
    else:
        body = functools.partial(_diff_attn_kernel, nq=nq, lam_init=lam_init)
    return pl.pallas_call(
        body,
        grid=(dims.bn, n_h, nq + 1),
        in_specs=[_full_spec(e.shape) for e in extra] + [q_spec, lat(qw), ctx(qw), lat(vw), ctx(vw)],
        out_specs=o_spec,
        out_shape=jax.ShapeDtypeStruct((dims.t_all, n_h * vw), BF),
        compiler_params=_cparams(("arbitrary", "arbitrary", "arbitrary"), 56),
    )(*extra, q, k, k, v, v)


def _router_kernel(x_ref, g_ref, sh_ref, sc_ref, wr_ref, f_ref, r_ref, *, n_exp):
    f = _norm_mod(x_ref[...], g_ref[0], sh_ref[0], sc_ref[0])
    f_ref[...] = f
    logits = jnp.dot(f, wr_ref[...], preferred_element_type=F32, precision=lax.Precision.HIGHEST)
    lane = lax.broadcasted_iota(jnp.int32, logits.shape, 1).astype(F32)
    logits = jnp.where(lane < n_exp, logits, NEG)
    m1 = jnp.max(logits, axis=-1, keepdims=True)
    i1 = jnp.min(jnp.where(logits == m1, lane, float(LANES)), axis=-1, keepdims=True)
    rest = jnp.where(lane == i1, NEG, logits)
    m2 = jnp.max(rest, axis=-1, keepdims=True)
    i2 = jnp.min(jnp.where(rest == m2, lane, float(LANES)), axis=-1, keepdims=True)
    e2 = jnp.exp(m2 - m1)
    w1 = 1.0 / (1.0 + e2)
    w2 = e2 / (1.0 + e2)
    r_ref[...] = jnp.where(lane == 0, i1, jnp.where(lane == 1, i2, jnp.where(lane == 2, w1,
                           jnp.where(lane == 3, w2, 0.0))))


def _router(dims, x, mods3, gains3, w_router_pad, n_exp, tm):
    row = lambda width: pl.BlockSpec((tm, width), lambda i: (i, 0))
    return pl.pallas_call(
        functools.partial(_router_kernel, n_exp=n_exp),
        grid=(dims.t_all // tm,),
        in_specs=[row(dims.d), _gain_spec(dims, 2), _mod_spec(dims, tm, 3), _mod_spec(dims, tm, 4),
                  _full_spec(w_router_pad.shape)],
        out_specs=[row(dims.d), row(LANES)],
        out_shape=[jax.ShapeDtypeStruct((dims.t_all, dims.d), F32),
                   jax.ShapeDtypeStruct((dims.t_all, LANES), F32)],
        compiler_params=_cparams(("arbitrary",), 40),
    )(x, gains3, mods3, mods3, w_router_pad)


def _row_copy(src_hbm, dst, s, d, sem):
    return pltpu.make_async_copy(src_hbm.at[pl.ds(s, 1)], dst.at[pl.ds(d, 1)], sem)


def _expert_kernel(be_ref, nu_ref, src_cur, src_nxt, x_hbm, wg_ref, wu_ref, wd_ref, o_ref,
                   xbuf, sem, h_s, acc_s, *, tm):
    blk, j = pl.program_id(0), pl.program_id(1)
    n_used = nu_ref[0]
    used = blk < n_used
    slot = blk % 2

    def start_gather(src_ref, dst_slot):
        def issue(r, c):
            _row_copy(x_hbm, xbuf.at[dst_slot], src_ref[0, 0, r], r, sem.at[dst_slot]).start()
            return c

        lax.fori_loop(0, tm, issue, 0, unroll=8)

    @pl.when(j == 0)
    def _():
        @pl.when(blk == 0)
        def _():
            start_gather(src_cur, 0)

        @pl.when(used)
        def _():
            pltpu.make_async_copy(x_hbm.at[pl.ds(0, tm)], xbuf.at[slot], sem.at[slot]).wait()
            h_s[...] = xbuf[slot].astype(BF)

        @pl.when(blk + 1 < n_used)
        def _():
            start_gather(src_nxt, 1 - slot)

        acc_s[...] = jnp.zeros_like(acc_s)

    @pl.when(used)
    def _():
        h = h_s[...]
        act = (_silu(_dot(h, wg_ref[0])) * _dot(h, wu_ref[0])).astype(BF)
        acc_s[...] += _dot(act, wd_ref[0])

    @pl.when(j == pl.num_programs(1) - 1)
    def _():
        o_ref[...] = acc_s[...]


def _experts(f_in, src, block_e, n_used, w_gu, w_down, tm, tn):
    d = f_in.shape[1]
    ff = w_down.shape[1]
    nj = ff // tn
    nblk = src.shape[0] // tm
    jj = lambda blk, j, nu: jnp.where(blk < nu[0], j, nj - 1)
    src3 = src.reshape(nblk, 1, tm)
    return pl.pallas_call(
        functools.partial(_expert_kernel, tm=tm),
        grid_spec=pltpu.PrefetchScalarGridSpec(
            num_scalar_prefetch=2,
            grid=(nblk, nj),
            in_specs=[pl.BlockSpec((1, 1, tm), lambda blk, j, be, nu: (blk, 0, 0), memory_space=pltpu.SMEM),
                      pl.BlockSpec((1, 1, tm), lambda blk, j, be, nu: (jnp.minimum(blk + 1, nblk - 1), 0, 0),
                                   memory_space=pltpu.SMEM),
                      pl.BlockSpec(memory_space=pl.ANY),
                      pl.BlockSpec((1, d, tn), lambda blk, j, be, nu: (be[blk], 0, jj(blk, j, nu))),
                      pl.BlockSpec((1, d, tn), lambda blk, j, be, nu: (be[blk], 0, nj + jj(blk, j, nu))),
                      pl.BlockSpec((1, tn, d), lambda blk, j, be, nu: (be[blk], jj(blk, j, nu), 0))],
            out_specs=pl.BlockSpec((tm, d), lambda blk, j, be, nu: (blk, 0)),
            scratch_shapes=[pltpu.VMEM((2, tm, d), F32), pltpu.SemaphoreType.DMA((2,)),
                            pltpu.VMEM((tm, d), BF), pltpu.VMEM((tm, d), F32)]),
        out_shape=jax.ShapeDtypeStruct((nblk * tm, d), F32),
        compiler_params=_cparams(("arbitrary", "arbitrary"), 56),
    )(block_e, n_used, src3, src3, f_in, w_gu, w_gu, w_down)


def _combine_kernel(dest_ref, r_ref, x_ref, gate_ref, g_ref, y_hbm, o_ref, buf, sem, *, tm):
    def issue(t, c):
        for k in range(TOP_K):
            _row_copy(y_hbm, buf.at[k], dest_ref[0, 0, TOP_K * t + k], t, sem.at[k]).start()
        return c

    lax.fori_loop(0, tm, issue, 0, unroll=4)
    for k in range(TOP_K):
        pltpu.make_async_copy(y_hbm.at[pl.ds(0, tm)], buf.at[k], sem.at[k]).wait()
    r = r_ref[...]
    y = buf[0] * r[:, 2:3] + buf[1] * r[:, 3:4]
    o_ref[...] = x_ref[...] + gate_ref[0] * _rms(y, g_ref[0])


def _combine(dims, y, dest, route, x, mods3, gains3, tm):
    nt = dims.t_all // tm
    row = lambda width: pl.BlockSpec((tm, width), lambda i: (i, 0))
    return pl.pallas_call(
        functools.partial(_combine_kernel, tm=tm),
        grid=(nt,),
        in_specs=[pl.BlockSpec((1, 1, TOP_K * tm), lambda i: (i, 0, 0), memory_space=pltpu.SMEM),
                  row(LANES), row(dims.d), _mod_spec(dims, tm, 5), _gain_spec(dims, 3),
                  pl.BlockSpec(memory_space=pl.ANY)],
        out_specs=row(dims.d),
        out_shape=jax.ShapeDtypeStruct(x.shape, F32),
        scratch_shapes=[pltpu.VMEM((TOP_K, tm, dims.d), F32), pltpu.SemaphoreType.DMA((TOP_K,))],
        input_output_aliases={2: 0},
        compiler_params=_cparams(("arbitrary",)),
    )(dest.reshape(nt, 1, TOP_K * tm), route, x, mods3, gains3, y)


def _routing_tables(route, n_exp, tm_e):
    t = route.shape[0]
    a = t * TOP_K
    flat_e = route[:, :TOP_K].astype(jnp.int32).reshape(a)
    onehot = (flat_e[:, None] == jnp.arange(n_exp)[None, :]).astype(jnp.int32)
    csum = jnp.cumsum(onehot, axis=0)
    rank = jnp.take_along_axis(csum, flat_e[:, None], axis=1)[:, 0] - 1
    counts = csum[-1]
    padded = (counts + tm_e - 1) // tm_e * tm_e
    ends = jnp.cumsum(padded)
    dest = (ends - padded)[flat_e] + rank
    n_blocks = -(-(a + n_exp * (tm_e - 1)) // tm_e)
    src = jnp.zeros((n_blocks * tm_e,), jnp.int32).at[dest].set(jnp.arange(a, dtype=jnp.int32) // TOP_K)
    block_start = jnp.arange(n_blocks, dtype=jnp.int32) * tm_e
    block_e = jnp.minimum(jnp.sum((ends[None, :] <= block_start[:, None]).astype(jnp.int32), axis=1), n_exp - 1)
    n_used = (ends[-1] // tm_e).reshape(1)
    return dest.astype(jnp.int32), src, block_e.astype(jnp.int32), n_used.astype(jnp.int32)


def _even_weights(w_in):
    kw = SWA_KV_HEADS * SWA_HEAD_DIM
    c0 = EVEN_BCU + EVEN_Q
    wk, wv = w_in[:, c0:c0 + kw], w_in[:, c0 + kw:c0 + 2 * kw]
    dup = lambda w: jnp.concatenate(
        [w[:, g * SWA_HEAD_DIM:(g + 1) * SWA_HEAD_DIM] for g in range(SWA_KV_HEADS) for _ in range(2)], axis=1)
    return jnp.concatenate([w_in[:, :c0], dup(wk), dup(wv)], axis=1).astype(BF)


def _odd_weights(w_in, w_q_up, w_kv_up):
    d = w_in.shape[0]
    c = 0
    mq = w_in[:, c:c + MLA_Q_RANK]; c += MLA_Q_RANK
    dq = w_in[:, c:c + DIFF_W]; c += DIFF_W
    kvd = w_in[:, c:c + MLA_KV_RANK]; c += MLA_KV_RANK
    kpe = w_in[:, c:c + MLA_ROPE]; c += MLA_ROPE
    dk = w_in[:, c:c + DIFF_W]; c += DIFF_W
    dv = w_in[:, c:]
    pad_tail = LANES - MLA_NOPE - MLA_ROPE
    kpe_chunk = jnp.concatenate([jnp.zeros((d, MLA_NOPE), F32), kpe, jnp.zeros((d, pad_tail), F32)], axis=1)
    w = jnp.concatenate([mq, kvd, kpe_chunk, dq, dk, dv], axis=1).astype(BF)
    qh = w_q_up.reshape(MLA_Q_RANK, MLA_HEADS, MLA_NOPE + MLA_ROPE)
    wq = jnp.pad(qh, ((0, 0), (0, 0), (0, pad_tail))).reshape(MLA_Q_RANK, MLA_QK).astype(BF)
    kvh = w_kv_up.reshape(MLA_KV_RANK, MLA_HEADS, MLA_NOPE + MLA_V)
    wk = jnp.pad(kvh[:, :, :MLA_NOPE], ((0, 0), (0, 0), (0, LANES - MLA_NOPE))).reshape(MLA_KV_RANK, MLA_QK)
    wv = kvh[:, :, MLA_NOPE:].reshape(MLA_KV_RANK, MLA_VW)
    return w, wq, jnp.concatenate([wk, wv], axis=1).astype(BF)


def kernel(x, c, ctx, c_ctx, w_mod, b_mod, norm_g, w_in_even, conv_w, sink, w_out_even, w_in_odd, mla_q_norm_g,
           mla_kv_norm_g, w_q_up, w_kv_up, diff_lambda, diff_subln_g, w_out_odd, w_ff_gu, w_ff_down, w_router,
           w_exp_gu, w_exp_down):
    bn, s, d = x.shape
    l = ctx.shape[1]
    depth = w_mod.shape[0]
    n_exp = w_router.shape[-1]
    dims = _Dims(bn, s, l, d)
    assert bn < MOD_ROWS and s % l == 0 and l % BLOCK == 0 and s % GRID_W == 0

    tm = _tile(math.gcd(s, bn * l), 512)
    tm_ffn = _tile(math.gcd(s, bn * l), 1024)
    tm_e = 1024 if dims.t_all >= 8192 else 256

    cc = jnp.zeros((MOD_ROWS, d), F32).at[:bn].set(c).at[bn].set(c_ctx)
    mods = _modulation(cc, w_mod, b_mod)
    rope64 = _rope_tables(s, tm, SWA_HEAD_DIM, SWA_HEAD_DIM, 0)
    rope32 = _rope_tables(s, tm, MLA_ROPE, LANES, MLA_NOPE)

    xa = jnp.concatenate([x.reshape(bn * s, d), ctx.reshape(bn * l, d)], axis=0)
    for layer in range(depth):
        i = layer // 2
        mods3 = mods[layer].reshape(MOD_ROWS, 1, N_MOD * d)
        gains3 = norm_g[layer].reshape(4, 1, d)
        if layer % 2 == 0:
            bcu, q, k, v = _even_proj(dims, xa, mods3, gains3, _even_weights(w_in_even[i]), rope64, tm)
            conv = _gated_conv(dims, bcu, conv_w[i])
            attn = _win_attn(dims, q, k, v, sink[i])
            wo = w_out_even[i].astype(BF)
            xa = _out_proj(dims, conv, attn, wo[:CONV_WIDTH], wo[CONV_WIDTH:], xa, mods3, gains3, tm)
            xa = _dense_ffn(dims, xa, mods3, gains3, w_ff_gu[i].astype(BF), w_ff_down[i].astype(BF),
                            tm_ffn, _tile(w_ff_down.shape[1], 256))
        else:
            w, wq, wkv = _odd_weights(w_in_odd[i], w_q_up[i], w_kv_up[i])
            qm, km, vm, dq, dk, dv = _odd_proj(dims, xa, mods3, gains3, w, wq, wkv,
                                               mla_q_norm_g[i].reshape(1, -1), mla_kv_norm_g[i].reshape(1, -1),
                                               rope32, rope64, tm)
            lam_init = 0.8 - 0.6 * math.exp(-0.3 * layer)
            extra = (diff_lambda[i], diff_subln_g[i].reshape(1, -1))
            o_m = _full_attn(dims, "mla", qm, km, vm)
            o_d = _full_attn(dims, "diff", dq, dk, dv, extra, lam_init)
            wo = w_out_odd[i].astype(BF)
            xa = _out_proj(dims, o_m, o_d, wo[:MLA_VW], wo[MLA_VW:], xa, mods3, gains3, tm)
            wr = jnp.pad(w_router[i], ((0, 0), (0, LANES - n_exp)))
            f_in, route = _router(dims, xa, mods3, gains3, wr, n_exp, tm)
            dest, src, block_e, n_used = _routing_tables(route, n_exp, tm_e)
            y = _experts(f_in, src, block_e, n_used, w_exp_gu[i].astype(BF), w_exp_down[i].astype(BF), tm_e,
                         _tile(w_exp_down.shape[2], 512))
            xa = _combine(dims, y, dest, route, xa, mods3, gains3, tm)
    return xa[:bn * s].reshape(bn, s, d)
```

```python
import functools
import math

import jax
import jax.numpy as jnp
from jax import lax
from jax.experimental import pallas as pl
from jax.experimental.pallas import tpu as pltpu

F32 = jnp.float32
BF = jnp.bfloat16

EPS = 1e-6
ROPE_THETA = 10000.0
GRID_W = 64
BLOCK = 128
N_MOD = 6
LANES = 128
MOD_ROWS = 16

CONV_WIDTH = 512
SWA_HEADS = 8
SWA_KV_HEADS = 2
SWA_HEAD_DIM = 64
MLA_HEADS = 8
MLA_Q_RANK = 256
MLA_KV_RANK = 128
MLA_NOPE = 64
MLA_ROPE = 32
MLA_V = 64
DIFF_HEADS = 4
DIFF_DIM = 64
DIFF_V = 2 * DIFF_DIM
TOP_K = 2

NEG = -1e30
LOG2E = math.log2(math.e)
KEY_CHUNK = 1024


def _tile(n, pref):
    if n <= pref:
        return n
    t = pref - pref % LANES
    while t >= LANES:
        if n % t == 0:
            return t
        t -= LANES
    raise ValueError((n, pref))


def _cparams(sem, vmem_mb=None):
    kw = dict(dimension_semantics=sem)
    if vmem_mb is not None:
        kw["vmem_limit_bytes"] = vmem_mb << 20
    return pltpu.CompilerParams(**kw)


def _dot(a, b):
    return jnp.dot(a, b, preferred_element_type=F32)


def _dot_nt(a, b):
    return lax.dot_general(a, b, (((1,), (1,)), ((), ())), preferred_element_type=F32)


def _rms(x, g):
    return x * lax.rsqrt(jnp.mean(x * x, axis=-1, keepdims=True) + EPS) * g


def _norm_mod(x, g, shift, scale):
    return _rms(x, g) * (1.0 + scale) + shift


def _silu(x):
    return x / (1.0 + jnp.exp(-x))


def _rope(x, cos, sa, sb, half):
    out = []
    for c in range(x.shape[1] // LANES):
        xc = x[:, c * LANES:(c + 1) * LANES]
        out.append(xc * cos + pltpu.roll(xc, LANES - half, 1) * sa + pltpu.roll(xc, half, 1) * sb)
    return out[0] if len(out) == 1 else jnp.concatenate(out, axis=1)


def _mod_kernel(cc_ref, w_ref, b_ref, o_ref):
    a = _silu(cc_ref[...]).astype(BF)
    o_ref[0] = _dot(a, w_ref[0].astype(BF)) + b_ref[0]


def _modulation(cc, w_mod, b_mod):
    depth, d, n = w_mod.shape
    tn = _tile(n, 1536)
    return pl.pallas_call(
        _mod_kernel,
        grid=(depth, n // tn),
        in_specs=[
            pl.BlockSpec((MOD_ROWS, d), lambda l, j: (0, 0)),
            pl.BlockSpec((1, d, tn), lambda l, j: (l, 0, j)),
            pl.BlockSpec((1, 1, tn), lambda l, j: (l, 0, j)),
        ],
        out_specs=pl.BlockSpec((1, MOD_ROWS, tn), lambda l, j: (l, 0, j)),
        out_shape=jax.ShapeDtypeStruct((depth, MOD_ROWS, n), F32),
        compiler_params=_cparams(("arbitrary", "arbitrary"), 40),
    )(cc, w_mod, b_mod.reshape(depth, 1, n))


class _Dims:
    def __init__(self, bn, s, l, d):
        self.bn, self.s, self.l, self.d = bn, s, l, d
        self.t_lat = bn * s
        self.t_all = bn * s + bn * l

    def mod_row(self, i, tm):
        return jnp.minimum(i * tm // self.s, self.bn)


def _mod_spec(dims, tm, col):
    return pl.BlockSpec((1, 1, dims.d), lambda i, *_: (dims.mod_row(i, tm), 0, col))


def _gain_spec(dims, k):
    return pl.BlockSpec((1, 1, dims.d), lambda i, *_: (k, 0, 0))


def _rope_spec(dims, tm):
    n_lat, per_seq = dims.t_lat // tm, dims.s // tm
    return pl.BlockSpec((tm, LANES), lambda i: (jnp.where(i < n_lat, i % per_seq, per_seq), 0))


def _full_spec(shape):
    return pl.BlockSpec(shape, lambda *_: (0,) * len(shape))


def _rope_tables(s, pad_rows, rot_dim, period, lane_off):
    axis_dim = rot_dim // 2
    half = axis_dim // 2
    inv = 1.0 / (ROPE_THETA ** (jnp.arange(0, axis_dim, 2, dtype=F32) / axis_dim))
    pos = jnp.arange(s)
    rows = (pos // GRID_W).astype(F32)[:, None]
    cols = (pos % GRID_W).astype(F32)[:, None]
    lane = jnp.arange(LANES)
    dd = lane % period - lane_off
    active = (dd >= 0) & (dd < rot_dim)
    dd = jnp.clip(dd, 0, rot_dim - 1)
    j = dd % axis_dim
    ang = jnp.where((dd // axis_dim == 0)[None, :], rows, cols) * inv[j % half][None, :]
    first = (j < half)[None, :]
    act = active[None, :]
    cos = jnp.where(act, jnp.cos(ang), 1.0)
    sin = jnp.where(act, jnp.sin(ang), 0.0)
    sa = jnp.where(first, -sin, 0.0)
    sb = jnp.where(first, 0.0, sin)
    pad = lambda t, v: jnp.concatenate([t, jnp.full((pad_rows, LANES), v, F32)], axis=0)
    return pad(cos, 1.0), pad(sa, 0.0), pad(sb, 0.0), half


EVEN_BCU = 3 * CONV_WIDTH
EVEN_Q = SWA_HEADS * SWA_HEAD_DIM
EVEN_KD = SWA_KV_HEADS * LANES


def _even_proj_kernel(x_ref, g_ref, sh_ref, sc_ref, w_ref, cos_ref, sa_ref, sb_ref,
                      bcu_ref, q_ref, k_ref, v_ref, *, half, q_scale):
    h = _norm_mod(x_ref[...], g_ref[0], sh_ref[0], sc_ref[0]).astype(BF)
    c0, c1, c2 = EVEN_BCU, EVEN_BCU + EVEN_Q, EVEN_BCU + EVEN_Q + EVEN_KD
    bcu_ref[...] = _dot(h, w_ref[:, :c0]).astype(BF)
    qk = _rope(_dot(h, w_ref[:, c0:c2]), cos_ref[...], sa_ref[...], sb_ref[...], half)
    q_ref[...] = (qk[:, :EVEN_Q] * q_scale).astype(BF)
    k_ref[...] = qk[:, EVEN_Q:].astype(BF)
    v_ref[...] = _dot(h, w_ref[:, c2:]).astype(BF)


def _even_proj(dims, x, mods3, gains3, w, rope, tm):
    n = w.shape[1]
    cos, sa, sb, half = rope
    row = lambda width: pl.BlockSpec((tm, width), lambda i: (i, 0))
    out_w = (EVEN_BCU, EVEN_Q, EVEN_KD, EVEN_KD)
    return pl.pallas_call(
        functools.partial(_even_proj_kernel, half=half, q_scale=SWA_HEAD_DIM ** -0.5 * LOG2E),
        grid=(dims.t_all // tm,),
        in_specs=[row(dims.d), _gain_spec(dims, 0), _mod_spec(dims, tm, 0), _mod_spec(dims, tm, 1),
                  _full_spec((dims.d, n)), _rope_spec(dims, tm), _rope_spec(dims, tm), _rope_spec(dims, tm)],
        out_specs=[row(wd) for wd in out_w],
        out_shape=[jax.ShapeDtypeStruct((dims.t_all, wd), BF) for wd in out_w],
        compiler_params=_cparams(("arbitrary",), 48),
    )(x, gains3, mods3, mods3, w, cos, sa, sb)


def _conv_kernel(b_ref, c_ref, u_ref, cp_ref, up_ref, cn_ref, un_ref, w_ref, o_ref, *, n_lat_blocks, per_seq):
    i = pl.program_id(0)
    rows = b_ref.shape[0]
    hr = cp_ref.shape[0]
    pos = i % per_seq
    is_lat = i < n_lat_blocks
    has_prev = jnp.logical_and(is_lat, pos > 0).astype(F32)
    has_next = jnp.logical_and(is_lat, pos < per_seq - 1).astype(F32)
    cu = c_ref[...].astype(F32) * u_ref[...].astype(F32)
    cu_p = (cp_ref[...].astype(F32) * up_ref[...].astype(F32))[hr - 1:hr] * has_prev
    cu_n = (cn_ref[...].astype(F32) * un_ref[...].astype(F32))[0:1] * has_next
    r = lax.broadcasted_iota(jnp.int32, cu.shape, 0)
    prev = jnp.where(r == 0, cu_p, pltpu.roll(cu, 1, 0))
    nxt = jnp.where(r == rows - 1, cu_n, pltpu.roll(cu, rows - 1, 0))
    w = w_ref[...]
    o_ref[...] = (b_ref[...].astype(F32) * (prev * w[0:1] + cu * w[1:2] + nxt * w[2:3])).astype(BF)


def _gated_conv(dims, bcu, conv_w):
    rows = dims.l
    hr = 16
    nblk = dims.t_all // rows
    per = rows // hr
    last = dims.t_all // hr - 1
    cw = CONV_WIDTH
    main = lambda col: pl.BlockSpec((rows, cw), lambda i: (i, col))
    prev = lambda col: pl.BlockSpec((hr, cw), lambda i: (jnp.maximum(i * per - 1, 0), col))
    nxt = lambda col: pl.BlockSpec((hr, cw), lambda i: (jnp.minimum((i + 1) * per, last), col))
    return pl.pallas_call(
        functools.partial(_conv_kernel, n_lat_blocks=dims.t_lat // rows, per_seq=dims.s // rows),
        grid=(nblk,),
        in_specs=[main(0), main(1), main(2), prev(1), prev(2), nxt(1), nxt(2), _full_spec(conv_w.shape)],
        out_specs=pl.BlockSpec((rows, cw), lambda i: (i, 0)),
        out_shape=jax.ShapeDtypeStruct((dims.t_all, cw), BF),
        compiler_params=_cparams(("arbitrary",)),
    )(bcu, bcu, bcu, bcu, bcu, bcu, bcu, conv_w)


def _win_attn_kernel(sink_ref, q_ref, kp_ref, ko_ref, kn_ref, kc_ref, vp_ref, vo_ref, vn_ref, vc_ref,
                     o_ref, *, nb):
    n = pl.program_id(1)
    blk = q_ref.shape[0]
    n_ctx = kc_ref.shape[0]
    is_lat = n < nb
    lo_s = jnp.where(is_lat, jnp.where(n >= 1, 0, blk), 3 * blk)
    hi_s = jnp.where(is_lat, jnp.where(n + 1 < nb, 3 * blk, 2 * blk), 0)
    shape = (blk, 3 * blk + n_ctx)
    c = lax.broadcasted_iota(jnp.int32, shape, 1)
    r = lax.broadcasted_iota(jnp.int32, shape, 0)
    ok = (c >= 3 * blk) | ((c >= r) & (c - 2 * blk <= r) & (c >= lo_s) & (c < hi_s))
    bias = jnp.where(ok, 0.0, NEG)
    kall = jnp.concatenate([kp_ref[...], ko_ref[...], kn_ref[...], kc_ref[...]], axis=0)
    vall = jnp.concatenate([vp_ref[...], vo_ref[...], vn_ref[...], vc_ref[...]], axis=0)
    lane = lax.broadcasted_iota(jnp.int32, (1, LANES), 1)
    keep = ((lane < LANES // 2).astype(F32).astype(BF), (lane >= LANES // 2).astype(F32).astype(BF))
    lo = lax.broadcasted_iota(jnp.int32, (blk, LANES), 1) < LANES // 2
    per_group = SWA_HEADS // SWA_KV_HEADS

    def score(head):
        pair, sub, g = head // 2, head % 2, head // per_group
        q2 = q_ref[:, pair * LANES:(pair + 1) * LANES]
        return _dot_nt(q2 * keep[sub], kall[:, g * LANES:(g + 1) * LANES]) + bias

    s_next = score(0)
    outs = []
    for head in range(SWA_HEADS):
        s = s_next
        if head + 1 < SWA_HEADS:
            s_next = score(head + 1)
        g = head // per_group
        sk = jnp.full((blk, 1), sink_ref[head], F32) * LOG2E
        m = jnp.maximum(jnp.max(s, axis=-1, keepdims=True), sk)
        p = jnp.exp2(s - m)
        den = jnp.sum(p, axis=-1, keepdims=True) + jnp.exp2(sk - m)
        outs.append(_dot(p.astype(BF), vall[:, g * LANES:(g + 1) * LANES]) / den)
        if head % 2 == 1:
            pair = head // 2
            o_ref[:, pair * LANES:(pair + 1) * LANES] = jnp.where(lo, outs[-2], outs[-1]).astype(BF)


def _win_attn(dims, q, k, v, sink):
    nb = dims.s // BLOCK
    nc = dims.l // BLOCK
    lat_blocks = dims.t_lat // BLOCK
    ctx_blk0 = dims.t_lat // dims.l

    def own(b, n):
        return jnp.where(n < nb, b * nb + n, lat_blocks + b * nc + (n - nb))

    def prev(b, n):
        return jnp.where(n < nb, b * nb + jnp.maximum(n - 1, 0), own(b, n))

    def nxt(b, n):
        return jnp.where(n < nb, b * nb + jnp.minimum(n + 1, nb - 1), own(b, n))

    kv = lambda f: pl.BlockSpec((BLOCK, EVEN_KD), lambda b, n: (f(b, n), 0))
    ctx = pl.BlockSpec((dims.l, EVEN_KD), lambda b, n: (ctx_blk0 + b, 0))
    qo = pl.BlockSpec((BLOCK, EVEN_Q), lambda b, n: (own(b, n), 0))
    return pl.pallas_call(
        functools.partial(_win_attn_kernel, nb=nb),
        grid=(dims.bn, nb + nc),
        in_specs=[pl.BlockSpec(memory_space=pltpu.SMEM), qo, kv(prev), kv(own), kv(nxt), ctx,
                  kv(prev), kv(own), kv(nxt), ctx],
        out_specs=qo,
        out_shape=jax.ShapeDtypeStruct((dims.t_all, EVEN_Q), BF),
        compiler_params=_cparams(("arbitrary", "arbitrary")),
    )(sink, q, k, k, k, k, v, v, v, v)


def _out_proj_kernel(a1_ref, a2_ref, w1_ref, w2_ref, x_ref, gate_ref, g_ref, o_ref):
    y = _dot(a1_ref[...], w1_ref[...]) + _dot(a2_ref[...], w2_ref[...])
    o_ref[...] = x_ref[...] + gate_ref[0] * _rms(y, g_ref[0])


def _out_proj(dims, a1, a2, w1, w2, x, mods3, gains3, tm):
    row = lambda width: pl.BlockSpec((tm, width), lambda i: (i, 0))
    return pl.pallas_call(
        _out_proj_kernel,
        grid=(dims.t_all // tm,),
        in_specs=[row(a1.shape[1]), row(a2.shape[1]), _full_spec(w1.shape), _full_spec(w2.shape),
                  row(dims.d), _mod_spec(dims, tm, 2), _gain_spec(dims, 1)],
        out_specs=row(dims.d),
        out_shape=jax.ShapeDtypeStruct(x.shape, F32),
        input_output_aliases={4: 0},
        compiler_params=_cparams(("arbitrary",), 40),
    )(a1, a2, w1, w2, x, mods3, gains3)


def _ffn_kernel(x_ref, g_ref, sh_ref, sc_ref, wg_ref, wu_ref, wd_ref, gate_ref, g2_ref, o_ref, h_s, acc_s):
    j = pl.program_id(1)

    @pl.when(j == 0)
    def _():
        h_s[...] = _norm_mod(x_ref[...], g_ref[0], sh_ref[0], sc_ref[0]).astype(BF)
        acc_s[...] = jnp.zeros_like(acc_s)

    h = h_s[...]
    act = (_silu(_dot(h, wg_ref[...])) * _dot(h, wu_ref[...])).astype(BF)
    acc_s[...] += _dot(act, wd_ref[...])

    @pl.when(j == pl.num_programs(1) - 1)
    def _():
        o_ref[...] = x_ref[...] + gate_ref[0] * _rms(acc_s[...], g2_ref[0])


def _dense_ffn(dims, x, mods3, gains3, w_gu, w_down, tm, tn):
    ff = w_down.shape[0]
    nj = ff // tn
    row = pl.BlockSpec((tm, dims.d), lambda i, j: (i, 0))
    return pl.pallas_call(
        _ffn_kernel,
        grid=(dims.t_all // tm, nj),
        in_specs=[row, _gain_spec(dims, 2), _mod_spec(dims, tm, 3), _mod_spec(dims, tm, 4),
                  pl.BlockSpec((dims.d, tn), lambda i, j: (0, j)),
                  pl.BlockSpec((dims.d, tn), lambda i, j: (0, nj + j)),
                  pl.BlockSpec((tn, dims.d), lambda i, j: (j, 0)),
                  _mod_spec(dims, tm, 5), _gain_spec(dims, 3)],
        out_specs=row,
        out_shape=jax.ShapeDtypeStruct(x.shape, F32),
        scratch_shapes=[pltpu.VMEM((tm, dims.d), BF), pltpu.VMEM((tm, dims.d), F32)],
        input_output_aliases={0: 0},
        compiler_params=_cparams(("arbitrary", "arbitrary"), 48),
    )(x, gains3, mods3, mods3, w_gu, w_gu, w_down, mods3, gains3)


ODD_STAGE1 = MLA_Q_RANK + MLA_KV_RANK + LANES
MLA_QK = MLA_HEADS * LANES
MLA_VW = MLA_HEADS * MLA_V
DIFF_W = DIFF_HEADS * 2 * DIFF_DIM


def _odd_proj_kernel(x_ref, g_ref, sh_ref, sc_ref, w_ref, wq_ref, wkv_ref, qg_ref, kvg_ref,
                     c32_ref, a32_ref, b32_ref, c64_ref, a64_ref, b64_ref,
                     q_ref, k_ref, v_ref, dq_ref, dk_ref, dv_ref, *, half32, half64, mla_scale, diff_scale):
    h = _norm_mod(x_ref[...], g_ref[0], sh_ref[0], sc_ref[0]).astype(BF)
    r32 = (c32_ref[...], a32_ref[...], b32_ref[...], half32)
    r64 = (c64_ref[...], a64_ref[...], b64_ref[...], half64)
    s1 = _dot(h, w_ref[:, :ODD_STAGE1])
    qn = _rms(s1[:, :MLA_Q_RANK], qg_ref[...]).astype(BF)
    kvn = _rms(s1[:, MLA_Q_RANK:MLA_Q_RANK + MLA_KV_RANK], kvg_ref[...]).astype(BF)
    kpe = _rope(s1[:, MLA_Q_RANK + MLA_KV_RANK:], *r32)
    q_ref[...] = (_rope(_dot(qn, wq_ref[...]), *r32) * mla_scale).astype(BF)
    kn = _dot(kvn, wkv_ref[:, :MLA_QK])
    k_ref[...] = (kn + jnp.concatenate([kpe] * MLA_HEADS, axis=1)).astype(BF)
    v_ref[...] = _dot(kvn, wkv_ref[:, MLA_QK:]).astype(BF)
    c0 = ODD_STAGE1
    dq_ref[...] = (_rope(_dot(h, w_ref[:, c0:c0 + DIFF_W]), *r64) * diff_scale).astype(BF)
    dk_ref[...] = _rope(_dot(h, w_ref[:, c0 + DIFF_W:c0 + 2 * DIFF_W]), *r64).astype(BF)
    dv_ref[...] = _dot(h, w_ref[:, c0 + 2 * DIFF_W:]).astype(BF)


def _odd_proj(dims, x, mods3, gains3, w, wq, wkv, qg, kvg, rope32, rope64, tm):
    row = lambda width: pl.BlockSpec((tm, width), lambda i: (i, 0))
    out_w = (MLA_QK, MLA_QK, MLA_VW, DIFF_W, DIFF_W, DIFF_HEADS * DIFF_V)
    rs = _rope_spec(dims, tm)
    return pl.pallas_call(
        functools.partial(_odd_proj_kernel, half32=rope32[3], half64=rope64[3],
                          mla_scale=(MLA_NOPE + MLA_ROPE) ** -0.5 * LOG2E, diff_scale=DIFF_DIM ** -0.5 * LOG2E),
        grid=(dims.t_all // tm,),
        in_specs=[row(dims.d), _gain_spec(dims, 0), _mod_spec(dims, tm, 0), _mod_spec(dims, tm, 1),
                  _full_spec(w.shape), _full_spec(wq.shape), _full_spec(wkv.shape),
                  _full_spec(qg.shape), _full_spec(kvg.shape), rs, rs, rs, rs, rs, rs],
        out_specs=[row(wd) for wd in out_w],
        out_shape=[jax.ShapeDtypeStruct((dims.t_all, wd), BF) for wd in out_w],
        compiler_params=_cparams(("arbitrary",), 48),
    )(x, gains3, mods3, mods3, w, wq, wkv, qg, kvg, *rope32[:3], *rope64[:3])


def _softmax_pv(streams, k_refs, v_refs):
    tasks = []
    for si, (q, cols_k, cols_v) in enumerate(streams):
        for kr, vr in zip(k_refs, v_refs):
            n = kr.shape[0]
            for c0 in range(0, n, KEY_CHUNK):
                tasks.append((si, q, kr, vr, c0, min(c0 + KEY_CHUNK, n), cols_k, cols_v))
    score = lambda t: _dot_nt(t[1], t[2][t[4]:t[5], t[6]])
    state = [None] * len(streams)
    s_next = score(tasks[0])
    for ti, t in enumerate(tasks):
        s = s_next
        if ti + 1 < len(tasks):
            s_next = score(tasks[ti + 1])
        si, _, _, vr, c0, c1, _, cols_v = t
        mc = jnp.max(s, axis=-1, keepdims=True)
        if state[si] is None:
            m_new = mc
        else:
            m, l, acc = state[si]
            m_new = jnp.maximum(m, mc)
        p = jnp.exp2(s - m_new)
        ps = jnp.sum(p, axis=-1, keepdims=True)
        pv = _dot(p.astype(BF), vr[c0:c1, cols_v])
        if state[si] is None:
            state[si] = (m_new, ps, pv)
        else:
            a = jnp.exp2(m - m_new)
            state[si] = (m_new, a * l + ps, a * acc + pv)
    return [acc / l for _, l, acc in state]


def _by_query_kind(nq, fn, kl_ref, kc_ref, vl_ref, vc_ref):
    qi = pl.program_id(2)

    @pl.when(qi < nq)
    def _():
        fn((kl_ref, kc_ref), (vl_ref, vc_ref))

    @pl.when(qi == nq)
    def _():
        fn((kc_ref,), (vc_ref,))


def _mla_attn_kernel(q_ref, kl_ref, kc_ref, vl_ref, vc_ref, o_ref, *, nq):
    def run(k_refs, v_refs):
        cols = [slice(sub * LANES, (sub + 1) * LANES) for sub in range(2)]
        outs = _softmax_pv([(q_ref[:, c], c, slice(None)) for c in cols], k_refs, v_refs)
        lo = lax.broadcasted_iota(jnp.int32, outs[0].shape, 1) < MLA_V
        o_ref[...] = jnp.where(lo, outs[0], outs[1]).astype(BF)

    _by_query_kind(nq, run, kl_ref, kc_ref, vl_ref, vc_ref)


def _diff_attn_kernel(lam_ref, sg_ref, q_ref, kl_ref, kc_ref, vl_ref, vc_ref, o_ref, *, nq, lam_init):
    def run(k_refs, v_refs):
        lp = lam_ref[...]
        lam = (jnp.exp(jnp.sum(lp[0:1] * lp[1:2], axis=-1, keepdims=True))
               - jnp.exp(jnp.sum(lp[2:3] * lp[3:4], axis=-1, keepdims=True)) + lam_init)
        lane = lax.broadcasted_iota(jnp.int32, (1, LANES), 1)
        q = q_ref[...]
        full = slice(None)
        o1, o2 = _softmax_pv([(q * (lane < DIFF_DIM).astype(F32).astype(BF), full, full),
                              (q * (lane >= DIFF_DIM).astype(F32).astype(BF), full, full)], k_refs, v_refs)
        o_ref[...] = (_rms(o1 - lam * o2, sg_ref[...]) * (1.0 - lam_init)).astype(BF)

    _by_query_kind(nq, run, kl_ref, kc_ref, vl_ref, vc_ref)


def _full_attn(dims, kind, q, k, v, extra=(), lam_init=0.0):
    if kind == "mla":
        n_h, qw, vw = MLA_HEADS // 2, 2 * LANES, LANES
    else:
        n_h, qw, vw = DIFF_HEADS, LANES, LANES
    tq = dims.l
    nq = dims.s // tq
    ctx_blk0 = dims.t_lat // dims.l
    q_row = lambda b, qi: jnp.where(qi < nq, b * nq + qi, ctx_blk0 + b)
    q_spec = pl.BlockSpec((tq, qw), lambda b, hh, qi: (q_row(b, qi), hh))
    o_spec = pl.BlockSpec((tq, vw), lambda b, hh, qi: (q_row(b, qi), hh))
    lat = lambda w: pl.BlockSpec((dims.s, w), lambda b, hh, qi: (b, hh))
    ctx = lambda w: pl.BlockSpec((dims.l, w), lambda b, hh, qi: (ctx_blk0 + b, hh))
    if kind == "mla":
        body = functools.partial(_mla_attn_kernel, nq=nq)
    else:
        body = functools.partial(_diff_attn_kernel, nq=nq, lam_init=lam_init)
    return pl.pallas_call(
        body,
        grid=(dims.bn, n_h, nq + 1),
        in_specs=[_full_spec(e.shape) for e in extra] + [q_spec, lat(qw), ctx(qw), lat(vw), ctx(vw)],
        out_specs=o_spec,
        out_shape=jax.ShapeDtypeStruct((dims.t_all, n_h * vw), BF),
        compiler_params=_cparams(("arbitrary", "arbitrary", "arbitrary"), 56),
    )(*extra, q, k, k, v, v)


def _router_kernel(x_ref, g_ref, sh_ref, sc_ref, wr_ref, f_ref, r_ref, *, n_exp):
    f = _norm_mod(x_ref[...], g_ref[0], sh_ref[0], sc_ref[0])
    f_ref[...] = f
    logits = jnp.dot(f, wr_ref[...], preferred_element_type=F32, precision=lax.Precision.HIGHEST)
    lane = lax.broadcasted_iota(jnp.int32, logits.shape, 1).astype(F32)
    logits = jnp.where(lane < n_exp, logits, NEG)
    m1 = jnp.max(logits, axis=-1, keepdims=True)
    i1 = jnp.min(jnp.where(logits == m1, lane, float(LANES)), axis=-1, keepdims=True)
    rest = jnp.where(lane == i1, NEG, logits)
    m2 = jnp.max(rest, axis=-1, keepdims=True)
    i2 = jnp.min(jnp.where(rest == m2, lane, float(LANES)), axis=-1, keepdims=True)
    e2 = jnp.exp(m2 - m1)
    w1 = 1.0 / (1.0 + e2)
    w2 = e2 / (1.0 + e2)
    r_ref[...] = jnp.where(lane == 0, i1, jnp.where(lane == 1, i2, jnp.where(lane == 2, w1,
                           jnp.where(lane == 3, w2, 0.0))))


def _router(dims, x, mods3, gains3, w_router_pad, n_exp, tm):
    row = lambda width: pl.BlockSpec((tm, width), lambda i: (i, 0))
    return pl.pallas_call(
        functools.partial(_router_kernel, n_exp=n_exp),
        grid=(dims.t_all // tm,),
        in_specs=[row(dims.d), _gain_spec(dims, 2), _mod_spec(dims, tm, 3), _mod_spec(dims, tm, 4),
                  _full_spec(w_router_pad.shape)],
        out_specs=[row(dims.d), row(LANES)],
        out_shape=[jax.ShapeDtypeStruct((dims.t_all, dims.d), F32),
                   jax.ShapeDtypeStruct((dims.t_all, LANES), F32)],
        compiler_params=_cparams(("arbitrary",), 40),
    )(x, gains3, mods3, mods3, w_router_pad)


def _row_copy(src_hbm, dst, s, d, sem):
    return pltpu.make_async_copy(src_hbm.at[pl.ds(s, 1)], dst.at[pl.ds(d, 1)], sem)


def _expert_kernel(be_ref, nu_ref, src_cur, src_nxt, x_hbm, wg_ref, wu_ref, wd_ref, o_ref,
                   xbuf, sem, h_s, acc_s, *, tm, nj):
    blk, j = pl.program_id(0), pl.program_id(1)
    nblk = pl.num_programs(0)
    n_used = nu_ref[0]
    used = blk < n_used
    slot = blk % 2
    per_step = tm // nj
    head = tm - per_step * nj

    def issue(src_ref, dst_slot, r):
        _row_copy(x_hbm, xbuf.at[dst_slot], src_ref[0, 0, r], r, sem.at[dst_slot]).start()

    def wait_slot(s):
        pltpu.make_async_copy(x_hbm.at[pl.ds(0, tm)], xbuf.at[s], sem.at[s]).wait()

    @pl.when(j == 0)
    def _():
        @pl.when(blk == 0)
        def _():
            def first(r, c):
                issue(src_cur, 0, r)
                return c

            lax.fori_loop(0, tm, first, 0, unroll=8)

        @pl.when(blk <= n_used)
        def _():
            wait_slot(slot)

        @pl.when(used)
        def _():
            h_s[...] = xbuf[slot].astype(BF)
            for r in range(head):
                issue(src_nxt, 1 - slot, r)

        acc_s[...] = jnp.zeros_like(acc_s)

    @pl.when(used)
    def _():
        for r in range(per_step):
            issue(src_nxt, 1 - slot, head + j * per_step + r)
        h = h_s[...]
        act = (_silu(_dot(h, wg_ref[0])) * _dot(h, wu_ref[0])).astype(BF)
        acc_s[...] += _dot(act, wd_ref[0])

    @pl.when(j == nj - 1)
    def _():
        o_ref[...] = acc_s[...]

        @pl.when(jnp.logical_and(blk == nblk - 1, used))
        def _():
            wait_slot(1 - slot)


def _experts(f_in, src, block_e, n_used, w_gu, w_down, tm, tn):
    d = f_in.shape[1]
    ff = w_down.shape[1]
    nj = ff // tn
    nblk = src.shape[0] // tm
    jj = lambda blk, j, nu: jnp.where(blk < nu[0], j, nj - 1)
    src3 = src.reshape(nblk, 1, tm)
    return pl.pallas_call(
        functools.partial(_expert_kernel, tm=tm, nj=nj),
        grid_spec=pltpu.PrefetchScalarGridSpec(
            num_scalar_prefetch=2,
            grid=(nblk, nj),
            in_specs=[pl.BlockSpec((1, 1, tm), lambda blk, j, be, nu: (blk, 0, 0), memory_space=pltpu.SMEM),
                      pl.BlockSpec((1, 1, tm), lambda blk, j, be, nu: (jnp.minimum(blk + 1, nblk - 1), 0, 0),
                                   memory_space=pltpu.SMEM),
                      pl.BlockSpec(memory_space=pl.ANY),
                      pl.BlockSpec((1, d, tn), lambda blk, j, be, nu: (be[blk], 0, jj(blk, j, nu))),
                      pl.BlockSpec((1, d, tn), lambda blk, j, be, nu: (be[blk], 0, nj + jj(blk, j, nu))),
                      pl.BlockSpec((1, tn, d), lambda blk, j, be, nu: (be[blk], jj(blk, j, nu), 0))],
            out_specs=pl.BlockSpec((tm, d), lambda blk, j, be, nu: (blk, 0)),
            scratch_shapes=[pltpu.VMEM((2, tm, d), F32), pltpu.SemaphoreType.DMA((2,)),
                            pltpu.VMEM((tm, d), BF), pltpu.VMEM((tm, d), F32)]),
        out_shape=jax.ShapeDtypeStruct((nblk * tm, d), F32),
        compiler_params=_cparams(("arbitrary", "arbitrary"), 56),
    )(block_e, n_used, src3, src3, f_in, w_gu, w_gu, w_down)


def _combine_kernel(dest_ref, r_ref, x_ref, gate_ref, g_ref, y_hbm, o_ref, buf, sem, *, tm):
    def issue(t, c):
        for k in range(TOP_K):
            _row_copy(y_hbm, buf.at[k], dest_ref[0, 0, TOP_K * t + k], t, sem.at[k]).start()
        return c

    lax.fori_loop(0, tm, issue, 0, unroll=4)
    for k in range(TOP_K):
        pltpu.make_async_copy(y_hbm.at[pl.ds(0, tm)], buf.at[k], sem.at[k]).wait()
    r = r_ref[...]
    y = buf[0] * r[:, 2:3] + buf[1] * r[:, 3:4]
    o_ref[...] = x_ref[...] + gate_ref[0] * _rms(y, g_ref[0])


def _combine(dims, y, dest, route, x, mods3, gains3, tm):
    nt = dims.t_all // tm
    row = lambda width: pl.BlockSpec((tm, width), lambda i: (i, 0))
    return pl.pallas_call(
        functools.partial(_combine_kernel, tm=tm),
        grid=(nt,),
        in_specs=[pl.BlockSpec((1, 1, TOP_K * tm), lambda i: (i, 0, 0), memory_space=pltpu.SMEM),
                  row(LANES), row(dims.d), _mod_spec(dims, tm, 5), _gain_spec(dims, 3),
                  pl.BlockSpec(memory_space=pl.ANY)],
        out_specs=row(dims.d),
        out_shape=jax.ShapeDtypeStruct(x.shape, F32),
        scratch_shapes=[pltpu.VMEM((TOP_K, tm, dims.d), F32), pltpu.SemaphoreType.DMA((TOP_K,))],
        input_output_aliases={2: 0},
        compiler_params=_cparams(("arbitrary",)),
    )(dest.reshape(nt, 1, TOP_K * tm), route, x, mods3, gains3, y)


def _routing_tables(route, n_exp, tm_e):
    t = route.shape[0]
    a = t * TOP_K
    flat_e = route[:, :TOP_K].astype(jnp.int32).reshape(a)
    onehot = (flat_e[:, None] == jnp.arange(n_exp)[None, :]).astype(jnp.int32)
    csum = jnp.cumsum(onehot, axis=0)
    rank = jnp.take_along_axis(csum, flat_e[:, None], axis=1)[:, 0] - 1
    counts = csum[-1]
    padded = (counts + tm_e - 1) // tm_e * tm_e
    ends = jnp.cumsum(padded)
    dest = (ends - padded)[flat_e] + rank
    n_blocks = -(-(a + n_exp * (tm_e - 1)) // tm_e)
    src = jnp.zeros((n_blocks * tm_e,), jnp.int32).at[dest].set(jnp.arange(a, dtype=jnp.int32) // TOP_K)
    block_start = jnp.arange(n_blocks, dtype=jnp.int32) * tm_e
    block_e = jnp.minimum(jnp.sum((ends[None, :] <= block_start[:, None]).astype(jnp.int32), axis=1), n_exp - 1)
    n_used = (ends[-1] // tm_e).reshape(1)
    return dest.astype(jnp.int32), src, block_e.astype(jnp.int32), n_used.astype(jnp.int32)


def _even_weights(w_in):
    kw = SWA_KV_HEADS * SWA_HEAD_DIM
    c0 = EVEN_BCU + EVEN_Q
    wk, wv = w_in[:, c0:c0 + kw], w_in[:, c0 + kw:c0 + 2 * kw]
    dup = lambda w: jnp.concatenate(
        [w[:, g * SWA_HEAD_DIM:(g + 1) * SWA_HEAD_DIM] for g in range(SWA_KV_HEADS) for _ in range(2)], axis=1)
    return jnp.concatenate([w_in[:, :c0], dup(wk), dup(wv)], axis=1).astype(BF)


def _odd_weights(w_in, w_q_up, w_kv_up):
    d = w_in.shape[0]
    c = 0
    mq = w_in[:, c:c + MLA_Q_RANK]; c += MLA_Q_RANK
    dq = w_in[:, c:c + DIFF_W]; c += DIFF_W
    kvd = w_in[:, c:c + MLA_KV_RANK]; c += MLA_KV_RANK
    kpe = w_in[:, c:c + MLA_ROPE]; c += MLA_ROPE
    dk = w_in[:, c:c + DIFF_W]; c += DIFF_W
    dv = w_in[:, c:]
    pad_tail = LANES - MLA_NOPE - MLA_ROPE
    kpe_chunk = jnp.concatenate([jnp.zeros((d, MLA_NOPE), F32), kpe, jnp.zeros((d, pad_tail), F32)], axis=1)
    w = jnp.concatenate([mq, kvd, kpe_chunk, dq, dk, dv], axis=1).astype(BF)
    qh = w_q_up.reshape(MLA_Q_RANK, MLA_HEADS, MLA_NOPE + MLA_ROPE)
    wq = jnp.pad(qh, ((0, 0), (0, 0), (0, pad_tail))).reshape(MLA_Q_RANK, MLA_QK).astype(BF)
    kvh = w_kv_up.reshape(MLA_KV_RANK, MLA_HEADS, MLA_NOPE + MLA_V)
    wk = jnp.pad(kvh[:, :, :MLA_NOPE], ((0, 0), (0, 0), (0, LANES - MLA_NOPE))).reshape(MLA_KV_RANK, MLA_QK)
    wv = kvh[:, :, MLA_NOPE:].reshape(MLA_KV_RANK, MLA_VW)
    return w, wq, jnp.concatenate([wk, wv], axis=1).astype(BF)


def kernel(x, c, ctx, c_ctx, w_mod, b_mod, norm_g, w_in_even, conv_w, sink, w_out_even, w_in_odd, mla_q_norm_g,
           mla_kv_norm_g, w_q_up, w_kv_up, diff_lambda, diff_subln_g, w_out_odd, w_ff_gu, w_ff_down, w_router,
           w_exp_gu, w_exp_down):
    bn, s, d = x.shape
    l = ctx.shape[1]
    depth = w_mod.shape[0]
    n_exp = w_router.shape[-1]
    dims = _Dims(bn, s, l, d)
    assert bn < MOD_ROWS and s % l == 0 and l % BLOCK == 0 and s % GRID_W == 0

    tm = _tile(math.gcd(s, bn * l), 512)
    tm_ffn = _tile(math.gcd(s, bn * l), 1024)
    tm_e = 1024 if dims.t_all >= 8192 else 256

    cc = jnp.zeros((MOD_ROWS, d), F32).at[:bn].set(c).at[bn].set(c_ctx)
    mods = _modulation(cc, w_mod, b_mod)
    rope64 = _rope_tables(s, tm, SWA_HEAD_DIM, SWA_HEAD_DIM, 0)
    rope32 = _rope_tables(s, tm, MLA_ROPE, LANES, MLA_NOPE)

    xa = jnp.concatenate([x.reshape(bn * s, d), ctx.reshape(bn * l, d)], axis=0)
    for layer in range(depth):
        i = layer // 2
        mods3 = mods[layer].reshape(MOD_ROWS, 1, N_MOD * d)
        gains3 = norm_g[layer].reshape(4, 1, d)
        if layer % 2 == 0:
            bcu, q, k, v = _even_proj(dims, xa, mods3, gains3, _even_weights(w_in_even[i]), rope64, tm)
            conv = _gated_conv(dims, bcu, conv_w[i])
            attn = _win_attn(dims, q, k, v, sink[i])
            wo = w_out_even[i].astype(BF)
            xa = _out_proj(dims, conv, attn, wo[:CONV_WIDTH], wo[CONV_WIDTH:], xa, mods3, gains3, tm)
            xa = _dense_ffn(dims, xa, mods3, gains3, w_ff_gu[i].astype(BF), w_ff_down[i].astype(BF),
                            tm_ffn, _tile(w_ff_down.shape[1], 256))
        else:
            w, wq, wkv = _odd_weights(w_in_odd[i], w_q_up[i], w_kv_up[i])
            qm, km, vm, dq, dk, dv = _odd_proj(dims, xa, mods3, gains3, w, wq, wkv,
                                               mla_q_norm_g[i].reshape(1, -1), mla_kv_norm_g[i].reshape(1, -1),
                                               rope32, rope64, tm)
            lam_init = 0.8 - 0.6 * math.exp(-0.3 * layer)
            extra = (diff_lambda[i], diff_subln_g[i].reshape(1, -1))
            o_m = _full_attn(dims, "mla", qm, km, vm)
            o_d = _full_attn(dims, "diff", dq, dk, dv, extra, lam_init)
            wo = w_out_odd[i].astype(BF)
            xa = _out_proj(dims, o_m, o_d, wo[:MLA_VW], wo[MLA_VW:], xa, mods3, gains3, tm)
            wr = jnp.pad(w_router[i], ((0, 0), (0, LANES - n_exp)))
            f_in, route = _router(dims, xa, mods3, gains3, wr, n_exp, tm)
            dest, src, block_e, n_used = _routing_tables(route, n_exp, tm_e)
            y = _experts(f_in, src, block_e, n_used, w_exp_gu[i].astype(BF), w_exp_down[i].astype(BF), tm_e,
                         _tile(w_exp_down.shape[2], 512))
            xa = _combine(dims, y, dest, route, xa, mods3, gains3, tm)
    return xa[:bn * s].reshape(bn, s, d)
```

```python
import functools
import math

import jax
import jax.numpy as jnp
from jax import lax
from jax.experimental import pallas as pl
from jax.experimental.pallas import tpu as pltpu

F32 = jnp.float32
BF = jnp.bfloat16

EPS = 1e-6
ROPE_THETA = 10000.0
GRID_W = 64
BLOCK = 128
N_MOD = 6
LANES = 128
MOD_ROWS = 16

CONV_WIDTH = 512
SWA_HEADS = 8
SWA_KV_HEADS = 2
SWA_HEAD_DIM = 64
MLA_HEADS = 8
MLA_Q_RANK = 256
MLA_KV_RANK = 128
MLA_NOPE = 64
MLA_ROPE = 32
MLA_V = 64
DIFF_HEADS = 4
DIFF_DIM = 64
DIFF_V = 2 * DIFF_DIM
TOP_K = 2

NEG = -1e30
LOG2E = math.log2(math.e)
KEY_CHUNK = 1024


def _tile(n, pref):
    if n <= pref:
        return n
    t = pref - pref % LANES
    while t >= LANES:
        if n % t == 0:
            return t
        t -= LANES
    raise ValueError((n, pref))


def _cparams(sem, vmem_mb=None):
    kw = dict(dimension_semantics=sem)
    if vmem_mb is not None:
        kw["vmem_limit_bytes"] = vmem_mb << 20
    return pltpu.CompilerParams(**kw)


def _dot(a, b):
    return jnp.dot(a, b, preferred_element_type=F32)


def _dot_nt(a, b):
    return lax.dot_general(a, b, (((1,), (1,)), ((), ())), preferred_element_type=F32)


def _rms(x, g):
    return x * lax.rsqrt(jnp.mean(x * x, axis=-1, keepdims=True) + EPS) * g


def _norm_mod(x, g, shift, scale):
    return _rms(x, g) * (1.0 + scale) + shift


def _silu(x):
    return x / (1.0 + jnp.exp(-x))


def _rope(x, cos, sa, sb, half):
    out = []
    for c in range(x.shape[1] // LANES):
        xc = x[:, c * LANES:(c + 1) * LANES]
        out.append(xc * cos + pltpu.roll(xc, LANES - half, 1) * sa + pltpu.roll(xc, half, 1) * sb)
    return out[0] if len(out) == 1 else jnp.concatenate(out, axis=1)


def _mod_kernel(cc_ref, w_ref, b_ref, o_ref):
    a = _silu(cc_ref[...]).astype(BF)
    o_ref[0] = _dot(a, w_ref[0].astype(BF)) + b_ref[0]


def _modulation(cc, w_mod, b_mod):
    depth, d, n = w_mod.shape
    tn = _tile(n, 1536)
    return pl.pallas_call(
        _mod_kernel,
        grid=(depth, n // tn),
        in_specs=[
            pl.BlockSpec((MOD_ROWS, d), lambda l, j: (0, 0)),
            pl.BlockSpec((1, d, tn), lambda l, j: (l, 0, j)),
            pl.BlockSpec((1, 1, tn), lambda l, j: (l, 0, j)),
        ],
        out_specs=pl.BlockSpec((1, MOD_ROWS, tn), lambda l, j: (l, 0, j)),
        out_shape=jax.ShapeDtypeStruct((depth, MOD_ROWS, n), F32),
        compiler_params=_cparams(("arbitrary", "arbitrary"), 40),
    )(cc, w_mod, b_mod.reshape(depth, 1, n))


class _Dims:
    def __init__(self, bn, s, l, d):
        self.bn, self.s, self.l, self.d = bn, s, l, d
        self.t_lat = bn * s
        self.t_all = bn * s + bn * l

    def mod_row(self, i, tm):
        return jnp.minimum(i * tm // self.s, self.bn)


def _mod_spec(dims, tm, col):
    return pl.BlockSpec((1, 1, dims.d), lambda i, *_: (dims.mod_row(i, tm), 0, col))


def _gain_spec(dims, k):
    return pl.BlockSpec((1, 1, dims.d), lambda i, *_: (k, 0, 0))


def _rope_spec(dims, tm):
    n_lat, per_seq = dims.t_lat // tm, dims.s // tm
    return pl.BlockSpec((tm, LANES), lambda i: (jnp.where(i < n_lat, i % per_seq, per_seq), 0))


def _full_spec(shape):
    return pl.BlockSpec(shape, lambda *_: (0,) * len(shape))


def _rope_tables(s, pad_rows, rot_dim, period, lane_off):
    axis_dim = rot_dim // 2
    half = axis_dim // 2
    inv = 1.0 / (ROPE_THETA ** (jnp.arange(0, axis_dim, 2, dtype=F32) / axis_dim))
    pos = jnp.arange(s)
    rows = (pos // GRID_W).astype(F32)[:, None]
    cols = (pos % GRID_W).astype(F32)[:, None]
    lane = jnp.arange(LANES)
    dd = lane % period - lane_off
    active = (dd >= 0) & (dd < rot_dim)
    dd = jnp.clip(dd, 0, rot_dim - 1)
    j = dd % axis_dim
    ang = jnp.where((dd // axis_dim == 0)[None, :], rows, cols) * inv[j % half][None, :]
    first = (j < half)[None, :]
    act = active[None, :]
    cos = jnp.where(act, jnp.cos(ang), 1.0)
    sin = jnp.where(act, jnp.sin(ang), 0.0)
    sa = jnp.where(first, -sin, 0.0)
    sb = jnp.where(first, 0.0, sin)
    pad = lambda t, v: jnp.concatenate([t, jnp.full((pad_rows, LANES), v, F32)], axis=0)
    return pad(cos, 1.0), pad(sa, 0.0), pad(sb, 0.0), half


EVEN_BCU = 3 * CONV_WIDTH
EVEN_Q = SWA_HEADS * SWA_HEAD_DIM
EVEN_KD = SWA_KV_HEADS * LANES


def _even_proj_kernel(x_ref, g_ref, sh_ref, sc_ref, w_ref, cos_ref, sa_ref, sb_ref,
                      bcu_ref, q_ref, k_ref, v_ref, *, half, q_scale):
    h = _norm_mod(x_ref[...], g_ref[0], sh_ref[0], sc_ref[0]).astype(BF)
    c0, c1, c2 = EVEN_BCU, EVEN_BCU + EVEN_Q, EVEN_BCU + EVEN_Q + EVEN_KD
    bcu_ref[...] = _dot(h, w_ref[:, :c0]).astype(BF)
    qk = _rope(_dot(h, w_ref[:, c0:c2]), cos_ref[...], sa_ref[...], sb_ref[...], half)
    q_ref[...] = (qk[:, :EVEN_Q] * q_scale).astype(BF)
    k_ref[...] = qk[:, EVEN_Q:].astype(BF)
    v_ref[...] = _dot(h, w_ref[:, c2:]).astype(BF)


def _even_proj(dims, x, mods3, gains3, w, rope, tm):
    n = w.shape[1]
    cos, sa, sb, half = rope
    row = lambda width: pl.BlockSpec((tm, width), lambda i: (i, 0))
    out_w = (EVEN_BCU, EVEN_Q, EVEN_KD, EVEN_KD)
    return pl.pallas_call(
        functools.partial(_even_proj_kernel, half=half, q_scale=SWA_HEAD_DIM ** -0.5 * LOG2E),
        grid=(dims.t_all // tm,),
        in_specs=[row(dims.d), _gain_spec(dims, 0), _mod_spec(dims, tm, 0), _mod_spec(dims, tm, 1),
                  _full_spec((dims.d, n)), _rope_spec(dims, tm), _rope_spec(dims, tm), _rope_spec(dims, tm)],
        out_specs=[row(wd) for wd in out_w],
        out_shape=[jax.ShapeDtypeStruct((dims.t_all, wd), BF) for wd in out_w],
        compiler_params=_cparams(("arbitrary",), 48),
    )(x, gains3, mods3, mods3, w, cos, sa, sb)


def _conv_kernel(b_ref, c_ref, u_ref, cp_ref, up_ref, cn_ref, un_ref, w_ref, o_ref, *, n_lat_blocks, per_seq):
    i = pl.program_id(0)
    rows = b_ref.shape[0]
    hr = cp_ref.shape[0]
    pos = i % per_seq
    is_lat = i < n_lat_blocks
    has_prev = jnp.logical_and(is_lat, pos > 0).astype(F32)
    has_next = jnp.logical_and(is_lat, pos < per_seq - 1).astype(F32)
    cu = c_ref[...].astype(F32) * u_ref[...].astype(F32)
    cu_p = (cp_ref[...].astype(F32) * up_ref[...].astype(F32))[hr - 1:hr] * has_prev
    cu_n = (cn_ref[...].astype(F32) * un_ref[...].astype(F32))[0:1] * has_next
    r = lax.broadcasted_iota(jnp.int32, cu.shape, 0)
    prev = jnp.where(r == 0, cu_p, pltpu.roll(cu, 1, 0))
    nxt = jnp.where(r == rows - 1, cu_n, pltpu.roll(cu, rows - 1, 0))
    w = w_ref[...]
    o_ref[...] = (b_ref[...].astype(F32) * (prev * w[0:1] + cu * w[1:2] + nxt * w[2:3])).astype(BF)


def _gated_conv(dims, bcu, conv_w):
    rows = dims.l
    hr = 16
    nblk = dims.t_all // rows
    per = rows // hr
    last = dims.t_all // hr - 1
    cw = CONV_WIDTH
    main = lambda col: pl.BlockSpec((rows, cw), lambda i: (i, col))
    prev = lambda col: pl.BlockSpec((hr, cw), lambda i: (jnp.maximum(i * per - 1, 0), col))
    nxt = lambda col: pl.BlockSpec((hr, cw), lambda i: (jnp.minimum((i + 1) * per, last), col))
    return pl.pallas_call(
        functools.partial(_conv_kernel, n_lat_blocks=dims.t_lat // rows, per_seq=dims.s // rows),
        grid=(nblk,),
        in_specs=[main(0), main(1), main(2), prev(1), prev(2), nxt(1), nxt(2), _full_spec(conv_w.shape)],
        out_specs=pl.BlockSpec((rows, cw), lambda i: (i, 0)),
        out_shape=jax.ShapeDtypeStruct((dims.t_all, cw), BF),
        compiler_params=_cparams(("arbitrary",)),
    )(bcu, bcu, bcu, bcu, bcu, bcu, bcu, conv_w)


def _win_attn_kernel(sink_ref, q_ref, kp_ref, ko_ref, kn_ref, kc_ref, vp_ref, vo_ref, vn_ref, vc_ref,
                     o_ref, *, nb):
    n = pl.program_id(1)
    blk = q_ref.shape[0]
    n_ctx = kc_ref.shape[0]
    is_lat = n < nb
    lo_s = jnp.where(is_lat, jnp.where(n >= 1, 0, blk), 3 * blk)
    hi_s = jnp.where(is_lat, jnp.where(n + 1 < nb, 3 * blk, 2 * blk), 0)
    shape = (blk, 3 * blk + n_ctx)
    c = lax.broadcasted_iota(jnp.int32, shape, 1)
    r = lax.broadcasted_iota(jnp.int32, shape, 0)
    ok = (c >= 3 * blk) | ((c >= r) & (c - 2 * blk <= r) & (c >= lo_s) & (c < hi_s))
    bias = jnp.where(ok, 0.0, NEG)
    kall = jnp.concatenate([kp_ref[...], ko_ref[...], kn_ref[...], kc_ref[...]], axis=0)
    vall = jnp.concatenate([vp_ref[...], vo_ref[...], vn_ref[...], vc_ref[...]], axis=0)
    lane = lax.broadcasted_iota(jnp.int32, (1, LANES), 1)
    keep = ((lane < LANES // 2).astype(F32).astype(BF), (lane >= LANES // 2).astype(F32).astype(BF))
    lo = lax.broadcasted_iota(jnp.int32, (blk, LANES), 1) < LANES // 2
    per_group = SWA_HEADS // SWA_KV_HEADS

    def score(head):
        pair, sub, g = head // 2, head % 2, head // per_group
        q2 = q_ref[:, pair * LANES:(pair + 1) * LANES]
        return _dot_nt(q2 * keep[sub], kall[:, g * LANES:(g + 1) * LANES]) + bias

    s_next = score(0)
    outs = []
    for head in range(SWA_HEADS):
        s = s_next
        if head + 1 < SWA_HEADS:
            s_next = score(head + 1)
        g = head // per_group
        sk = jnp.full((blk, 1), sink_ref[head], F32) * LOG2E
        m = jnp.maximum(jnp.max(s, axis=-1, keepdims=True), sk)
        p = jnp.exp2(s - m)
        den = jnp.sum(p, axis=-1, keepdims=True) + jnp.exp2(sk - m)
        outs.append(_dot(p.astype(BF), vall[:, g * LANES:(g + 1) * LANES]) / den)
        if head % 2 == 1:
            pair = head // 2
            o_ref[:, pair * LANES:(pair + 1) * LANES] = jnp.where(lo, outs[-2], outs[-1]).astype(BF)


def _win_attn(dims, q, k, v, sink):
    nb = dims.s // BLOCK
    nc = dims.l // BLOCK
    lat_blocks = dims.t_lat // BLOCK
    ctx_blk0 = dims.t_lat // dims.l

    def own(b, n):
        return jnp.where(n < nb, b * nb + n, lat_blocks + b * nc + (n - nb))

    def prev(b, n):
        return jnp.where(n < nb, b * nb + jnp.maximum(n - 1, 0), own(b, n))

    def nxt(b, n):
        return jnp.where(n < nb, b * nb + jnp.minimum(n + 1, nb - 1), own(b, n))

    kv = lambda f: pl.BlockSpec((BLOCK, EVEN_KD), lambda b, n: (f(b, n), 0))
    ctx = pl.BlockSpec((dims.l, EVEN_KD), lambda b, n: (ctx_blk0 + b, 0))
    qo = pl.BlockSpec((BLOCK, EVEN_Q), lambda b, n: (own(b, n), 0))
    return pl.pallas_call(
        functools.partial(_win_attn_kernel, nb=nb),
        grid=(dims.bn, nb + nc),
        in_specs=[pl.BlockSpec(memory_space=pltpu.SMEM), qo, kv(prev), kv(own), kv(nxt), ctx,
                  kv(prev), kv(own), kv(nxt), ctx],
        out_specs=qo,
        out_shape=jax.ShapeDtypeStruct((dims.t_all, EVEN_Q), BF),
        compiler_params=_cparams(("arbitrary", "arbitrary")),
    )(sink, q, k, k, k, k, v, v, v, v)


def _out_proj_kernel(a1_ref, a2_ref, w1_ref, w2_ref, x_ref, gate_ref, g_ref, o_ref):
    y = _dot(a1_ref[...], w1_ref[...]) + _dot(a2_ref[...], w2_ref[...])
    o_ref[...] = x_ref[...] + gate_ref[0] * _rms(y, g_ref[0])


def _out_proj(dims, a1, a2, w1, w2, x, mods3, gains3, tm):
    row = lambda width: pl.BlockSpec((tm, width), lambda i: (i, 0))
    return pl.pallas_call(
        _out_proj_kernel,
        grid=(a1.shape[0] // tm,),
        in_specs=[row(a1.shape[1]), row(a2.shape[1]), _full_spec(w1.shape), _full_spec(w2.shape),
                  row(dims.d), _mod_spec(dims, tm, 2), _gain_spec(dims, 1)],
        out_specs=row(dims.d),
        out_shape=jax.ShapeDtypeStruct(x.shape, F32),
        input_output_aliases={4: 0},
        compiler_params=_cparams(("arbitrary",), 40),
    )(a1, a2, w1, w2, x, mods3, gains3)


def _ffn_kernel(x_ref, g_ref, sh_ref, sc_ref, wg_ref, wu_ref, wd_ref, gate_ref, g2_ref, o_ref, h_s, acc_s):
    j = pl.program_id(1)

    @pl.when(j == 0)
    def _():
        h_s[...] = _norm_mod(x_ref[...], g_ref[0], sh_ref[0], sc_ref[0]).astype(BF)
        acc_s[...] = jnp.zeros_like(acc_s)

    h = h_s[...]
    act = (_silu(_dot(h, wg_ref[...])) * _dot(h, wu_ref[...])).astype(BF)
    acc_s[...] += _dot(act, wd_ref[...])

    @pl.when(j == pl.num_programs(1) - 1)
    def _():
        o_ref[...] = x_ref[...] + gate_ref[0] * _rms(acc_s[...], g2_ref[0])


def _dense_ffn(dims, x, mods3, gains3, w_gu, w_down, tm, tn):
    ff = w_down.shape[0]
    nj = ff // tn
    row = pl.BlockSpec((tm, dims.d), lambda i, j: (i, 0))
    return pl.pallas_call(
        _ffn_kernel,
        grid=(dims.t_all // tm, nj),
        in_specs=[row, _gain_spec(dims, 2), _mod_spec(dims, tm, 3), _mod_spec(dims, tm, 4),
                  pl.BlockSpec((dims.d, tn), lambda i, j: (0, j)),
                  pl.BlockSpec((dims.d, tn), lambda i, j: (0, nj + j)),
                  pl.BlockSpec((tn, dims.d), lambda i, j: (j, 0)),
                  _mod_spec(dims, tm, 5), _gain_spec(dims, 3)],
        out_specs=row,
        out_shape=jax.ShapeDtypeStruct(x.shape, F32),
        scratch_shapes=[pltpu.VMEM((tm, dims.d), BF), pltpu.VMEM((tm, dims.d), F32)],
        input_output_aliases={0: 0},
        compiler_params=_cparams(("arbitrary", "arbitrary"), 48),
    )(x, gains3, mods3, mods3, w_gu, w_gu, w_down, mods3, gains3)


ODD_STAGE1 = MLA_Q_RANK + MLA_KV_RANK + LANES
MLA_QK = MLA_HEADS * LANES
MLA_VW = MLA_HEADS * MLA_V
DIFF_W = DIFF_HEADS * 2 * DIFF_DIM


def _store_with_ones(ref, v):
    lane = lax.broadcasted_iota(jnp.int32, (v.shape[0], LANES), 1)
    ones = (lane == 0).astype(F32).astype(BF)
    for c in range(v.shape[1] // LANES):
        ref[:, 2 * c * LANES:(2 * c + 1) * LANES] = v[:, c * LANES:(c + 1) * LANES].astype(BF)
        ref[:, (2 * c + 1) * LANES:(2 * c + 2) * LANES] = ones


def _odd_proj_kernel(x_ref, g_ref, sh_ref, sc_ref, w_ref, wq_ref, wkv_ref, qg_ref, kvg_ref,
                     c32_ref, a32_ref, b32_ref, c64_ref, a64_ref, b64_ref,
                     q_ref, k_ref, v_ref, dq_ref, dk_ref, dv_ref, *, half32, half64, mla_scale, diff_scale):
    h = _norm_mod(x_ref[...], g_ref[0], sh_ref[0], sc_ref[0]).astype(BF)
    r32 = (c32_ref[...], a32_ref[...], b32_ref[...], half32)
    r64 = (c64_ref[...], a64_ref[...], b64_ref[...], half64)
    s1 = _dot(h, w_ref[:, :ODD_STAGE1])
    qn = _rms(s1[:, :MLA_Q_RANK], qg_ref[...]).astype(BF)
    kvn = _rms(s1[:, MLA_Q_RANK:MLA_Q_RANK + MLA_KV_RANK], kvg_ref[...]).astype(BF)
    kpe = _rope(s1[:, MLA_Q_RANK + MLA_KV_RANK:], *r32)
    q_ref[...] = (_rope(_dot(qn, wq_ref[...]), *r32) * mla_scale).astype(BF)
    kn = _dot(kvn, wkv_ref[:, :MLA_QK])
    k_ref[...] = (kn + jnp.concatenate([kpe] * MLA_HEADS, axis=1)).astype(BF)
    _store_with_ones(v_ref, _dot(kvn, wkv_ref[:, MLA_QK:]))
    c0 = ODD_STAGE1
    dq_ref[...] = (_rope(_dot(h, w_ref[:, c0:c0 + DIFF_W]), *r64) * diff_scale).astype(BF)
    dk_ref[...] = _rope(_dot(h, w_ref[:, c0 + DIFF_W:c0 + 2 * DIFF_W]), *r64).astype(BF)
    _store_with_ones(dv_ref, _dot(h, w_ref[:, c0 + 2 * DIFF_W:]))


def _odd_proj(dims, x, mods3, gains3, w, wq, wkv, qg, kvg, rope32, rope64, tm):
    row = lambda width: pl.BlockSpec((tm, width), lambda i: (i, 0))
    out_w = (MLA_QK, MLA_QK, 2 * MLA_VW, DIFF_W, DIFF_W, 2 * DIFF_HEADS * DIFF_V)
    rs = _rope_spec(dims, tm)
    return pl.pallas_call(
        functools.partial(_odd_proj_kernel, half32=rope32[3], half64=rope64[3],
                          mla_scale=(MLA_NOPE + MLA_ROPE) ** -0.5 * LOG2E, diff_scale=DIFF_DIM ** -0.5 * LOG2E),
        grid=(dims.t_all // tm,),
        in_specs=[row(dims.d), _gain_spec(dims, 0), _mod_spec(dims, tm, 0), _mod_spec(dims, tm, 1),
                  _full_spec(w.shape), _full_spec(wq.shape), _full_spec(wkv.shape),
                  _full_spec(qg.shape), _full_spec(kvg.shape), rs, rs, rs, rs, rs, rs],
        out_specs=[row(wd) for wd in out_w],
        out_shape=[jax.ShapeDtypeStruct((dims.t_all, wd), BF) for wd in out_w],
        compiler_params=_cparams(("arbitrary",), 48),
    )(x, gains3, mods3, mods3, w, wq, wkv, qg, kvg, *rope32[:3], *rope64[:3])


def _softmax_pv(streams, k_refs, v_refs):
    tasks = []
    for si, (q, cols_k, cols_v) in enumerate(streams):
        for kr, vr in zip(k_refs, v_refs):
            n = kr.shape[0]
            for c0 in range(0, n, KEY_CHUNK):
                tasks.append((si, q, kr, vr, c0, min(c0 + KEY_CHUNK, n), cols_k, cols_v))
    score = lambda t: _dot_nt(t[1], t[2][t[4]:t[5], t[6]])
    state = [None] * len(streams)
    s_next = score(tasks[0])
    for ti, t in enumerate(tasks):
        s = s_next
        if ti + 1 < len(tasks):
            s_next = score(tasks[ti + 1])
        si, _, _, vr, c0, c1, _, cols_v = t
        mc = jnp.max(s, axis=-1, keepdims=True)
        if state[si] is None:
            m_new = mc
        else:
            m, acc = state[si]
            m_new = jnp.maximum(m, mc)
        pv = _dot(jnp.exp2((s - m_new).astype(BF)), vr[c0:c1, cols_v])
        if state[si] is None:
            state[si] = (m_new, pv)
        else:
            state[si] = (m_new, jnp.exp2(m - m_new) * acc + pv)
    return [acc[:, :LANES] / acc[:, LANES:LANES + 1] for _, acc in state]


def _by_query_kind(nq, fn, kl_ref, kc_ref, vl_ref, vc_ref):
    qi = pl.program_id(2)

    @pl.when(qi < nq)
    def _():
        fn((kl_ref, kc_ref), (vl_ref, vc_ref))

    @pl.when(qi == nq)
    def _():
        fn((kc_ref,), (vc_ref,))


def _mla_attn_kernel(q_ref, kl_ref, kc_ref, vl_ref, vc_ref, o_ref, *, nq):
    def run(k_refs, v_refs):
        cols = [slice(sub * LANES, (sub + 1) * LANES) for sub in range(2)]
        outs = _softmax_pv([(q_ref[:, c], c, slice(None)) for c in cols], k_refs, v_refs)
        lo = lax.broadcasted_iota(jnp.int32, outs[0].shape, 1) < MLA_V
        o_ref[...] = jnp.where(lo, outs[0], outs[1]).astype(BF)

    _by_query_kind(nq, run, kl_ref, kc_ref, vl_ref, vc_ref)


def _diff_attn_kernel(lam_ref, sg_ref, q_ref, kl_ref, kc_ref, vl_ref, vc_ref, o_ref, *, nq, lam_init):
    def run(k_refs, v_refs):
        lp = lam_ref[...]
        lam = (jnp.exp(jnp.sum(lp[0:1] * lp[1:2], axis=-1, keepdims=True))
               - jnp.exp(jnp.sum(lp[2:3] * lp[3:4], axis=-1, keepdims=True)) + lam_init)
        lane = lax.broadcasted_iota(jnp.int32, (1, LANES), 1)
        q = q_ref[...]
        full = slice(None)
        o1, o2 = _softmax_pv([(q * (lane < DIFF_DIM).astype(F32).astype(BF), full, full),
                              (q * (lane >= DIFF_DIM).astype(F32).astype(BF), full, full)], k_refs, v_refs)
        o_ref[...] = (_rms(o1 - lam * o2, sg_ref[...]) * (1.0 - lam_init)).astype(BF)

    _by_query_kind(nq, run, kl_ref, kc_ref, vl_ref, vc_ref)


def _full_attn(dims, kind, q, k, v, ctx_queries, extra=(), lam_init=0.0):
    if kind == "mla":
        n_h, qw, ow = MLA_HEADS // 2, 2 * LANES, LANES
    else:
        n_h, qw, ow = DIFF_HEADS, LANES, LANES
    vw = 2 * LANES
    tq = dims.l
    nq = dims.s // tq
    ctx_blk0 = dims.t_lat // dims.l
    q_row = lambda b, qi: jnp.where(qi < nq, b * nq + qi, ctx_blk0 + b)
    q_spec = pl.BlockSpec((tq, qw), lambda b, hh, qi: (q_row(b, qi), hh))
    o_spec = pl.BlockSpec((tq, ow), lambda b, hh, qi: (q_row(b, qi), hh))
    lat = lambda w: pl.BlockSpec((dims.s, w), lambda b, hh, qi: (b, hh))
    ctx = lambda w: pl.BlockSpec((dims.l, w), lambda b, hh, qi: (ctx_blk0 + b, hh))
    if kind == "mla":
        body = functools.partial(_mla_attn_kernel, nq=nq)
    else:
        body = functools.partial(_diff_attn_kernel, nq=nq, lam_init=lam_init)
    return pl.pallas_call(
        body,
        grid=(dims.bn, n_h, nq + 1 if ctx_queries else nq),
        in_specs=[_full_spec(e.shape) for e in extra] + [q_spec, lat(qw), ctx(qw), lat(vw), ctx(vw)],
        out_specs=o_spec,
        out_shape=jax.ShapeDtypeStruct((dims.t_all if ctx_queries else dims.t_lat, n_h * ow), BF),
        compiler_params=_cparams(("arbitrary", "arbitrary", "arbitrary"), 56),
    )(*extra, q, k, k, v, v)


def _router_kernel(x_ref, g_ref, sh_ref, sc_ref, wr_ref, f_ref, r_ref, *, n_exp):
    f = _norm_mod(x_ref[...], g_ref[0], sh_ref[0], sc_ref[0])
    f_ref[...] = f
    logits = jnp.dot(f, wr_ref[...], preferred_element_type=F32, precision=lax.Precision.HIGHEST)
    lane = lax.broadcasted_iota(jnp.int32, logits.shape, 1).astype(F32)
    logits = jnp.where(lane < n_exp, logits, NEG)
    m1 = jnp.max(logits, axis=-1, keepdims=True)
    i1 = jnp.min(jnp.where(logits == m1, lane, float(LANES)), axis=-1, keepdims=True)
    rest = jnp.where(lane == i1, NEG, logits)
    m2 = jnp.max(rest, axis=-1, keepdims=True)
    i2 = jnp.min(jnp.where(rest == m2, lane, float(LANES)), axis=-1, keepdims=True)
    e2 = jnp.exp(m2 - m1)
    w1 = 1.0 / (1.0 + e2)
    w2 = e2 / (1.0 + e2)
    r_ref[...] = jnp.where(lane == 0, i1, jnp.where(lane == 1, i2, jnp.where(lane == 2, w1,
                           jnp.where(lane == 3, w2, 0.0))))


def _router(dims, x, rows, mods3, gains3, w_router_pad, n_exp, tm):
    row = lambda width: pl.BlockSpec((tm, width), lambda i: (i, 0))
    return pl.pallas_call(
        functools.partial(_router_kernel, n_exp=n_exp),
        grid=(rows // tm,),
        in_specs=[row(dims.d), _gain_spec(dims, 2), _mod_spec(dims, tm, 3), _mod_spec(dims, tm, 4),
                  _full_spec(w_router_pad.shape)],
        out_specs=[row(dims.d), row(LANES)],
        out_shape=[jax.ShapeDtypeStruct((rows, dims.d), F32),
                   jax.ShapeDtypeStruct((rows, LANES), F32)],
        compiler_params=_cparams(("arbitrary",), 40),
    )(x, gains3, mods3, mods3, w_router_pad)


def _row_copy(src_hbm, dst, s, d, sem):
    return pltpu.make_async_copy(src_hbm.at[pl.ds(s, 1)], dst.at[pl.ds(d, 1)], sem)


def _expert_kernel(be_ref, nu_ref, src_cur, src_nxt, x_hbm, wg_ref, wu_ref, wd_ref, o_ref,
                   xbuf, sem, h_s, acc_s, *, tm, nj):
    blk, j = pl.program_id(0), pl.program_id(1)
    nblk = pl.num_programs(0)
    n_used = nu_ref[0]
    used = blk < n_used
    slot = blk % 2
    per_step = tm // nj
    head = tm - per_step * nj

    def issue(src_ref, dst_slot, r):
        _row_copy(x_hbm, xbuf.at[dst_slot], src_ref[0, 0, r], r, sem.at[dst_slot]).start()

    def wait_slot(s):
        pltpu.make_async_copy(x_hbm.at[pl.ds(0, tm)], xbuf.at[s], sem.at[s]).wait()

    @pl.when(j == 0)
    def _():
        @pl.when(blk == 0)
        def _():
            def first(r, c):
                issue(src_cur, 0, r)
                return c

            lax.fori_loop(0, tm, first, 0, unroll=8)

        @pl.when(blk <= n_used)
        def _():
            wait_slot(slot)

        @pl.when(used)
        def _():
            h_s[...] = xbuf[slot].astype(BF)
            for r in range(head):
                issue(src_nxt, 1 - slot, r)

        acc_s[...] = jnp.zeros_like(acc_s)

    @pl.when(used)
    def _():
        for r in range(per_step):
            issue(src_nxt, 1 - slot, head + j * per_step + r)
        h = h_s[...]
        act = (_silu(_dot(h, wg_ref[0])) * _dot(h, wu_ref[0])).astype(BF)
        acc_s[...] += _dot(act, wd_ref[0])

    @pl.when(j == nj - 1)
    def _():
        o_ref[...] = acc_s[...]

        @pl.when(jnp.logical_and(blk == nblk - 1, used))
        def _():
            wait_slot(1 - slot)


def _experts(f_in, src, block_e, n_used, w_gu, w_down, tm, tn):
    d = f_in.shape[1]
    ff = w_down.shape[1]
    nj = ff // tn
    nblk = src.shape[0] // tm
    jj = lambda blk, j, nu: jnp.where(blk < nu[0], j, nj - 1)
    src3 = src.reshape(nblk, 1, tm)
    return pl.pallas_call(
        functools.partial(_expert_kernel, tm=tm, nj=nj),
        grid_spec=pltpu.PrefetchScalarGridSpec(
            num_scalar_prefetch=2,
            grid=(nblk, nj),
            in_specs=[pl.BlockSpec((1, 1, tm), lambda blk, j, be, nu: (blk, 0, 0), memory_space=pltpu.SMEM),
                      pl.BlockSpec((1, 1, tm), lambda blk, j, be, nu: (jnp.minimum(blk + 1, nblk - 1), 0, 0),
                                   memory_space=pltpu.SMEM),
                      pl.BlockSpec(memory_space=pl.ANY),
                      pl.BlockSpec((1, d, tn), lambda blk, j, be, nu: (be[blk], 0, jj(blk, j, nu))),
                      pl.BlockSpec((1, d, tn), lambda blk, j, be, nu: (be[blk], 0, nj + jj(blk, j, nu))),
                      pl.BlockSpec((1, tn, d), lambda blk, j, be, nu: (be[blk], jj(blk, j, nu), 0))],
            out_specs=pl.BlockSpec((tm, d), lambda blk, j, be, nu: (blk, 0)),
            scratch_shapes=[pltpu.VMEM((2, tm, d), F32), pltpu.SemaphoreType.DMA((2,)),
                            pltpu.VMEM((tm, d), BF), pltpu.VMEM((tm, d), F32)]),
        out_shape=jax.ShapeDtypeStruct((nblk * tm, d), F32),
        compiler_params=_cparams(("arbitrary", "arbitrary"), 56),
    )(block_e, n_used, src3, src3, f_in, w_gu, w_gu, w_down)


def _combine_kernel(dest_ref, r_ref, x_ref, gate_ref, g_ref, y_hbm, o_ref, buf, sem, *, tm):
    def issue(t, c):
        for k in range(TOP_K):
            _row_copy(y_hbm, buf.at[k], dest_ref[0, 0, TOP_K * t + k], t, sem.at[k]).start()
        return c

    lax.fori_loop(0, tm, issue, 0, unroll=4)
    for k in range(TOP_K):
        pltpu.make_async_copy(y_hbm.at[pl.ds(0, tm)], buf.at[k], sem.at[k]).wait()
    r = r_ref[...]
    y = buf[0] * r[:, 2:3] + buf[1] * r[:, 3:4]
    o_ref[...] = x_ref[...] + gate_ref[0] * _rms(y, g_ref[0])


def _combine(dims, y, dest, route, x, mods3, gains3, tm):
    rows = route.shape[0]
    nt = rows // tm
    row = lambda width: pl.BlockSpec((tm, width), lambda i: (i, 0))
    return pl.pallas_call(
        functools.partial(_combine_kernel, tm=tm),
        grid=(nt,),
        in_specs=[pl.BlockSpec((1, 1, TOP_K * tm), lambda i: (i, 0, 0), memory_space=pltpu.SMEM),
                  row(LANES), row(dims.d), _mod_spec(dims, tm, 5), _gain_spec(dims, 3),
                  pl.BlockSpec(memory_space=pl.ANY)],
        out_specs=row(dims.d),
        out_shape=jax.ShapeDtypeStruct((rows, dims.d), F32),
        scratch_shapes=[pltpu.VMEM((TOP_K, tm, dims.d), F32), pltpu.SemaphoreType.DMA((TOP_K,))],
        input_output_aliases={2: 0} if rows == x.shape[0] else {},
        compiler_params=_cparams(("arbitrary",)),
    )(dest.reshape(nt, 1, TOP_K * tm), route, x, mods3, gains3, y)


def _routing_tables(route, n_exp, tm_e):
    t = route.shape[0]
    a = t * TOP_K
    flat_e = route[:, :TOP_K].astype(jnp.int32).reshape(a)
    onehot = (flat_e[:, None] == jnp.arange(n_exp)[None, :]).astype(jnp.int32)
    csum = jnp.cumsum(onehot, axis=0)
    rank = jnp.take_along_axis(csum, flat_e[:, None], axis=1)[:, 0] - 1
    counts = csum[-1]
    padded = (counts + tm_e - 1) // tm_e * tm_e
    ends = jnp.cumsum(padded)
    dest = (ends - padded)[flat_e] + rank
    n_blocks = -(-(a + n_exp * (tm_e - 1)) // tm_e)
    src = jnp.zeros((n_blocks * tm_e,), jnp.int32).at[dest].set(jnp.arange(a, dtype=jnp.int32) // TOP_K)
    block_start = jnp.arange(n_blocks, dtype=jnp.int32) * tm_e
    block_e = jnp.minimum(jnp.sum((ends[None, :] <= block_start[:, None]).astype(jnp.int32), axis=1), n_exp - 1)
    n_used = (ends[-1] // tm_e).reshape(1)
    return dest.astype(jnp.int32), src, block_e.astype(jnp.int32), n_used.astype(jnp.int32)


def _even_weights(w_in):
    kw = SWA_KV_HEADS * SWA_HEAD_DIM
    c0 = EVEN_BCU + EVEN_Q
    wk, wv = w_in[:, c0:c0 + kw], w_in[:, c0 + kw:c0 + 2 * kw]
    dup = lambda w: jnp.concatenate(
        [w[:, g * SWA_HEAD_DIM:(g + 1) * SWA_HEAD_DIM] for g in range(SWA_KV_HEADS) for _ in range(2)], axis=1)
    return jnp.concatenate([w_in[:, :c0], dup(wk), dup(wv)], axis=1).astype(BF)


def _odd_weights(w_in, w_q_up, w_kv_up):
    d = w_in.shape[0]
    c = 0
    mq = w_in[:, c:c + MLA_Q_RANK]; c += MLA_Q_RANK
    dq = w_in[:, c:c + DIFF_W]; c += DIFF_W
    kvd = w_in[:, c:c + MLA_KV_RANK]; c += MLA_KV_RANK
    kpe = w_in[:, c:c + MLA_ROPE]; c += MLA_ROPE
    dk = w_in[:, c:c + DIFF_W]; c += DIFF_W
    dv = w_in[:, c:]
    pad_tail = LANES - MLA_NOPE - MLA_ROPE
    kpe_chunk = jnp.concatenate([jnp.zeros((d, MLA_NOPE), F32), kpe, jnp.zeros((d, pad_tail), F32)], axis=1)
    w = jnp.concatenate([mq, kvd, kpe_chunk, dq, dk, dv], axis=1).astype(BF)
    qh = w_q_up.reshape(MLA_Q_RANK, MLA_HEADS, MLA_NOPE + MLA_ROPE)
    wq = jnp.pad(qh, ((0, 0), (0, 0), (0, pad_tail))).reshape(MLA_Q_RANK, MLA_QK).astype(BF)
    kvh = w_kv_up.reshape(MLA_KV_RANK, MLA_HEADS, MLA_NOPE + MLA_V)
    wk = jnp.pad(kvh[:, :, :MLA_NOPE], ((0, 0), (0, 0), (0, LANES - MLA_NOPE))).reshape(MLA_KV_RANK, MLA_QK)
    wv = kvh[:, :, MLA_NOPE:].reshape(MLA_KV_RANK, MLA_VW)
    return w, wq, jnp.concatenate([wk, wv], axis=1).astype(BF)


def kernel(x, c, ctx, c_ctx, w_mod, b_mod, norm_g, w_in_even, conv_w, sink, w_out_even, w_in_odd, mla_q_norm_g,
           mla_kv_norm_g, w_q_up, w_kv_up, diff_lambda, diff_subln_g, w_out_odd, w_ff_gu, w_ff_down, w_router,
           w_exp_gu, w_exp_down):
    bn, s, d = x.shape
    l = ctx.shape[1]
    depth = w_mod.shape[0]
    n_exp = w_router.shape[-1]
    dims = _Dims(bn, s, l, d)
    assert bn < MOD_ROWS and s % l == 0 and l % BLOCK == 0 and s % GRID_W == 0

    tm = _tile(math.gcd(s, bn * l), 512)
    tm_ffn = _tile(math.gcd(s, bn * l), 1024)
    tm_e = 1024 if dims.t_all >= 8192 else 256

    cc = jnp.zeros((MOD_ROWS, d), F32).at[:bn].set(c).at[bn].set(c_ctx)
    mods = _modulation(cc, w_mod, b_mod)
    rope64 = _rope_tables(s, tm, SWA_HEAD_DIM, SWA_HEAD_DIM, 0)
    rope32 = _rope_tables(s, tm, MLA_ROPE, LANES, MLA_NOPE)

    xa = jnp.concatenate([x.reshape(bn * s, d), ctx.reshape(bn * l, d)], axis=0)
    for layer in range(depth):
        i = layer // 2
        mods3 = mods[layer].reshape(MOD_ROWS, 1, N_MOD * d)
        gains3 = norm_g[layer].reshape(4, 1, d)
        if layer % 2 == 0:
            bcu, q, k, v = _even_proj(dims, xa, mods3, gains3, _even_weights(w_in_even[i]), rope64, tm)
            conv = _gated_conv(dims, bcu, conv_w[i])
            attn = _win_attn(dims, q, k, v, sink[i])
            wo = w_out_even[i].astype(BF)
            xa = _out_proj(dims, conv, attn, wo[:CONV_WIDTH], wo[CONV_WIDTH:], xa, mods3, gains3, tm)
            xa = _dense_ffn(dims, xa, mods3, gains3, w_ff_gu[i].astype(BF), w_ff_down[i].astype(BF),
                            tm_ffn, _tile(w_ff_down.shape[1], 256))
        else:
            w, wq, wkv = _odd_weights(w_in_odd[i], w_q_up[i], w_kv_up[i])
            qm, km, vm, dq, dk, dv = _odd_proj(dims, xa, mods3, gains3, w, wq, wkv,
                                               mla_q_norm_g[i].reshape(1, -1), mla_kv_norm_g[i].reshape(1, -1),
                                               rope32, rope64, tm)
            lam_init = 0.8 - 0.6 * math.exp(-0.3 * layer)
            extra = (diff_lambda[i], diff_subln_g[i].reshape(1, -1))
            need_ctx = layer < depth - 1
            rows = dims.t_all if need_ctx else dims.t_lat
            o_m = _full_attn(dims, "mla", qm, km, vm, need_ctx)
            o_d = _full_attn(dims, "diff", dq, dk, dv, need_ctx, extra, lam_init)
            wo = w_out_odd[i].astype(BF)
            xa = _out_proj(dims, o_m, o_d, wo[:MLA_VW], wo[MLA_VW:], xa, mods3, gains3, tm)
            wr = jnp.pad(w_router[i], ((0, 0), (0, LANES - n_exp)))
            f_in, route = _router(dims, xa, rows, mods3, gains3, wr, n_exp, tm)
            dest, src, block_e, n_used = _routing_tables(route, n_exp, tm_e)
            y = _experts(f_in, src, block_e, n_used, w_exp_gu[i].astype(BF), w_exp_down[i].astype(BF), tm_e,
                         _tile(w_exp_down.shape[2], 512))
            xa = _combine(dims, y, dest, route, xa, mods3, gains3, tm)
    return xa[:bn * s].reshape(bn, s, d)
```

```python
import functools
import math

import jax
import jax.numpy as jnp
from jax import lax
from jax.experimental import pallas as pl
from jax.experimental.pallas import tpu as pltpu

F32 = jnp.float32
BF = jnp.bfloat16

EPS = 1e-6
ROPE_THETA = 10000.0
GRID_W = 64
BLOCK = 128
N_MOD = 6
LANES = 128
MOD_ROWS = 16

CONV_WIDTH = 512
SWA_HEADS = 8
SWA_KV_HEADS = 2
SWA_HEAD_DIM = 64
MLA_HEADS = 8
MLA_Q_RANK = 256
MLA_KV_RANK = 128
MLA_NOPE = 64
MLA_ROPE = 32
MLA_V = 64
DIFF_HEADS = 4
DIFF_DIM = 64
DIFF_V = 2 * DIFF_DIM
TOP_K = 2

NEG = -1e30
LOG2E = math.log2(math.e)
KEY_CHUNK = 1024


def _tile(n, pref):
    if n <= pref:
        return n
    t = pref - pref % LANES
    while t >= LANES:
        if n % t == 0:
            return t
        t -= LANES
    raise ValueError((n, pref))


def _cparams(sem, vmem_mb=None):
    kw = dict(dimension_semantics=sem)
    if vmem_mb is not None:
        kw["vmem_limit_bytes"] = vmem_mb << 20
    return pltpu.CompilerParams(**kw)


def _dot(a, b):
    return jnp.dot(a, b, preferred_element_type=F32)


def _dot_nt(a, b):
    return lax.dot_general(a, b, (((1,), (1,)), ((), ())), preferred_element_type=F32)


def _rms(x, g):
    return x * lax.rsqrt(jnp.mean(x * x, axis=-1, keepdims=True) + EPS) * g


def _norm_mod(x, g, shift, scale):
    return _rms(x, g) * (1.0 + scale) + shift


def _silu(x):
    return x / (1.0 + jnp.exp(-x))


def _rope(x, cos, sa, sb, half):
    out = []
    for c in range(x.shape[1] // LANES):
        xc = x[:, c * LANES:(c + 1) * LANES]
        out.append(xc * cos + pltpu.roll(xc, LANES - half, 1) * sa + pltpu.roll(xc, half, 1) * sb)
    return out[0] if len(out) == 1 else jnp.concatenate(out, axis=1)


def _mod_kernel(cc_ref, w_ref, b_ref, o_ref):
    a = _silu(cc_ref[...]).astype(BF)
    o_ref[0] = _dot(a, w_ref[0].astype(BF)) + b_ref[0]


def _modulation(cc, w_mod, b_mod):
    depth, d, n = w_mod.shape
    tn = _tile(n, 1536)
    return pl.pallas_call(
        _mod_kernel,
        grid=(depth, n // tn),
        in_specs=[
            pl.BlockSpec((MOD_ROWS, d), lambda l, j: (0, 0)),
            pl.BlockSpec((1, d, tn), lambda l, j: (l, 0, j)),
            pl.BlockSpec((1, 1, tn), lambda l, j: (l, 0, j)),
        ],
        out_specs=pl.BlockSpec((1, MOD_ROWS, tn), lambda l, j: (l, 0, j)),
        out_shape=jax.ShapeDtypeStruct((depth, MOD_ROWS, n), F32),
        compiler_params=_cparams(("arbitrary", "arbitrary"), 40),
    )(cc, w_mod, b_mod.reshape(depth, 1, n))


class _Dims:
    def __init__(self, bn, s, l, d):
        self.bn, self.s, self.l, self.d = bn, s, l, d
        self.t_lat = bn * s
        self.t_all = bn * s + bn * l

    def mod_row(self, i, tm):
        return jnp.minimum(i * tm // self.s, self.bn)


def _mod_spec(dims, tm, col):
    return pl.BlockSpec((1, 1, dims.d), lambda i, *_: (dims.mod_row(i, tm), 0, col))


def _gain_spec(dims, k):
    return pl.BlockSpec((1, 1, dims.d), lambda i, *_: (k, 0, 0))


def _rope_spec(dims, tm):
    n_lat, per_seq = dims.t_lat // tm, dims.s // tm
    return pl.BlockSpec((tm, LANES), lambda i: (jnp.where(i < n_lat, i % per_seq, per_seq), 0))


def _full_spec(shape):
    return pl.BlockSpec(shape, lambda *_: (0,) * len(shape))


def _rope_tables(s, pad_rows, rot_dim, period, lane_off):
    axis_dim = rot_dim // 2
    half = axis_dim // 2
    inv = 1.0 / (ROPE_THETA ** (jnp.arange(0, axis_dim, 2, dtype=F32) / axis_dim))
    pos = jnp.arange(s)
    rows = (pos // GRID_W).astype(F32)[:, None]
    cols = (pos % GRID_W).astype(F32)[:, None]
    lane = jnp.arange(LANES)
    dd = lane % period - lane_off
    active = (dd >= 0) & (dd < rot_dim)
    dd = jnp.clip(dd, 0, rot_dim - 1)
    j = dd % axis_dim
    ang = jnp.where((dd // axis_dim == 0)[None, :], rows, cols) * inv[j % half][None, :]
    first = (j < half)[None, :]
    act = active[None, :]
    cos = jnp.where(act, jnp.cos(ang), 1.0)
    sin = jnp.where(act, jnp.sin(ang), 0.0)
    sa = jnp.where(first, -sin, 0.0)
    sb = jnp.where(first, 0.0, sin)
    pad = lambda t, v: jnp.concatenate([t, jnp.full((pad_rows, LANES), v, F32)], axis=0)
    return pad(cos, 1.0), pad(sa, 0.0), pad(sb, 0.0), half


EVEN_BCU = 3 * CONV_WIDTH
EVEN_Q = SWA_HEADS * SWA_HEAD_DIM
EVEN_KD = SWA_KV_HEADS * LANES


def _even_proj_kernel(x_ref, g_ref, sh_ref, sc_ref, w_ref, cos_ref, sa_ref, sb_ref,
                      bcu_ref, q_ref, k_ref, v_ref, *, half, q_scale):
    h = _norm_mod(x_ref[...], g_ref[0], sh_ref[0], sc_ref[0]).astype(BF)
    c0, c1, c2 = EVEN_BCU, EVEN_BCU + EVEN_Q, EVEN_BCU + EVEN_Q + EVEN_KD
    bcu_ref[...] = _dot(h, w_ref[:, :c0]).astype(BF)
    qk = _rope(_dot(h, w_ref[:, c0:c2]), cos_ref[...], sa_ref[...], sb_ref[...], half)
    q_ref[...] = (qk[:, :EVEN_Q] * q_scale).astype(BF)
    k_ref[...] = qk[:, EVEN_Q:].astype(BF)
    v_ref[...] = _dot(h, w_ref[:, c2:]).astype(BF)


def _even_proj(dims, x, mods3, gains3, w, rope, tm):
    n = w.shape[1]
    cos, sa, sb, half = rope
    row = lambda width: pl.BlockSpec((tm, width), lambda i: (i, 0))
    out_w = (EVEN_BCU, EVEN_Q, EVEN_KD, EVEN_KD)
    return pl.pallas_call(
        functools.partial(_even_proj_kernel, half=half, q_scale=SWA_HEAD_DIM ** -0.5 * LOG2E),
        grid=(dims.t_all // tm,),
        in_specs=[row(dims.d), _gain_spec(dims, 0), _mod_spec(dims, tm, 0), _mod_spec(dims, tm, 1),
                  _full_spec((dims.d, n)), _rope_spec(dims, tm), _rope_spec(dims, tm), _rope_spec(dims, tm)],
        out_specs=[row(wd) for wd in out_w],
        out_shape=[jax.ShapeDtypeStruct((dims.t_all, wd), BF) for wd in out_w],
        compiler_params=_cparams(("arbitrary",), 48),
    )(x, gains3, mods3, mods3, w, cos, sa, sb)


def _conv_kernel(b_ref, c_ref, u_ref, cp_ref, up_ref, cn_ref, un_ref, w_ref, o_ref, *, n_lat_blocks, per_seq):
    i = pl.program_id(0)
    rows = b_ref.shape[0]
    hr = cp_ref.shape[0]
    pos = i % per_seq
    is_lat = i < n_lat_blocks
    has_prev = jnp.logical_and(is_lat, pos > 0).astype(F32)
    has_next = jnp.logical_and(is_lat, pos < per_seq - 1).astype(F32)
    cu = c_ref[...].astype(F32) * u_ref[...].astype(F32)
    cu_p = (cp_ref[...].astype(F32) * up_ref[...].astype(F32))[hr - 1:hr] * has_prev
    cu_n = (cn_ref[...].astype(F32) * un_ref[...].astype(F32))[0:1] * has_next
    r = lax.broadcasted_iota(jnp.int32, cu.shape, 0)
    prev = jnp.where(r == 0, cu_p, pltpu.roll(cu, 1, 0))
    nxt = jnp.where(r == rows - 1, cu_n, pltpu.roll(cu, rows - 1, 0))
    w = w_ref[...]
    o_ref[...] = (b_ref[...].astype(F32) * (prev * w[0:1] + cu * w[1:2] + nxt * w[2:3])).astype(BF)


def _gated_conv(dims, bcu, conv_w):
    rows = dims.l
    hr = 16
    nblk = dims.t_all // rows
    per = rows // hr
    last = dims.t_all // hr - 1
    cw = CONV_WIDTH
    main = lambda col: pl.BlockSpec((rows, cw), lambda i: (i, col))
    prev = lambda col: pl.BlockSpec((hr, cw), lambda i: (jnp.maximum(i * per - 1, 0), col))
    nxt = lambda col: pl.BlockSpec((hr, cw), lambda i: (jnp.minimum((i + 1) * per, last), col))
    return pl.pallas_call(
        functools.partial(_conv_kernel, n_lat_blocks=dims.t_lat // rows, per_seq=dims.s // rows),
        grid=(nblk,),
        in_specs=[main(0), main(1), main(2), prev(1), prev(2), nxt(1), nxt(2), _full_spec(conv_w.shape)],
        out_specs=pl.BlockSpec((rows, cw), lambda i: (i, 0)),
        out_shape=jax.ShapeDtypeStruct((dims.t_all, cw), BF),
        compiler_params=_cparams(("arbitrary",)),
    )(bcu, bcu, bcu, bcu, bcu, bcu, bcu, conv_w)


def _win_attn_kernel(sink_ref, q_ref, kp_ref, ko_ref, kn_ref, kc_ref, vp_ref, vo_ref, vn_ref, vc_ref,
                     o_ref, *, nb):
    n = pl.program_id(1)
    blk = q_ref.shape[0]
    n_ctx = kc_ref.shape[0]
    is_lat = n < nb
    lo_s = jnp.where(is_lat, jnp.where(n >= 1, 0, blk), 3 * blk)
    hi_s = jnp.where(is_lat, jnp.where(n + 1 < nb, 3 * blk, 2 * blk), 0)
    shape = (blk, 3 * blk + n_ctx)
    c = lax.broadcasted_iota(jnp.int32, shape, 1)
    r = lax.broadcasted_iota(jnp.int32, shape, 0)
    ok = (c >= 3 * blk) | ((c >= r) & (c - 2 * blk <= r) & (c >= lo_s) & (c < hi_s))
    bias = jnp.where(ok, 0.0, NEG)
    kall = jnp.concatenate([kp_ref[...], ko_ref[...], kn_ref[...], kc_ref[...]], axis=0)
    vall = jnp.concatenate([vp_ref[...], vo_ref[...], vn_ref[...], vc_ref[...]], axis=0)
    lane = lax.broadcasted_iota(jnp.int32, (1, LANES), 1)
    keep = ((lane < LANES // 2).astype(F32).astype(BF), (lane >= LANES // 2).astype(F32).astype(BF))
    lo = lax.broadcasted_iota(jnp.int32, (blk, LANES), 1) < LANES // 2
    per_group = SWA_HEADS // SWA_KV_HEADS

    def score(head):
        pair, sub, g = head // 2, head % 2, head // per_group
        q2 = q_ref[:, pair * LANES:(pair + 1) * LANES]
        return _dot_nt(q2 * keep[sub], kall[:, g * LANES:(g + 1) * LANES]) + bias

    s_next = score(0)
    outs = []
    for head in range(SWA_HEADS):
        s = s_next
        if head + 1 < SWA_HEADS:
            s_next = score(head + 1)
        g = head // per_group
        sk = jnp.full((blk, 1), sink_ref[head], F32) * LOG2E
        m = jnp.maximum(jnp.max(s, axis=-1, keepdims=True), sk)
        p = jnp.exp2(s - m)
        den = jnp.sum(p, axis=-1, keepdims=True) + jnp.exp2(sk - m)
        outs.append(_dot(p.astype(BF), vall[:, g * LANES:(g + 1) * LANES]) / den)
        if head % 2 == 1:
            pair = head // 2
            o_ref[:, pair * LANES:(pair + 1) * LANES] = jnp.where(lo, outs[-2], outs[-1]).astype(BF)


def _win_attn(dims, q, k, v, sink):
    nb = dims.s // BLOCK
    nc = dims.l // BLOCK
    lat_blocks = dims.t_lat // BLOCK
    ctx_blk0 = dims.t_lat // dims.l

    def own(b, n):
        return jnp.where(n < nb, b * nb + n, lat_blocks + b * nc + (n - nb))

    def prev(b, n):
        return jnp.where(n < nb, b * nb + jnp.maximum(n - 1, 0), own(b, n))

    def nxt(b, n):
        return jnp.where(n < nb, b * nb + jnp.minimum(n + 1, nb - 1), own(b, n))

    kv = lambda f: pl.BlockSpec((BLOCK, EVEN_KD), lambda b, n: (f(b, n), 0))
    ctx = pl.BlockSpec((dims.l, EVEN_KD), lambda b, n: (ctx_blk0 + b, 0))
    qo = pl.BlockSpec((BLOCK, EVEN_Q), lambda b, n: (own(b, n), 0))
    return pl.pallas_call(
        functools.partial(_win_attn_kernel, nb=nb),
        grid=(dims.bn, nb + nc),
        in_specs=[pl.BlockSpec(memory_space=pltpu.SMEM), qo, kv(prev), kv(own), kv(nxt), ctx,
                  kv(prev), kv(own), kv(nxt), ctx],
        out_specs=qo,
        out_shape=jax.ShapeDtypeStruct((dims.t_all, EVEN_Q), BF),
        compiler_params=_cparams(("arbitrary", "arbitrary")),
    )(sink, q, k, k, k, k, v, v, v, v)


def _out_proj_kernel(a1_ref, a2_ref, w1_ref, w2_ref, x_ref, gate_ref, g_ref, o_ref):
    y = _dot(a1_ref[...], w1_ref[...]) + _dot(a2_ref[...], w2_ref[...])
    o_ref[...] = x_ref[...] + gate_ref[0] * _rms(y, g_ref[0])


def _out_proj(dims, a1, a2, w1, w2, x, mods3, gains3, tm):
    row = lambda width: pl.BlockSpec((tm, width), lambda i: (i, 0))
    return pl.pallas_call(
        _out_proj_kernel,
        grid=(a1.shape[0] // tm,),
        in_specs=[row(a1.shape[1]), row(a2.shape[1]), _full_spec(w1.shape), _full_spec(w2.shape),
                  row(dims.d), _mod_spec(dims, tm, 2), _gain_spec(dims, 1)],
        out_specs=row(dims.d),
        out_shape=jax.ShapeDtypeStruct(x.shape, F32),
        input_output_aliases={4: 0},
        compiler_params=_cparams(("arbitrary",), 40),
    )(a1, a2, w1, w2, x, mods3, gains3)


def _ffn_kernel(x_ref, g_ref, sh_ref, sc_ref, wg_ref, wu_ref, wd_ref, gate_ref, g2_ref, o_ref, h_s, acc_s):
    j = pl.program_id(1)

    @pl.when(j == 0)
    def _():
        h_s[...] = _norm_mod(x_ref[...], g_ref[0], sh_ref[0], sc_ref[0]).astype(BF)
        acc_s[...] = jnp.zeros_like(acc_s)

    h = h_s[...]
    act = (_silu(_dot(h, wg_ref[...])) * _dot(h, wu_ref[...])).astype(BF)
    acc_s[...] += _dot(act, wd_ref[...])

    @pl.when(j == pl.num_programs(1) - 1)
    def _():
        o_ref[...] = x_ref[...] + gate_ref[0] * _rms(acc_s[...], g2_ref[0])


def _dense_ffn(dims, x, mods3, gains3, w_gu, w_down, tm, tn):
    ff = w_down.shape[0]
    nj = ff // tn
    row = pl.BlockSpec((tm, dims.d), lambda i, j: (i, 0))
    return pl.pallas_call(
        _ffn_kernel,
        grid=(dims.t_all // tm, nj),
        in_specs=[row, _gain_spec(dims, 2), _mod_spec(dims, tm, 3), _mod_spec(dims, tm, 4),
                  pl.BlockSpec((dims.d, tn), lambda i, j: (0, j)),
                  pl.BlockSpec((dims.d, tn), lambda i, j: (0, nj + j)),
                  pl.BlockSpec((tn, dims.d), lambda i, j: (j, 0)),
                  _mod_spec(dims, tm, 5), _gain_spec(dims, 3)],
        out_specs=row,
        out_shape=jax.ShapeDtypeStruct(x.shape, F32),
        scratch_shapes=[pltpu.VMEM((tm, dims.d), BF), pltpu.VMEM((tm, dims.d), F32)],
        input_output_aliases={0: 0},
        compiler_params=_cparams(("arbitrary", "arbitrary"), 48),
    )(x, gains3, mods3, mods3, w_gu, w_gu, w_down, mods3, gains3)


ODD_STAGE1 = MLA_Q_RANK + MLA_KV_RANK + LANES
MLA_QK = MLA_HEADS * LANES
MLA_VW = MLA_HEADS * MLA_V
DIFF_W = DIFF_HEADS * 2 * DIFF_DIM


def _odd_proj_kernel(x_ref, g_ref, sh_ref, sc_ref, w_ref, wq_ref, wkv_ref, qg_ref, kvg_ref,
                     c32_ref, a32_ref, b32_ref, c64_ref, a64_ref, b64_ref,
                     q_ref, k_ref, v_ref, dq_ref, dk_ref, dv_ref, *, half32, half64, mla_scale, diff_scale):
    h = _norm_mod(x_ref[...], g_ref[0], sh_ref[0], sc_ref[0]).astype(BF)
    r32 = (c32_ref[...], a32_ref[...], b32_ref[...], half32)
    r64 = (c64_ref[...], a64_ref[...], b64_ref[...], half64)
    s1 = _dot(h, w_ref[:, :ODD_STAGE1])
    qn = _rms(s1[:, :MLA_Q_RANK], qg_ref[...]).astype(BF)
    kvn = _rms(s1[:, MLA_Q_RANK:MLA_Q_RANK + MLA_KV_RANK], kvg_ref[...]).astype(BF)
    kpe = _rope(s1[:, MLA_Q_RANK + MLA_KV_RANK:], *r32)
    q_ref[...] = (_rope(_dot(qn, wq_ref[...]), *r32) * mla_scale).astype(BF)
    kn = _dot(kvn, wkv_ref[:, :MLA_QK])
    k_ref[...] = (kn + jnp.concatenate([kpe] * MLA_HEADS, axis=1)).astype(BF)
    v_ref[...] = _dot(kvn, wkv_ref[:, MLA_QK:]).astype(BF)
    c0 = ODD_STAGE1
    dq_ref[...] = (_rope(_dot(h, w_ref[:, c0:c0 + DIFF_W]), *r64) * diff_scale).astype(BF)
    dk_ref[...] = _rope(_dot(h, w_ref[:, c0 + DIFF_W:c0 + 2 * DIFF_W]), *r64).astype(BF)
    dv_ref[...] = _dot(h, w_ref[:, c0 + 2 * DIFF_W:]).astype(BF)


def _odd_proj(dims, x, mods3, gains3, w, wq, wkv, qg, kvg, rope32, rope64, tm):
    row = lambda width: pl.BlockSpec((tm, width), lambda i: (i, 0))
    out_w = (MLA_QK, MLA_QK, MLA_VW, DIFF_W, DIFF_W, DIFF_HEADS * DIFF_V)
    rs = _rope_spec(dims, tm)
    return pl.pallas_call(
        functools.partial(_odd_proj_kernel, half32=rope32[3], half64=rope64[3],
                          mla_scale=(MLA_NOPE + MLA_ROPE) ** -0.5 * LOG2E, diff_scale=DIFF_DIM ** -0.5 * LOG2E),
        grid=(dims.t_all // tm,),
        in_specs=[row(dims.d), _gain_spec(dims, 0), _mod_spec(dims, tm, 0), _mod_spec(dims, tm, 1),
                  _full_spec(w.shape), _full_spec(wq.shape), _full_spec(wkv.shape),
                  _full_spec(qg.shape), _full_spec(kvg.shape), rs, rs, rs, rs, rs, rs],
        out_specs=[row(wd) for wd in out_w],
        out_shape=[jax.ShapeDtypeStruct((dims.t_all, wd), BF) for wd in out_w],
        compiler_params=_cparams(("arbitrary",), 48),
    )(x, gains3, mods3, mods3, w, wq, wkv, qg, kvg, *rope32[:3], *rope64[:3])


def _softmax_pv(streams, k_refs, v_refs):
    tasks = []
    for si, (q, cols_k, cols_v) in enumerate(streams):
        for kr, vr in zip(k_refs, v_refs):
            n = kr.shape[0]
            for c0 in range(0, n, KEY_CHUNK):
                tasks.append((si, q, kr, vr, c0, min(c0 + KEY_CHUNK, n), cols_k, cols_v))
    score = lambda t: _dot_nt(t[1], t[2][t[4]:t[5], t[6]])
    state = [None] * len(streams)
    s_next = score(tasks[0])
    for ti, t in enumerate(tasks):
        s = s_next
        if ti + 1 < len(tasks):
            s_next = score(tasks[ti + 1])
        si, _, _, vr, c0, c1, _, cols_v = t
        mc = jnp.max(s, axis=-1, keepdims=True)
        if state[si] is None:
            m_new = mc
        else:
            m, l, acc = state[si]
            m_new = jnp.maximum(m, mc)
        p = jnp.exp2(s - m_new)
        ps = jnp.sum(p, axis=-1, keepdims=True)
        pv = _dot(p.astype(BF), vr[c0:c1, cols_v])
        if state[si] is None:
            state[si] = (m_new, ps, pv)
        else:
            a = jnp.exp2(m - m_new)
            state[si] = (m_new, a * l + ps, a * acc + pv)
    return [acc / l for _, l, acc in state]


def _mla_attn_kernel(q_ref, *refs, n_seg):
    k_refs, v_refs, o_ref = refs[:n_seg], refs[n_seg:2 * n_seg], refs[2 * n_seg]
    cols = [slice(sub * LANES, (sub + 1) * LANES) for sub in range(2)]
    outs = _softmax_pv([(q_ref[:, c], c, slice(None)) for c in cols], k_refs, v_refs)
    lo = lax.broadcasted_iota(jnp.int32, outs[0].shape, 1) < MLA_V
    o_ref[...] = jnp.where(lo, outs[0], outs[1]).astype(BF)


def _diff_attn_kernel(lam_ref, sg_ref, q_ref, *refs, n_seg, lam_init):
    k_refs, v_refs, o_ref = refs[:n_seg], refs[n_seg:2 * n_seg], refs[2 * n_seg]
    lp = lam_ref[...]
    lam = (jnp.exp(jnp.sum(lp[0:1] * lp[1:2], axis=-1, keepdims=True))
           - jnp.exp(jnp.sum(lp[2:3] * lp[3:4], axis=-1, keepdims=True)) + lam_init)
    lane = lax.broadcasted_iota(jnp.int32, (1, LANES), 1)
    q = q_ref[...]
    full = slice(None)
    o1, o2 = _softmax_pv([(q * (lane < DIFF_DIM).astype(F32).astype(BF), full, full),
                          (q * (lane >= DIFF_DIM).astype(F32).astype(BF), full, full)], k_refs, v_refs)
    o_ref[...] = (_rms(o1 - lam * o2, sg_ref[...]) * (1.0 - lam_init)).astype(BF)


def _full_attn_call(dims, kind, q, k, v, ctx_queries, tq, extra, lam_init):
    if kind == "mla":
        n_h, qw, vw = MLA_HEADS // 2, 2 * LANES, LANES
    else:
        n_h, qw, vw = DIFF_HEADS, LANES, LANES
    ctx_blk0 = dims.t_lat // dims.l
    ctx_seg = (dims.l, lambda b: ctx_blk0 + b)
    if ctx_queries:
        tq, nq, out_rows = dims.l, 1, dims.bn * dims.l
        q_row = lambda b, qi: ctx_blk0 + b
        o_row = lambda b, qi: b
        seg = [ctx_seg]
    else:
        nq, out_rows = dims.s // tq, dims.t_lat
        q_row = o_row = lambda b, qi: b * nq + qi
        seg = [(dims.s, lambda b: b), ctx_seg]
    kv_specs = lambda w: [pl.BlockSpec((rows, w), (lambda f: lambda b, hh, qi: (f(b), hh))(f)) for rows, f in seg]
    if kind == "mla":
        body = functools.partial(_mla_attn_kernel, n_seg=len(seg))
    else:
        body = functools.partial(_diff_attn_kernel, n_seg=len(seg), lam_init=lam_init)
    return pl.pallas_call(
        body,
        grid=(dims.bn, n_h, nq),
        in_specs=([_full_spec(e.shape) for e in extra]
                  + [pl.BlockSpec((tq, qw), lambda b, hh, qi: (q_row(b, qi), hh))] + kv_specs(qw) + kv_specs(vw)),
        out_specs=pl.BlockSpec((tq, vw), lambda b, hh, qi: (o_row(b, qi), hh)),
        out_shape=jax.ShapeDtypeStruct((out_rows, n_h * vw), BF),
        compiler_params=_cparams(("arbitrary", "arbitrary", "arbitrary"), 56),
    )(*extra, q, *([k] * len(seg)), *([v] * len(seg)))


def _full_attn(dims, kind, q, k, v, need_ctx, tq, extra=(), lam_init=0.0):
    lat = _full_attn_call(dims, kind, q, k, v, False, tq, extra, lam_init)
    if not need_ctx:
        return lat
    return jnp.concatenate([lat, _full_attn_call(dims, kind, q, k, v, True, tq, extra, lam_init)], axis=0)


def _router_kernel(x_ref, g_ref, sh_ref, sc_ref, wr_ref, f_ref, r_ref, *, n_exp):
    f = _norm_mod(x_ref[...], g_ref[0], sh_ref[0], sc_ref[0])
    f_ref[...] = f
    logits = jnp.dot(f, wr_ref[...], preferred_element_type=F32, precision=lax.Precision.HIGHEST)
    lane = lax.broadcasted_iota(jnp.int32, logits.shape, 1).astype(F32)
    logits = jnp.where(lane < n_exp, logits, NEG)
    m1 = jnp.max(logits, axis=-1, keepdims=True)
    i1 = jnp.min(jnp.where(logits == m1, lane, float(LANES)), axis=-1, keepdims=True)
    rest = jnp.where(lane == i1, NEG, logits)
    m2 = jnp.max(rest, axis=-1, keepdims=True)
    i2 = jnp.min(jnp.where(rest == m2, lane, float(LANES)), axis=-1, keepdims=True)
    e2 = jnp.exp(m2 - m1)
    w1 = 1.0 / (1.0 + e2)
    w2 = e2 / (1.0 + e2)
    r_ref[...] = jnp.where(lane == 0, i1, jnp.where(lane == 1, i2, jnp.where(lane == 2, w1,
                           jnp.where(lane == 3, w2, 0.0))))


def _router(dims, x, rows, mods3, gains3, w_router_pad, n_exp, tm):
    row = lambda width: pl.BlockSpec((tm, width), lambda i: (i, 0))
    return pl.pallas_call(
        functools.partial(_router_kernel, n_exp=n_exp),
        grid=(rows // tm,),
        in_specs=[row(dims.d), _gain_spec(dims, 2), _mod_spec(dims, tm, 3), _mod_spec(dims, tm, 4),
                  _full_spec(w_router_pad.shape)],
        out_specs=[row(dims.d), row(LANES)],
        out_shape=[jax.ShapeDtypeStruct((rows, dims.d), F32),
                   jax.ShapeDtypeStruct((rows, LANES), F32)],
        compiler_params=_cparams(("arbitrary",), 40),
    )(x, gains3, mods3, mods3, w_router_pad)


def _row_copy(src_hbm, dst, s, d, sem):
    return pltpu.make_async_copy(src_hbm.at[pl.ds(s, 1)], dst.at[pl.ds(d, 1)], sem)


def _expert_kernel(be_ref, nu_ref, src_cur, src_nxt, x_hbm, wg_ref, wu_ref, wd_ref, o_ref,
                   xbuf, sem, h_s, acc_s, *, tm, nj):
    blk, j = pl.program_id(0), pl.program_id(1)
    nblk = pl.num_programs(0)
    n_used = nu_ref[0]
    used = blk < n_used
    slot = blk % 2
    per_step = tm // nj
    head = tm - per_step * nj

    def issue(src_ref, dst_slot, r):
        _row_copy(x_hbm, xbuf.at[dst_slot], src_ref[0, 0, r], r, sem.at[dst_slot]).start()

    def wait_slot(s):
        pltpu.make_async_copy(x_hbm.at[pl.ds(0, tm)], xbuf.at[s], sem.at[s]).wait()

    @pl.when(j == 0)
    def _():
        @pl.when(blk == 0)
        def _():
            def first(r, c):
                issue(src_cur, 0, r)
                return c

            lax.fori_loop(0, tm, first, 0, unroll=8)

        @pl.when(blk <= n_used)
        def _():
            wait_slot(slot)

        @pl.when(used)
        def _():
            h_s[...] = xbuf[slot].astype(BF)
            for r in range(head):
                issue(src_nxt, 1 - slot, r)

        acc_s[...] = jnp.zeros_like(acc_s)

    @pl.when(used)
    def _():
        for r in range(per_step):
            issue(src_nxt, 1 - slot, head + j * per_step + r)
        h = h_s[...]
        act = (_silu(_dot(h, wg_ref[0])) * _dot(h, wu_ref[0])).astype(BF)
        acc_s[...] += _dot(act, wd_ref[0])

    @pl.when(j == nj - 1)
    def _():
        o_ref[...] = acc_s[...]

        @pl.when(jnp.logical_and(blk == nblk - 1, used))
        def _():
            wait_slot(1 - slot)


def _experts(f_in, src, block_e, n_used, w_gu, w_down, tm, tn):
    d = f_in.shape[1]
    ff = w_down.shape[1]
    nj = ff // tn
    nblk = src.shape[0] // tm
    jj = lambda blk, j, nu: jnp.where(blk < nu[0], j, nj - 1)
    src3 = src.reshape(nblk, 1, tm)
    return pl.pallas_call(
        functools.partial(_expert_kernel, tm=tm, nj=nj),
        grid_spec=pltpu.PrefetchScalarGridSpec(
            num_scalar_prefetch=2,
            grid=(nblk, nj),
            in_specs=[pl.BlockSpec((1, 1, tm), lambda blk, j, be, nu: (blk, 0, 0), memory_space=pltpu.SMEM),
                      pl.BlockSpec((1, 1, tm), lambda blk, j, be, nu: (jnp.minimum(blk + 1, nblk - 1), 0, 0),
                                   memory_space=pltpu.SMEM),
                      pl.BlockSpec(memory_space=pl.ANY),
                      pl.BlockSpec((1, d, tn), lambda blk, j, be, nu: (be[blk], 0, jj(blk, j, nu))),
                      pl.BlockSpec((1, d, tn), lambda blk, j, be, nu: (be[blk], 0, nj + jj(blk, j, nu))),
                      pl.BlockSpec((1, tn, d), lambda blk, j, be, nu: (be[blk], jj(blk, j, nu), 0))],
            out_specs=pl.BlockSpec((tm, d), lambda blk, j, be, nu: (blk, 0)),
            scratch_shapes=[pltpu.VMEM((2, tm, d), F32), pltpu.SemaphoreType.DMA((2,)),
                            pltpu.VMEM((tm, d), BF), pltpu.VMEM((tm, d), F32)]),
        out_shape=jax.ShapeDtypeStruct((nblk * tm, d), F32),
        compiler_params=_cparams(("arbitrary", "arbitrary"), 56),
    )(block_e, n_used, src3, src3, f_in, w_gu, w_gu, w_down)


def _combine_kernel(dest_ref, r_ref, x_ref, gate_ref, g_ref, y_hbm, o_ref, buf, sem, *, tm):
    def issue(t, c):
        for k in range(TOP_K):
            _row_copy(y_hbm, buf.at[k], dest_ref[0, 0, TOP_K * t + k], t, sem.at[k]).start()
        return c

    lax.fori_loop(0, tm, issue, 0, unroll=4)
    for k in range(TOP_K):
        pltpu.make_async_copy(y_hbm.at[pl.ds(0, tm)], buf.at[k], sem.at[k]).wait()
    r = r_ref[...]
    y = buf[0] * r[:, 2:3] + buf[1] * r[:, 3:4]
    o_ref[...] = x_ref[...] + gate_ref[0] * _rms(y, g_ref[0])


def _combine(dims, y, dest, route, x, mods3, gains3, tm):
    rows = route.shape[0]
    nt = rows // tm
    row = lambda width: pl.BlockSpec((tm, width), lambda i: (i, 0))
    return pl.pallas_call(
        functools.partial(_combine_kernel, tm=tm),
        grid=(nt,),
        in_specs=[pl.BlockSpec((1, 1, TOP_K * tm), lambda i: (i, 0, 0), memory_space=pltpu.SMEM),
                  row(LANES), row(dims.d), _mod_spec(dims, tm, 5), _gain_spec(dims, 3),
                  pl.BlockSpec(memory_space=pl.ANY)],
        out_specs=row(dims.d),
        out_shape=jax.ShapeDtypeStruct((rows, dims.d), F32),
        scratch_shapes=[pltpu.VMEM((TOP_K, tm, dims.d), F32), pltpu.SemaphoreType.DMA((TOP_K,))],
        input_output_aliases={2: 0} if rows == x.shape[0] else {},
        compiler_params=_cparams(("arbitrary",)),
    )(dest.reshape(nt, 1, TOP_K * tm), route, x, mods3, gains3, y)


def _routing_tables(route, n_exp, tm_e):
    t = route.shape[0]
    a = t * TOP_K
    flat_e = route[:, :TOP_K].astype(jnp.int32).reshape(a)
    onehot = (flat_e[:, None] == jnp.arange(n_exp)[None, :]).astype(jnp.int32)
    csum = jnp.cumsum(onehot, axis=0)
    rank = jnp.take_along_axis(csum, flat_e[:, None], axis=1)[:, 0] - 1
    counts = csum[-1]
    padded = (counts + tm_e - 1) // tm_e * tm_e
    ends = jnp.cumsum(padded)
    dest = (ends - padded)[flat_e] + rank
    n_blocks = -(-(a + n_exp * (tm_e - 1)) // tm_e)
    src = jnp.zeros((n_blocks * tm_e,), jnp.int32).at[dest].set(jnp.arange(a, dtype=jnp.int32) // TOP_K)
    block_start = jnp.arange(n_blocks, dtype=jnp.int32) * tm_e
    block_e = jnp.minimum(jnp.sum((ends[None, :] <= block_start[:, None]).astype(jnp.int32), axis=1), n_exp - 1)
    n_used = (ends[-1] // tm_e).reshape(1)
    return dest.astype(jnp.int32), src, block_e.astype(jnp.int32), n_used.astype(jnp.int32)


def _even_weights(w_in):
    kw = SWA_KV_HEADS * SWA_HEAD_DIM
    c0 = EVEN_BCU + EVEN_Q
    wk, wv = w_in[:, c0:c0 + kw], w_in[:, c0 + kw:c0 + 2 * kw]
    dup = lambda w: jnp.concatenate(
        [w[:, g * SWA_HEAD_DIM:(g + 1) * SWA_HEAD_DIM] for g in range(SWA_KV_HEADS) for _ in range(2)], axis=1)
    return jnp.concatenate([w_in[:, :c0], dup(wk), dup(wv)], axis=1).astype(BF)


def _odd_weights(w_in, w_q_up, w_kv_up):
    d = w_in.shape[0]
    c = 0
    mq = w_in[:, c:c + MLA_Q_RANK]; c += MLA_Q_RANK
    dq = w_in[:, c:c + DIFF_W]; c += DIFF_W
    kvd = w_in[:, c:c + MLA_KV_RANK]; c += MLA_KV_RANK
    kpe = w_in[:, c:c + MLA_ROPE]; c += MLA_ROPE
    dk = w_in[:, c:c + DIFF_W]; c += DIFF_W
    dv = w_in[:, c:]
    pad_tail = LANES - MLA_NOPE - MLA_ROPE
    kpe_chunk = jnp.concatenate([jnp.zeros((d, MLA_NOPE), F32), kpe, jnp.zeros((d, pad_tail), F32)], axis=1)
    w = jnp.concatenate([mq, kvd, kpe_chunk, dq, dk, dv], axis=1).astype(BF)
    qh = w_q_up.reshape(MLA_Q_RANK, MLA_HEADS, MLA_NOPE + MLA_ROPE)
    wq = jnp.pad(qh, ((0, 0), (0, 0), (0, pad_tail))).reshape(MLA_Q_RANK, MLA_QK).astype(BF)
    kvh = w_kv_up.reshape(MLA_KV_RANK, MLA_HEADS, MLA_NOPE + MLA_V)
    wk = jnp.pad(kvh[:, :, :MLA_NOPE], ((0, 0), (0, 0), (0, LANES - MLA_NOPE))).reshape(MLA_KV_RANK, MLA_QK)
    wv = kvh[:, :, MLA_NOPE:].reshape(MLA_KV_RANK, MLA_VW)
    return w, wq, jnp.concatenate([wk, wv], axis=1).astype(BF)


def kernel(x, c, ctx, c_ctx, w_mod, b_mod, norm_g, w_in_even, conv_w, sink, w_out_even, w_in_odd, mla_q_norm_g,
           mla_kv_norm_g, w_q_up, w_kv_up, diff_lambda, diff_subln_g, w_out_odd, w_ff_gu, w_ff_down, w_router,
           w_exp_gu, w_exp_down):
    bn, s, d = x.shape
    l = ctx.shape[1]
    depth = w_mod.shape[0]
    n_exp = w_router.shape[-1]
    dims = _Dims(bn, s, l, d)
    assert bn < MOD_ROWS and s % l == 0 and l % BLOCK == 0 and s % GRID_W == 0

    tm = _tile(math.gcd(s, bn * l), 512)
    tm_ffn = _tile(math.gcd(s, bn * l), 1024)
    tm_e = 1024 if dims.t_all >= 8192 else 256
    tq = _tile(s, 512)

    cc = jnp.zeros((MOD_ROWS, d), F32).at[:bn].set(c).at[bn].set(c_ctx)
    mods = _modulation(cc, w_mod, b_mod)
    rope64 = _rope_tables(s, tm, SWA_HEAD_DIM, SWA_HEAD_DIM, 0)
    rope32 = _rope_tables(s, tm, MLA_ROPE, LANES, MLA_NOPE)

    xa = jnp.concatenate([x.reshape(bn * s, d), ctx.reshape(bn * l, d)], axis=0)
    for layer in range(depth):
        i = layer // 2
        mods3 = mods[layer].reshape(MOD_ROWS, 1, N_MOD * d)
        gains3 = norm_g[layer].reshape(4, 1, d)
        if layer % 2 == 0:
            bcu, q, k, v = _even_proj(dims, xa, mods3, gains3, _even_weights(w_in_even[i]), rope64, tm)
            conv = _gated_conv(dims, bcu, conv_w[i])
            attn = _win_attn(dims, q, k, v, sink[i])
            wo = w_out_even[i].astype(BF)
            xa = _out_proj(dims, conv, attn, wo[:CONV_WIDTH], wo[CONV_WIDTH:], xa, mods3, gains3, tm)
            xa = _dense_ffn(dims, xa, mods3, gains3, w_ff_gu[i].astype(BF), w_ff_down[i].astype(BF),
                            tm_ffn, _tile(w_ff_down.shape[1], 256))
        else:
            w, wq, wkv = _odd_weights(w_in_odd[i], w_q_up[i], w_kv_up[i])
            qm, km, vm, dq, dk, dv = _odd_proj(dims, xa, mods3, gains3, w, wq, wkv,
                                               mla_q_norm_g[i].reshape(1, -1), mla_kv_norm_g[i].reshape(1, -1),
                                               rope32, rope64, tm)
            lam_init = 0.8 - 0.6 * math.exp(-0.3 * layer)
            extra = (diff_lambda[i], diff_subln_g[i].reshape(1, -1))
            need_ctx = layer < depth - 1
            rows = dims.t_all if need_ctx else dims.t_lat
            o_m = _full_attn(dims, "mla", qm, km, vm, need_ctx, tq)
            o_d = _full_attn(dims, "diff", dq, dk, dv, need_ctx, tq, extra, lam_init)
            wo = w_out_odd[i].astype(BF)
            xa = _out_proj(dims, o_m, o_d, wo[:MLA_VW], wo[MLA_VW:], xa, mods3, gains3, tm)
            wr = jnp.pad(w_router[i], ((0, 0), (0, LANES - n_exp)))
            f_in, route = _router(dims, xa, rows, mods3, gains3, wr, n_exp, tm)
            dest, src, block_e, n_used = _routing_tables(route, n_exp, tm_e)
            y = _experts(f_in, src, block_e, n_used, w_exp_gu[i].astype(BF), w_exp_down[i].astype(BF), tm_e,
                         _tile(w_exp_down.shape[2], 512))
            xa = _combine(dims, y, dest, route, xa, mods3, gains3, tm)
    return xa[:bn * s].reshape(bn, s, d)
```

```python
import functools
import math

import jax
import jax.numpy as jnp
from jax import lax
from jax.experimental import pallas as pl
from jax.experimental.pallas import tpu as pltpu

F32 = jnp.float32
BF = jnp.bfloat16

EPS = 1e-6
ROPE_THETA = 10000.0
GRID_W = 64
BLOCK = 128
N_MOD = 6
LANES = 128
MXU_N = 256
MOD_ROWS = 16

CONV_WIDTH = 512
SWA_HEADS = 8
SWA_KV_HEADS = 2
SWA_HEAD_DIM = 64
MLA_HEADS = 8
MLA_Q_RANK = 256
MLA_KV_RANK = 128
MLA_NOPE = 64
MLA_ROPE = 32
MLA_V = 64
DIFF_HEADS = 4
DIFF_DIM = 64
DIFF_V = 2 * DIFF_DIM
TOP_K = 2

NEG = -1e30
LOG2E = math.log2(math.e)
KEY_CHUNK = 1024


def _tile(n, pref):
    if n <= pref:
        return n
    t = pref - pref % LANES
    while t >= LANES:
        if n % t == 0:
            return t
        t -= LANES
    raise ValueError((n, pref))


def _cparams(sem, vmem_mb=None):
    kw = dict(dimension_semantics=sem)
    if vmem_mb is not None:
        kw["vmem_limit_bytes"] = vmem_mb << 20
    return pltpu.CompilerParams(**kw)


def _dot(a, b):
    return jnp.dot(a, b, preferred_element_type=F32)


def _dot_nt(a, b):
    return lax.dot_general(a, b, (((1,), (1,)), ((), ())), preferred_element_type=F32)


def _rms(x, g):
    return x * lax.rsqrt(jnp.mean(x * x, axis=-1, keepdims=True) + EPS) * g


def _norm_mod(x, g, shift, scale):
    return _rms(x, g) * (1.0 + scale) + shift


def _silu(x):
    return x / (1.0 + jnp.exp(-x))


def _rope(x, cos, sa, sb, half):
    out = []
    for c in range(x.shape[1] // LANES):
        xc = x[:, c * LANES:(c + 1) * LANES]
        out.append(xc * cos + pltpu.roll(xc, LANES - half, 1) * sa + pltpu.roll(xc, half, 1) * sb)
    return out[0] if len(out) == 1 else jnp.concatenate(out, axis=1)


def _mod_kernel(cc_ref, w_ref, b_ref, o_ref):
    a = _silu(cc_ref[...]).astype(BF)
    o_ref[0] = _dot(a, w_ref[0].astype(BF)) + b_ref[0]


def _modulation(cc, w_mod, b_mod):
    depth, d, n = w_mod.shape
    tn = _tile(n, 1536)
    return pl.pallas_call(
        _mod_kernel,
        grid=(depth, n // tn),
        in_specs=[
            pl.BlockSpec((MOD_ROWS, d), lambda l, j: (0, 0)),
            pl.BlockSpec((1, d, tn), lambda l, j: (l, 0, j)),
            pl.BlockSpec((1, 1, tn), lambda l, j: (l, 0, j)),
        ],
        out_specs=pl.BlockSpec((1, MOD_ROWS, tn), lambda l, j: (l, 0, j)),
        out_shape=jax.ShapeDtypeStruct((depth, MOD_ROWS, n), F32),
        compiler_params=_cparams(("arbitrary", "arbitrary"), 40),
    )(cc, w_mod, b_mod.reshape(depth, 1, n))


class _Dims:
    def __init__(self, bn, s, l, d):
        self.bn, self.s, self.l, self.d = bn, s, l, d
        self.t_lat = bn * s
        self.t_all = bn * s + bn * l

    def mod_row(self, i, tm):
        return jnp.minimum(i * tm // self.s, self.bn)


def _mod_spec(dims, tm, col):
    return pl.BlockSpec((1, 1, dims.d), lambda i, *_: (dims.mod_row(i, tm), 0, col))


def _gain_spec(dims, k):
    return pl.BlockSpec((1, 1, dims.d), lambda i, *_: (k, 0, 0))


def _rope_spec(dims, tm):
    n_lat, per_seq = dims.t_lat // tm, dims.s // tm
    return pl.BlockSpec((tm, LANES), lambda i: (jnp.where(i < n_lat, i % per_seq, per_seq), 0))


def _full_spec(shape):
    return pl.BlockSpec(shape, lambda *_: (0,) * len(shape))


def _rope_tables(s, pad_rows, rot_dim, period, lane_off):
    axis_dim = rot_dim // 2
    half = axis_dim // 2
    inv = 1.0 / (ROPE_THETA ** (jnp.arange(0, axis_dim, 2, dtype=F32) / axis_dim))
    pos = jnp.arange(s)
    rows = (pos // GRID_W).astype(F32)[:, None]
    cols = (pos % GRID_W).astype(F32)[:, None]
    lane = jnp.arange(LANES)
    dd = lane % period - lane_off
    active = (dd >= 0) & (dd < rot_dim)
    dd = jnp.clip(dd, 0, rot_dim - 1)
    j = dd % axis_dim
    ang = jnp.where((dd // axis_dim == 0)[None, :], rows, cols) * inv[j % half][None, :]
    first = (j < half)[None, :]
    act = active[None, :]
    cos = jnp.where(act, jnp.cos(ang), 1.0)
    sin = jnp.where(act, jnp.sin(ang), 0.0)
    sa = jnp.where(first, -sin, 0.0)
    sb = jnp.where(first, 0.0, sin)
    pad = lambda t, v: jnp.concatenate([t, jnp.full((pad_rows, LANES), v, F32)], axis=0)
    return pad(cos, 1.0), pad(sa, 0.0), pad(sb, 0.0), half


EVEN_BCU = 3 * CONV_WIDTH
EVEN_Q = SWA_HEADS * SWA_HEAD_DIM
EVEN_KD = SWA_KV_HEADS * LANES


def _even_proj_kernel(x_ref, g_ref, sh_ref, sc_ref, w_ref, cos_ref, sa_ref, sb_ref,
                      bcu_ref, q_ref, k_ref, v_ref, *, half, q_scale):
    h = _norm_mod(x_ref[...], g_ref[0], sh_ref[0], sc_ref[0]).astype(BF)
    c0, c1, c2 = EVEN_BCU, EVEN_BCU + EVEN_Q, EVEN_BCU + EVEN_Q + EVEN_KD
    bcu_ref[...] = _dot(h, w_ref[:, :c0]).astype(BF)
    qk = _rope(_dot(h, w_ref[:, c0:c2]), cos_ref[...], sa_ref[...], sb_ref[...], half)
    q_ref[...] = (qk[:, :EVEN_Q] * q_scale).astype(BF)
    k_ref[...] = qk[:, EVEN_Q:].astype(BF)
    v_ref[...] = _dot(h, w_ref[:, c2:]).astype(BF)


def _even_proj(dims, x, mods3, gains3, w, rope, tm):
    n = w.shape[1]
    cos, sa, sb, half = rope
    row = lambda width: pl.BlockSpec((tm, width), lambda i: (i, 0))
    out_w = (EVEN_BCU, EVEN_Q, EVEN_KD, EVEN_KD)
    return pl.pallas_call(
        functools.partial(_even_proj_kernel, half=half, q_scale=SWA_HEAD_DIM ** -0.5 * LOG2E),
        grid=(dims.t_all // tm,),
        in_specs=[row(dims.d), _gain_spec(dims, 0), _mod_spec(dims, tm, 0), _mod_spec(dims, tm, 1),
                  _full_spec((dims.d, n)), _rope_spec(dims, tm), _rope_spec(dims, tm), _rope_spec(dims, tm)],
        out_specs=[row(wd) for wd in out_w],
        out_shape=[jax.ShapeDtypeStruct((dims.t_all, wd), BF) for wd in out_w],
        compiler_params=_cparams(("arbitrary",), 48),
    )(x, gains3, mods3, mods3, w, cos, sa, sb)


def _conv_kernel(b_ref, c_ref, u_ref, cp_ref, up_ref, cn_ref, un_ref, w_ref, o_ref, *, n_lat_blocks, per_seq):
    i = pl.program_id(0)
    rows = b_ref.shape[0]
    hr = cp_ref.shape[0]
    pos = i % per_seq
    is_lat = i < n_lat_blocks
    has_prev = jnp.logical_and(is_lat, pos > 0).astype(F32)
    has_next = jnp.logical_and(is_lat, pos < per_seq - 1).astype(F32)
    cu = c_ref[...].astype(F32) * u_ref[...].astype(F32)
    cu_p = (cp_ref[...].astype(F32) * up_ref[...].astype(F32))[hr - 1:hr] * has_prev
    cu_n = (cn_ref[...].astype(F32) * un_ref[...].astype(F32))[0:1] * has_next
    r = lax.broadcasted_iota(jnp.int32, cu.shape, 0)
    prev = jnp.where(r == 0, cu_p, pltpu.roll(cu, 1, 0))
    nxt = jnp.where(r == rows - 1, cu_n, pltpu.roll(cu, rows - 1, 0))
    w = w_ref[...]
    o_ref[...] = (b_ref[...].astype(F32) * (prev * w[0:1] + cu * w[1:2] + nxt * w[2:3])).astype(BF)


def _gated_conv(dims, bcu, conv_w):
    rows = dims.l
    hr = 16
    nblk = dims.t_all // rows
    per = rows // hr
    last = dims.t_all // hr - 1
    cw = CONV_WIDTH
    main = lambda col: pl.BlockSpec((rows, cw), lambda i: (i, col))
    prev = lambda col: pl.BlockSpec((hr, cw), lambda i: (jnp.maximum(i * per - 1, 0), col))
    nxt = lambda col: pl.BlockSpec((hr, cw), lambda i: (jnp.minimum((i + 1) * per, last), col))
    return pl.pallas_call(
        functools.partial(_conv_kernel, n_lat_blocks=dims.t_lat // rows, per_seq=dims.s // rows),
        grid=(nblk,),
        in_specs=[main(0), main(1), main(2), prev(1), prev(2), nxt(1), nxt(2), _full_spec(conv_w.shape)],
        out_specs=pl.BlockSpec((rows, cw), lambda i: (i, 0)),
        out_shape=jax.ShapeDtypeStruct((dims.t_all, cw), BF),
        compiler_params=_cparams(("arbitrary",)),
    )(bcu, bcu, bcu, bcu, bcu, bcu, bcu, conv_w)


def _win_attn_kernel(sink_ref, q_ref, kp_ref, ko_ref, kn_ref, kc_ref, vp_ref, vo_ref, vn_ref, vc_ref,
                     o_ref, *, nb):
    n = pl.program_id(1)
    blk = q_ref.shape[0]
    n_ctx = kc_ref.shape[0]
    is_lat = n < nb
    lo_s = jnp.where(is_lat, jnp.where(n >= 1, 0, blk), 3 * blk)
    hi_s = jnp.where(is_lat, jnp.where(n + 1 < nb, 3 * blk, 2 * blk), 0)
    shape = (blk, 3 * blk + n_ctx)
    c = lax.broadcasted_iota(jnp.int32, shape, 1)
    r = lax.broadcasted_iota(jnp.int32, shape, 0)
    ok = (c >= 3 * blk) | ((c >= r) & (c - 2 * blk <= r) & (c >= lo_s) & (c < hi_s))
    bias = jnp.where(ok, 0.0, NEG)
    kall = jnp.concatenate([kp_ref[...], ko_ref[...], kn_ref[...], kc_ref[...]], axis=0)
    vall = jnp.concatenate([vp_ref[...], vo_ref[...], vn_ref[...], vc_ref[...]], axis=0)
    lane = lax.broadcasted_iota(jnp.int32, (1, LANES), 1)
    keep = ((lane < LANES // 2).astype(F32).astype(BF), (lane >= LANES // 2).astype(F32).astype(BF))
    lo = lax.broadcasted_iota(jnp.int32, (blk, LANES), 1) < LANES // 2
    per_group = SWA_HEADS // SWA_KV_HEADS

    def score(head):
        pair, sub, g = head // 2, head % 2, head // per_group
        q2 = q_ref[:, pair * LANES:(pair + 1) * LANES]
        return _dot_nt(q2 * keep[sub], kall[:, g * LANES:(g + 1) * LANES]) + bias

    s_next = score(0)
    outs = []
    for head in range(SWA_HEADS):
        s = s_next
        if head + 1 < SWA_HEADS:
            s_next = score(head + 1)
        g = head // per_group
        sk = jnp.full((blk, 1), sink_ref[head], F32) * LOG2E
        m = jnp.maximum(jnp.max(s, axis=-1, keepdims=True), sk)
        p = jnp.exp2(s - m)
        den = jnp.sum(p, axis=-1, keepdims=True) + jnp.exp2(sk - m)
        outs.append(_dot(p.astype(BF), vall[:, g * LANES:(g + 1) * LANES]) / den)
        if head % 2 == 1:
            pair = head // 2
            o_ref[:, pair * LANES:(pair + 1) * LANES] = jnp.where(lo, outs[-2], outs[-1]).astype(BF)


def _win_attn(dims, q, k, v, sink):
    nb = dims.s // BLOCK
    nc = dims.l // BLOCK
    lat_blocks = dims.t_lat // BLOCK
    ctx_blk0 = dims.t_lat // dims.l

    def own(b, n):
        return jnp.where(n < nb, b * nb + n, lat_blocks + b * nc + (n - nb))

    def prev(b, n):
        return jnp.where(n < nb, b * nb + jnp.maximum(n - 1, 0), own(b, n))

    def nxt(b, n):
        return jnp.where(n < nb, b * nb + jnp.minimum(n + 1, nb - 1), own(b, n))

    kv = lambda f: pl.BlockSpec((BLOCK, EVEN_KD), lambda b, n: (f(b, n), 0))
    ctx = pl.BlockSpec((dims.l, EVEN_KD), lambda b, n: (ctx_blk0 + b, 0))
    qo = pl.BlockSpec((BLOCK, EVEN_Q), lambda b, n: (own(b, n), 0))
    return pl.pallas_call(
        functools.partial(_win_attn_kernel, nb=nb),
        grid=(dims.bn, nb + nc),
        in_specs=[pl.BlockSpec(memory_space=pltpu.SMEM), qo, kv(prev), kv(own), kv(nxt), ctx,
                  kv(prev), kv(own), kv(nxt), ctx],
        out_specs=qo,
        out_shape=jax.ShapeDtypeStruct((dims.t_all, EVEN_Q), BF),
        compiler_params=_cparams(("arbitrary", "arbitrary")),
    )(sink, q, k, k, k, k, v, v, v, v)


def _out_proj_kernel(a1_ref, a2_ref, w1_ref, w2_ref, x_ref, gate_ref, g_ref, o_ref):
    y = _dot(a1_ref[...], w1_ref[...]) + _dot(a2_ref[...], w2_ref[...])
    o_ref[...] = x_ref[...] + gate_ref[0] * _rms(y, g_ref[0])


def _out_proj(dims, a1, a2, w1, w2, x, mods3, gains3, tm, gate_col=2, gain_row=1):
    row = lambda width: pl.BlockSpec((tm, width), lambda i: (i, 0))
    return pl.pallas_call(
        _out_proj_kernel,
        grid=(a1.shape[0] // tm,),
        in_specs=[row(a1.shape[1]), row(a2.shape[1]), _full_spec(w1.shape), _full_spec(w2.shape),
                  row(dims.d), _mod_spec(dims, tm, gate_col), _gain_spec(dims, gain_row)],
        out_specs=row(dims.d),
        out_shape=jax.ShapeDtypeStruct(x.shape, F32),
        input_output_aliases={4: 0},
        compiler_params=_cparams(("arbitrary",), 40),
    )(a1, a2, w1, w2, x, mods3, gains3)


def _ffn_up_kernel(x_ref, g_ref, sh_ref, sc_ref, w_ref, lo_ref, hi_ref, *, ff):
    h = _norm_mod(x_ref[...], g_ref[0], sh_ref[0], sc_ref[0]).astype(BF)
    c0 = 0
    for ref in (lo_ref, hi_ref):
        c1 = c0 + ref.shape[1]
        ref[...] = (_silu(_dot(h, w_ref[:, c0:c1])) * _dot(h, w_ref[:, ff + c0:ff + c1])).astype(BF)
        c0 = c1


def _ffn_split(ff):
    lo = (ff // 2 + MXU_N - 1) // MXU_N * MXU_N
    return (lo, ff - lo) if 0 < lo < ff else (ff // 2, ff - ff // 2)


def _ffn_up(dims, x, mods3, gains3, w_gu, tm):
    ff = w_gu.shape[1] // 2
    row = lambda width: pl.BlockSpec((tm, width), lambda i: (i, 0))
    widths = _ffn_split(ff)
    return pl.pallas_call(
        functools.partial(_ffn_up_kernel, ff=ff),
        grid=(dims.t_all // tm,),
        in_specs=[row(dims.d), _gain_spec(dims, 2), _mod_spec(dims, tm, 3), _mod_spec(dims, tm, 4),
                  _full_spec(w_gu.shape)],
        out_specs=[row(wd) for wd in widths],
        out_shape=[jax.ShapeDtypeStruct((dims.t_all, wd), BF) for wd in widths],
        compiler_params=_cparams(("arbitrary",), 56),
    )(x, gains3, mods3, mods3, w_gu)


ODD_STAGE1 = MLA_Q_RANK + MLA_KV_RANK + LANES
MLA_QK = MLA_HEADS * LANES
MLA_VW = MLA_HEADS * MLA_V
DIFF_W = DIFF_HEADS * 2 * DIFF_DIM


def _odd_proj_kernel(x_ref, g_ref, sh_ref, sc_ref, w_ref, wq_ref, wkv_ref, qg_ref, kvg_ref,
                     c32_ref, a32_ref, b32_ref, c64_ref, a64_ref, b64_ref,
                     q_ref, k_ref, v_ref, dq_ref, dk_ref, dv_ref, *, half32, half64, mla_scale, diff_scale):
    h = _norm_mod(x_ref[...], g_ref[0], sh_ref[0], sc_ref[0]).astype(BF)
    r32 = (c32_ref[...], a32_ref[...], b32_ref[...], half32)
    r64 = (c64_ref[...], a64_ref[...], b64_ref[...], half64)
    s1 = _dot(h, w_ref[:, :ODD_STAGE1])
    qn = _rms(s1[:, :MLA_Q_RANK], qg_ref[...]).astype(BF)
    kvn = _rms(s1[:, MLA_Q_RANK:MLA_Q_RANK + MLA_KV_RANK], kvg_ref[...]).astype(BF)
    kpe = _rope(s1[:, MLA_Q_RANK + MLA_KV_RANK:], *r32)
    q_ref[...] = (_rope(_dot(qn, wq_ref[...]), *r32) * mla_scale).astype(BF)
    kn = _dot(kvn, wkv_ref[:, :MLA_QK])
    k_ref[...] = (kn + jnp.concatenate([kpe] * MLA_HEADS, axis=1)).astype(BF)
    v_ref[...] = _dot(kvn, wkv_ref[:, MLA_QK:]).astype(BF)
    c0 = ODD_STAGE1
    dq_ref[...] = (_rope(_dot(h, w_ref[:, c0:c0 + DIFF_W]), *r64) * diff_scale).astype(BF)
    dk_ref[...] = _rope(_dot(h, w_ref[:, c0 + DIFF_W:c0 + 2 * DIFF_W]), *r64).astype(BF)
    dv_ref[...] = _dot(h, w_ref[:, c0 + 2 * DIFF_W:]).astype(BF)


def _odd_proj(dims, x, mods3, gains3, w, wq, wkv, qg, kvg, rope32, rope64, tm):
    row = lambda width: pl.BlockSpec((tm, width), lambda i: (i, 0))
    out_w = (MLA_QK, MLA_QK, MLA_VW, DIFF_W, DIFF_W, DIFF_HEADS * DIFF_V)
    rs = _rope_spec(dims, tm)
    return pl.pallas_call(
        functools.partial(_odd_proj_kernel, half32=rope32[3], half64=rope64[3],
                          mla_scale=(MLA_NOPE + MLA_ROPE) ** -0.5 * LOG2E, diff_scale=DIFF_DIM ** -0.5 * LOG2E),
        grid=(dims.t_all // tm,),
        in_specs=[row(dims.d), _gain_spec(dims, 0), _mod_spec(dims, tm, 0), _mod_spec(dims, tm, 1),
                  _full_spec(w.shape), _full_spec(wq.shape), _full_spec(wkv.shape),
                  _full_spec(qg.shape), _full_spec(kvg.shape), rs, rs, rs, rs, rs, rs],
        out_specs=[row(wd) for wd in out_w],
        out_shape=[jax.ShapeDtypeStruct((dims.t_all, wd), BF) for wd in out_w],
        compiler_params=_cparams(("arbitrary",), 48),
    )(x, gains3, mods3, mods3, w, wq, wkv, qg, kvg, *rope32[:3], *rope64[:3])


def _softmax_pv(streams, k_refs, v_refs):
    tasks = []
    for si, (q, cols_k, cols_v) in enumerate(streams):
        for kr, vr in zip(k_refs, v_refs):
            n = kr.shape[0]
            for c0 in range(0, n, KEY_CHUNK):
                tasks.append((si, q, kr, vr, c0, min(c0 + KEY_CHUNK, n), cols_k, cols_v))
    score = lambda t: _dot_nt(t[1], t[2][t[4]:t[5], t[6]])
    state = [None] * len(streams)
    s_next = score(tasks[0])
    for ti, t in enumerate(tasks):
        s = s_next
        if ti + 1 < len(tasks):
            s_next = score(tasks[ti + 1])
        si, _, _, vr, c0, c1, _, cols_v = t
        mc = jnp.max(s, axis=-1, keepdims=True)
        if state[si] is None:
            m_new = mc
        else:
            m, l, acc = state[si]
            m_new = jnp.maximum(m, mc)
        p = jnp.exp2(s - m_new)
        ps = jnp.sum(p, axis=-1, keepdims=True)
        pv = _dot(p.astype(BF), vr[c0:c1, cols_v])
        if state[si] is None:
            state[si] = (m_new, ps, pv)
        else:
            a = jnp.exp2(m - m_new)
            state[si] = (m_new, a * l + ps, a * acc + pv)
    return [acc / l for _, l, acc in state]


def _mla_attn_kernel(q_ref, *refs, n_seg):
    k_refs, v_refs, o_ref = refs[:n_seg], refs[n_seg:2 * n_seg], refs[2 * n_seg]
    cols = [slice(sub * LANES, (sub + 1) * LANES) for sub in range(2)]
    outs = _softmax_pv([(q_ref[:, c], c, slice(None)) for c in cols], k_refs, v_refs)
    lo = lax.broadcasted_iota(jnp.int32, outs[0].shape, 1) < MLA_V
    o_ref[...] = jnp.where(lo, outs[0], outs[1]).astype(BF)


def _diff_attn_kernel(lam_ref, sg_ref, q_ref, *refs, n_seg, lam_init):
    k_refs, v_refs, o_ref = refs[:n_seg], refs[n_seg:2 * n_seg], refs[2 * n_seg]
    lp = lam_ref[...]
    lam = (jnp.exp(jnp.sum(lp[0:1] * lp[1:2], axis=-1, keepdims=True))
           - jnp.exp(jnp.sum(lp[2:3] * lp[3:4], axis=-1, keepdims=True)) + lam_init)
    lane = lax.broadcasted_iota(jnp.int32, (1, LANES), 1)
    q = q_ref[...]
    full = slice(None)
    o1, o2 = _softmax_pv([(q * (lane < DIFF_DIM).astype(F32).astype(BF), full, full),
                          (q * (lane >= DIFF_DIM).astype(F32).astype(BF), full, full)], k_refs, v_refs)
    o_ref[...] = (_rms(o1 - lam * o2, sg_ref[...]) * (1.0 - lam_init)).astype(BF)


def _full_attn_call(dims, kind, q, k, v, ctx_queries, tq, extra, lam_init):
    if kind == "mla":
        n_h, qw, vw = MLA_HEADS // 2, 2 * LANES, LANES
    else:
        n_h, qw, vw = DIFF_HEADS, LANES, LANES
    ctx_blk0 = dims.t_lat // dims.l
    ctx_seg = (dims.l, lambda b: ctx_blk0 + b)
    if ctx_queries:
        tq, nq, out_rows = dims.l, 1, dims.bn * dims.l
        q_row = lambda b, qi: ctx_blk0 + b
        o_row = lambda b, qi: b
        seg = [ctx_seg]
    else:
        nq, out_rows = dims.s // tq, dims.t_lat
        q_row = o_row = lambda b, qi: b * nq + qi
        seg = [(dims.s, lambda b: b), ctx_seg]
    kv_specs = lambda w: [pl.BlockSpec((rows, w), (lambda f: lambda b, hh, qi: (f(b), hh))(f)) for rows, f in seg]
    if kind == "mla":
        body = functools.partial(_mla_attn_kernel, n_seg=len(seg))
    else:
        body = functools.partial(_diff_attn_kernel, n_seg=len(seg), lam_init=lam_init)
    return pl.pallas_call(
        body,
        grid=(dims.bn, n_h, nq),
        in_specs=([_full_spec(e.shape) for e in extra]
                  + [pl.BlockSpec((tq, qw), lambda b, hh, qi: (q_row(b, qi), hh))] + kv_specs(qw) + kv_specs(vw)),
        out_specs=pl.BlockSpec((tq, vw), lambda b, hh, qi: (o_row(b, qi), hh)),
        out_shape=jax.ShapeDtypeStruct((out_rows, n_h * vw), BF),
        compiler_params=_cparams(("arbitrary", "arbitrary", "arbitrary"), 56),
    )(*extra, q, *([k] * len(seg)), *([v] * len(seg)))


def _full_attn(dims, kind, q, k, v, need_ctx, tq, extra=(), lam_init=0.0):
    lat = _full_attn_call(dims, kind, q, k, v, False, tq, extra, lam_init)
    if not need_ctx:
        return lat
    return jnp.concatenate([lat, _full_attn_call(dims, kind, q, k, v, True, tq, extra, lam_init)], axis=0)


def _router_kernel(x_ref, g_ref, sh_ref, sc_ref, wr_ref, f_ref, r_ref, *, n_exp):
    f = _norm_mod(x_ref[...], g_ref[0], sh_ref[0], sc_ref[0])
    f_ref[...] = f
    logits = jnp.dot(f, wr_ref[...], preferred_element_type=F32, precision=lax.Precision.HIGHEST)
    lane = lax.broadcasted_iota(jnp.int32, logits.shape, 1).astype(F32)
    logits = jnp.where(lane < n_exp, logits, NEG)
    m1 = jnp.max(logits, axis=-1, keepdims=True)
    i1 = jnp.min(jnp.where(logits == m1, lane, float(LANES)), axis=-1, keepdims=True)
    rest = jnp.where(lane == i1, NEG, logits)
    m2 = jnp.max(rest, axis=-1, keepdims=True)
    i2 = jnp.min(jnp.where(rest == m2, lane, float(LANES)), axis=-1, keepdims=True)
    e2 = jnp.exp(m2 - m1)
    w1 = 1.0 / (1.0 + e2)
    w2 = e2 / (1.0 + e2)
    r_ref[...] = jnp.where(lane == 0, i1, jnp.where(lane == 1, i2, jnp.where(lane == 2, w1,
                           jnp.where(lane == 3, w2, 0.0))))


def _router(dims, x, rows, mods3, gains3, w_router_pad, n_exp, tm):
    row = lambda width: pl.BlockSpec((tm, width), lambda i: (i, 0))
    return pl.pallas_call(
        functools.partial(_router_kernel, n_exp=n_exp),
        grid=(rows // tm,),
        in_specs=[row(dims.d), _gain_spec(dims, 2), _mod_spec(dims, tm, 3), _mod_spec(dims, tm, 4),
                  _full_spec(w_router_pad.shape)],
        out_specs=[row(dims.d), row(LANES)],
        out_shape=[jax.ShapeDtypeStruct((rows, dims.d), F32),
                   jax.ShapeDtypeStruct((rows, LANES), F32)],
        compiler_params=_cparams(("arbitrary",), 40),
    )(x, gains3, mods3, mods3, w_router_pad)


def _row_copy(src_hbm, dst, s, d, sem):
    return pltpu.make_async_copy(src_hbm.at[pl.ds(s, 1)], dst.at[pl.ds(d, 1)], sem)


def _expert_kernel(be_ref, nu_ref, src_cur, src_nxt, x_hbm, wg_ref, wu_ref, wd_ref, o_ref,
                   xbuf, sem, h_s, acc_s, *, tm, nj):
    blk, j = pl.program_id(0), pl.program_id(1)
    nblk = pl.num_programs(0)
    n_used = nu_ref[0]
    used = blk < n_used
    slot = blk % 2
    per_step = tm // nj
    head = tm - per_step * nj

    def issue(src_ref, dst_slot, r):
        _row_copy(x_hbm, xbuf.at[dst_slot], src_ref[0, 0, r], r, sem.at[dst_slot]).start()

    def wait_slot(s):
        pltpu.make_async_copy(x_hbm.at[pl.ds(0, tm)], xbuf.at[s], sem.at[s]).wait()

    @pl.when(j == 0)
    def _():
        @pl.when(blk == 0)
        def _():
            def first(r, c):
                issue(src_cur, 0, r)
                return c

            lax.fori_loop(0, tm, first, 0, unroll=8)

        @pl.when(blk <= n_used)
        def _():
            wait_slot(slot)

        @pl.when(used)
        def _():
            h_s[...] = xbuf[slot].astype(BF)
            for r in range(head):
                issue(src_nxt, 1 - slot, r)

        acc_s[...] = jnp.zeros_like(acc_s)

    @pl.when(used)
    def _():
        for r in range(per_step):
            issue(src_nxt, 1 - slot, head + j * per_step + r)
        h = h_s[...]
        act = (_silu(_dot(h, wg_ref[0].astype(BF))) * _dot(h, wu_ref[0].astype(BF))).astype(BF)
        acc_s[...] += _dot(act, wd_ref[0].astype(BF))

    @pl.when(j == nj - 1)
    def _():
        o_ref[...] = acc_s[...]

        @pl.when(jnp.logical_and(blk == nblk - 1, used))
        def _():
            wait_slot(1 - slot)


def _experts(f_in, src, block_e, n_used, w_gu, w_down, tm, tn):
    d = f_in.shape[1]
    ff = w_down.shape[1]
    nj = ff // tn
    nblk = src.shape[0] // tm
    jj = lambda blk, j, nu: jnp.where(blk < nu[0], j, nj - 1)
    src3 = src.reshape(nblk, 1, tm)
    return pl.pallas_call(
        functools.partial(_expert_kernel, tm=tm, nj=nj),
        grid_spec=pltpu.PrefetchScalarGridSpec(
            num_scalar_prefetch=2,
            grid=(nblk, nj),
            in_specs=[pl.BlockSpec((1, 1, tm), lambda blk, j, be, nu: (blk, 0, 0), memory_space=pltpu.SMEM),
                      pl.BlockSpec((1, 1, tm), lambda blk, j, be, nu: (jnp.minimum(blk + 1, nblk - 1), 0, 0),
                                   memory_space=pltpu.SMEM),
                      pl.BlockSpec(memory_space=pl.ANY),
                      pl.BlockSpec((1, d, tn), lambda blk, j, be, nu: (be[blk], 0, jj(blk, j, nu))),
                      pl.BlockSpec((1, d, tn), lambda blk, j, be, nu: (be[blk], 0, nj + jj(blk, j, nu))),
                      pl.BlockSpec((1, tn, d), lambda blk, j, be, nu: (be[blk], jj(blk, j, nu), 0))],
            out_specs=pl.BlockSpec((tm, d), lambda blk, j, be, nu: (blk, 0)),
            scratch_shapes=[pltpu.VMEM((2, tm, d), F32), pltpu.SemaphoreType.DMA((2,)),
                            pltpu.VMEM((tm, d), BF), pltpu.VMEM((tm, d), F32)]),
        out_shape=jax.ShapeDtypeStruct((nblk * tm, d), F32),
        compiler_params=_cparams(("arbitrary", "arbitrary"), 56),
    )(block_e, n_used, src3, src3, f_in, w_gu, w_gu, w_down)


def _combine_kernel(dest_ref, r_ref, x_ref, gate_ref, g_ref, y_hbm, o_ref, buf, sem, *, tm):
    def issue(t, c):
        for k in range(TOP_K):
            _row_copy(y_hbm, buf.at[k], dest_ref[0, 0, TOP_K * t + k], t, sem.at[k]).start()
        return c

    lax.fori_loop(0, tm, issue, 0, unroll=4)
    for k in range(TOP_K):
        pltpu.make_async_copy(y_hbm.at[pl.ds(0, tm)], buf.at[k], sem.at[k]).wait()
    r = r_ref[...]
    y = buf[0] * r[:, 2:3] + buf[1] * r[:, 3:4]
    o_ref[...] = x_ref[...] + gate_ref[0] * _rms(y, g_ref[0])


def _combine(dims, y, dest, route, x, mods3, gains3, tm):
    rows = route.shape[0]
    nt = rows // tm
    row = lambda width: pl.BlockSpec((tm, width), lambda i: (i, 0))
    return pl.pallas_call(
        functools.partial(_combine_kernel, tm=tm),
        grid=(nt,),
        in_specs=[pl.BlockSpec((1, 1, TOP_K * tm), lambda i: (i, 0, 0), memory_space=pltpu.SMEM),
                  row(LANES), row(dims.d), _mod_spec(dims, tm, 5), _gain_spec(dims, 3),
                  pl.BlockSpec(memory_space=pl.ANY)],
        out_specs=row(dims.d),
        out_shape=jax.ShapeDtypeStruct((rows, dims.d), F32),
        scratch_shapes=[pltpu.VMEM((TOP_K, tm, dims.d), F32), pltpu.SemaphoreType.DMA((TOP_K,))],
        input_output_aliases={2: 0} if rows == x.shape[0] else {},
        compiler_params=_cparams(("arbitrary",)),
    )(dest.reshape(nt, 1, TOP_K * tm), route, x, mods3, gains3, y)


def _routing_tables(route, n_exp, tm_e):
    t = route.shape[0]
    a = t * TOP_K
    flat_e = route[:, :TOP_K].astype(jnp.int32).reshape(a)
    onehot = (flat_e[:, None] == jnp.arange(n_exp)[None, :]).astype(jnp.int32)
    csum = jnp.cumsum(onehot, axis=0)
    rank = jnp.take_along_axis(csum, flat_e[:, None], axis=1)[:, 0] - 1
    counts = csum[-1]
    padded = (counts + tm_e - 1) // tm_e * tm_e
    ends = jnp.cumsum(padded)
    dest = (ends - padded)[flat_e] + rank
    n_blocks = -(-(a + n_exp * (tm_e - 1)) // tm_e)
    src = jnp.zeros((n_blocks * tm_e,), jnp.int32).at[dest].set(jnp.arange(a, dtype=jnp.int32) // TOP_K)
    block_start = jnp.arange(n_blocks, dtype=jnp.int32) * tm_e
    block_e = jnp.minimum(jnp.sum((ends[None, :] <= block_start[:, None]).astype(jnp.int32), axis=1), n_exp - 1)
    n_used = (ends[-1] // tm_e).reshape(1)
    return dest.astype(jnp.int32), src, block_e.astype(jnp.int32), n_used.astype(jnp.int32)


def _even_weights(w_in):
    kw = SWA_KV_HEADS * SWA_HEAD_DIM
    c0 = EVEN_BCU + EVEN_Q
    wk, wv = w_in[:, c0:c0 + kw], w_in[:, c0 + kw:c0 + 2 * kw]
    dup = lambda w: jnp.concatenate(
        [w[:, g * SWA_HEAD_DIM:(g + 1) * SWA_HEAD_DIM] for g in range(SWA_KV_HEADS) for _ in range(2)], axis=1)
    return jnp.concatenate([w_in[:, :c0], dup(wk), dup(wv)], axis=1).astype(BF)


def _odd_weights(w_in, w_q_up, w_kv_up):
    d = w_in.shape[0]
    c = 0
    mq = w_in[:, c:c + MLA_Q_RANK]; c += MLA_Q_RANK
    dq = w_in[:, c:c + DIFF_W]; c += DIFF_W
    kvd = w_in[:, c:c + MLA_KV_RANK]; c += MLA_KV_RANK
    kpe = w_in[:, c:c + MLA_ROPE]; c += MLA_ROPE
    dk = w_in[:, c:c + DIFF_W]; c += DIFF_W
    dv = w_in[:, c:]
    pad_tail = LANES - MLA_NOPE - MLA_ROPE
    kpe_chunk = jnp.concatenate([jnp.zeros((d, MLA_NOPE), F32), kpe, jnp.zeros((d, pad_tail), F32)], axis=1)
    w = jnp.concatenate([mq, kvd, kpe_chunk, dq, dk, dv], axis=1).astype(BF)
    qh = w_q_up.reshape(MLA_Q_RANK, MLA_HEADS, MLA_NOPE + MLA_ROPE)
    wq = jnp.pad(qh, ((0, 0), (0, 0), (0, pad_tail))).reshape(MLA_Q_RANK, MLA_QK).astype(BF)
    kvh = w_kv_up.reshape(MLA_KV_RANK, MLA_HEADS, MLA_NOPE + MLA_V)
    wk = jnp.pad(kvh[:, :, :MLA_NOPE], ((0, 0), (0, 0), (0, LANES - MLA_NOPE))).reshape(MLA_KV_RANK, MLA_QK)
    wv = kvh[:, :, MLA_NOPE:].reshape(MLA_KV_RANK, MLA_VW)
    return w, wq, jnp.concatenate([wk, wv], axis=1).astype(BF)


def kernel(x, c, ctx, c_ctx, w_mod, b_mod, norm_g, w_in_even, conv_w, sink, w_out_even, w_in_odd, mla_q_norm_g,
           mla_kv_norm_g, w_q_up, w_kv_up, diff_lambda, diff_subln_g, w_out_odd, w_ff_gu, w_ff_down, w_router,
           w_exp_gu, w_exp_down):
    bn, s, d = x.shape
    l = ctx.shape[1]
    depth = w_mod.shape[0]
    n_exp = w_router.shape[-1]
    dims = _Dims(bn, s, l, d)
    assert bn < MOD_ROWS and s % l == 0 and l % BLOCK == 0 and s % GRID_W == 0

    tm = _tile(math.gcd(s, bn * l), 512)
    tm_e = 1024 if dims.t_all >= 8192 else 256
    tq = _tile(s, 512)

    cc = jnp.zeros((MOD_ROWS, d), F32).at[:bn].set(c).at[bn].set(c_ctx)
    mods = _modulation(cc, w_mod, b_mod)
    rope64 = _rope_tables(s, tm, SWA_HEAD_DIM, SWA_HEAD_DIM, 0)
    rope32 = _rope_tables(s, tm, MLA_ROPE, LANES, MLA_NOPE)

    xa = jnp.concatenate([x.reshape(bn * s, d), ctx.reshape(bn * l, d)], axis=0)
    for layer in range(depth):
        i = layer // 2
        mods3 = mods[layer].reshape(MOD_ROWS, 1, N_MOD * d)
        gains3 = norm_g[layer].reshape(4, 1, d)
        if layer % 2 == 0:
            bcu, q, k, v = _even_proj(dims, xa, mods3, gains3, _even_weights(w_in_even[i]), rope64, tm)
            conv = _gated_conv(dims, bcu, conv_w[i])
            attn = _win_attn(dims, q, k, v, sink[i])
            wo = w_out_even[i].astype(BF)
            xa = _out_proj(dims, conv, attn, wo[:CONV_WIDTH], wo[CONV_WIDTH:], xa, mods3, gains3, tm)
            act_lo, act_hi = _ffn_up(dims, xa, mods3, gains3, w_ff_gu[i].astype(BF), tm)
            wd = w_ff_down[i].astype(BF)
            xa = _out_proj(dims, act_lo, act_hi, wd[:act_lo.shape[1]], wd[act_lo.shape[1]:], xa, mods3, gains3, tm,
                           gate_col=5, gain_row=3)
        else:
            w, wq, wkv = _odd_weights(w_in_odd[i], w_q_up[i], w_kv_up[i])
            qm, km, vm, dq, dk, dv = _odd_proj(dims, xa, mods3, gains3, w, wq, wkv,
                                               mla_q_norm_g[i].reshape(1, -1), mla_kv_norm_g[i].reshape(1, -1),
                                               rope32, rope64, tm)
            lam_init = 0.8 - 0.6 * math.exp(-0.3 * layer)
            extra = (diff_lambda[i], diff_subln_g[i].reshape(1, -1))
            need_ctx = layer < depth - 1
            rows = dims.t_all if need_ctx else dims.t_lat
            o_m = _full_attn(dims, "mla", qm, km, vm, need_ctx, tq)
            o_d = _full_attn(dims, "diff", dq, dk, dv, need_ctx, tq, extra, lam_init)
            wo = w_out_odd[i].astype(BF)
            xa = _out_proj(dims, o_m, o_d, wo[:MLA_VW], wo[MLA_VW:], xa, mods3, gains3, tm)
            wr = jnp.pad(w_router[i], ((0, 0), (0, LANES - n_exp)))
            f_in, route = _router(dims, xa, rows, mods3, gains3, wr, n_exp, tm)
            dest, src, block_e, n_used = _routing_tables(route, n_exp, tm_e)
            y = _experts(f_in, src, block_e, n_used, w_exp_gu[i], w_exp_down[i], tm_e,
                         _tile(w_exp_down.shape[2], 512))
            xa = _combine(dims, y, dest, route, xa, mods3, gains3, tm)
    return xa[:bn * s].reshape(bn, s, d)
```

```python
import functools
import math

import jax
import jax.numpy as jnp
from jax import lax
from jax.experimental import pallas as pl
from jax.experimental.pallas import tpu as pltpu

F32 = jnp.float32
BF = jnp.bfloat16

EPS = 1e-6
ROPE_THETA = 10000.0
GRID_W = 64
BLOCK = 128
N_MOD = 6
LANES = 128
MXU_N = 256
MOD_ROWS = 16

CONV_WIDTH = 512
SWA_HEADS = 8
SWA_KV_HEADS = 2
SWA_HEAD_DIM = 64
MLA_HEADS = 8
MLA_Q_RANK = 256
MLA_KV_RANK = 128
MLA_NOPE = 64
MLA_ROPE = 32
MLA_V = 64
DIFF_HEADS = 4
DIFF_DIM = 64
DIFF_V = 2 * DIFF_DIM
TOP_K = 2

NEG = -1e30
LOG2E = math.log2(math.e)
KEY_CHUNK = 1024


def _tile(n, pref):
    if n <= pref:
        return n
    t = pref - pref % LANES
    while t >= LANES:
        if n % t == 0:
            return t
        t -= LANES
    raise ValueError((n, pref))


def _cparams(sem, vmem_mb=None):
    kw = dict(dimension_semantics=sem)
    if vmem_mb is not None:
        kw["vmem_limit_bytes"] = vmem_mb << 20
    return pltpu.CompilerParams(**kw)


def _dot(a, b):
    return jnp.dot(a, b, preferred_element_type=F32)


def _dot_nt(a, b):
    return lax.dot_general(a, b, (((1,), (1,)), ((), ())), preferred_element_type=F32)


def _rms(x, g):
    return x * lax.rsqrt(jnp.mean(x * x, axis=-1, keepdims=True) + EPS) * g


def _norm_mod(x, g, shift, scale):
    return _rms(x, g) * (1.0 + scale) + shift


def _silu(x):
    return x / (1.0 + jnp.exp(-x))


def _rope(x, cos, sa, sb, half):
    out = []
    for c in range(x.shape[1] // LANES):
        xc = x[:, c * LANES:(c + 1) * LANES]
        out.append(xc * cos + pltpu.roll(xc, LANES - half, 1) * sa + pltpu.roll(xc, half, 1) * sb)
    return out[0] if len(out) == 1 else jnp.concatenate(out, axis=1)


def _mod_kernel(cc_ref, w_ref, b_ref, o_ref):
    a = _silu(cc_ref[...]).astype(BF)
    o_ref[0] = _dot(a, w_ref[0].astype(BF)) + b_ref[0]


def _modulation(cc, w_mod, b_mod):
    depth, d, n = w_mod.shape
    tn = _tile(n, 1536)
    return pl.pallas_call(
        _mod_kernel,
        grid=(depth, n // tn),
        in_specs=[
            pl.BlockSpec((MOD_ROWS, d), lambda l, j: (0, 0)),
            pl.BlockSpec((1, d, tn), lambda l, j: (l, 0, j)),
            pl.BlockSpec((1, 1, tn), lambda l, j: (l, 0, j)),
        ],
        out_specs=pl.BlockSpec((1, MOD_ROWS, tn), lambda l, j: (l, 0, j)),
        out_shape=jax.ShapeDtypeStruct((depth, MOD_ROWS, n), F32),
        compiler_params=_cparams(("arbitrary", "arbitrary"), 40),
    )(cc, w_mod, b_mod.reshape(depth, 1, n))


class _Dims:
    def __init__(self, bn, s, l, d):
        self.bn, self.s, self.l, self.d = bn, s, l, d
        self.t_lat = bn * s
        self.t_all = bn * s + bn * l

    def mod_row(self, i, tm):
        return jnp.minimum(i * tm // self.s, self.bn)


def _mod_spec(dims, tm, col):
    return pl.BlockSpec((1, 1, dims.d), lambda i, *_: (dims.mod_row(i, tm), 0, col))


def _gain_spec(dims, k):
    return pl.BlockSpec((1, 1, dims.d), lambda i, *_: (k, 0, 0))


def _rope_spec(dims, tm):
    n_lat, per_seq = dims.t_lat // tm, dims.s // tm
    return pl.BlockSpec((tm, LANES), lambda i: (jnp.where(i < n_lat, i % per_seq, per_seq), 0))


def _full_spec(shape):
    return pl.BlockSpec(shape, lambda *_: (0,) * len(shape))


def _rope_tables(s, pad_rows, rot_dim, period, lane_off):
    axis_dim = rot_dim // 2
    half = axis_dim // 2
    inv = 1.0 / (ROPE_THETA ** (jnp.arange(0, axis_dim, 2, dtype=F32) / axis_dim))
    pos = jnp.arange(s)
    rows = (pos // GRID_W).astype(F32)[:, None]
    cols = (pos % GRID_W).astype(F32)[:, None]
    lane = jnp.arange(LANES)
    dd = lane % period - lane_off
    active = (dd >= 0) & (dd < rot_dim)
    dd = jnp.clip(dd, 0, rot_dim - 1)
    j = dd % axis_dim
    ang = jnp.where((dd // axis_dim == 0)[None, :], rows, cols) * inv[j % half][None, :]
    first = (j < half)[None, :]
    act = active[None, :]
    cos = jnp.where(act, jnp.cos(ang), 1.0)
    sin = jnp.where(act, jnp.sin(ang), 0.0)
    sa = jnp.where(first, -sin, 0.0)
    sb = jnp.where(first, 0.0, sin)
    pad = lambda t, v: jnp.concatenate([t, jnp.full((pad_rows, LANES), v, F32)], axis=0)
    return pad(cos, 1.0), pad(sa, 0.0), pad(sb, 0.0), half


EVEN_BCU = 3 * CONV_WIDTH
EVEN_Q = SWA_HEADS * SWA_HEAD_DIM
EVEN_KD = SWA_KV_HEADS * LANES


def _even_proj_kernel(*refs, half, q_scale, n_lat_tiles):
    if n_lat_tiles is None:
        x_ref, g_ref, sh_ref, sc_ref, w_ref, cos_ref, sa_ref, sb_ref, bcu_ref, q_ref, k_ref, v_ref = refs
        x = x_ref[...]
    else:
        (xl_ref, xc_ref, g_ref, sh_ref, sc_ref, w_ref, cos_ref, sa_ref, sb_ref,
         bcu_ref, q_ref, k_ref, v_ref, xa_ref) = refs
        x = jnp.where(pl.program_id(0) < n_lat_tiles, xl_ref[...], xc_ref[...])
        xa_ref[...] = x
    h = _norm_mod(x, g_ref[0], sh_ref[0], sc_ref[0]).astype(BF)
    c0, c1, c2 = EVEN_BCU, EVEN_BCU + EVEN_Q, EVEN_BCU + EVEN_Q + EVEN_KD
    bcu_ref[...] = _dot(h, w_ref[:, :c0]).astype(BF)
    qk = _rope(_dot(h, w_ref[:, c0:c2]), cos_ref[...], sa_ref[...], sb_ref[...], half)
    q_ref[...] = (qk[:, :EVEN_Q] * q_scale).astype(BF)
    k_ref[...] = qk[:, EVEN_Q:].astype(BF)
    v_ref[...] = _dot(h, w_ref[:, c2:]).astype(BF)


def _even_proj(dims, xs, mods3, gains3, w, rope, tm):
    n = w.shape[1]
    cos, sa, sb, half = rope
    row = lambda width: pl.BlockSpec((tm, width), lambda i: (i, 0))
    out_w = (EVEN_BCU, EVEN_Q, EVEN_KD, EVEN_KD)
    out_specs = [row(wd) for wd in out_w]
    out_shape = [jax.ShapeDtypeStruct((dims.t_all, wd), BF) for wd in out_w]
    if len(xs) == 1:
        n_lat, x_specs = None, [row(dims.d)]
    else:
        n_lat = dims.t_lat // tm
        x_specs = [pl.BlockSpec((tm, dims.d), lambda i: (jnp.minimum(i, n_lat - 1), 0)),
                   pl.BlockSpec((tm, dims.d), lambda i: (jnp.maximum(i - n_lat, 0), 0))]
        out_specs.append(row(dims.d))
        out_shape.append(jax.ShapeDtypeStruct((dims.t_all, dims.d), F32))
    return pl.pallas_call(
        functools.partial(_even_proj_kernel, half=half, q_scale=SWA_HEAD_DIM ** -0.5 * LOG2E, n_lat_tiles=n_lat),
        grid=(dims.t_all // tm,),
        in_specs=x_specs + [_gain_spec(dims, 0), _mod_spec(dims, tm, 0), _mod_spec(dims, tm, 1),
                            _full_spec((dims.d, n)), _rope_spec(dims, tm), _rope_spec(dims, tm),
                            _rope_spec(dims, tm)],
        out_specs=out_specs,
        out_shape=out_shape,
        compiler_params=_cparams(("arbitrary",), 48),
    )(*xs, gains3, mods3, mods3, w, cos, sa, sb)


def _conv_kernel(b_ref, c_ref, u_ref, cp_ref, up_ref, cn_ref, un_ref, w_ref, o_ref, *, n_lat_blocks, per_seq):
    i = pl.program_id(0)
    rows = b_ref.shape[0]
    hr = cp_ref.shape[0]
    pos = i % per_seq
    is_lat = i < n_lat_blocks
    has_prev = jnp.logical_and(is_lat, pos > 0).astype(F32)
    has_next = jnp.logical_and(is_lat, pos < per_seq - 1).astype(F32)
    cu = c_ref[...].astype(F32) * u_ref[...].astype(F32)
    cu_p = (cp_ref[...].astype(F32) * up_ref[...].astype(F32))[hr - 1:hr] * has_prev
    cu_n = (cn_ref[...].astype(F32) * un_ref[...].astype(F32))[0:1] * has_next
    r = lax.broadcasted_iota(jnp.int32, cu.shape, 0)
    prev = jnp.where(r == 0, cu_p, pltpu.roll(cu, 1, 0))
    nxt = jnp.where(r == rows - 1, cu_n, pltpu.roll(cu, rows - 1, 0))
    w = w_ref[...]
    o_ref[...] = (b_ref[...].astype(F32) * (prev * w[0:1] + cu * w[1:2] + nxt * w[2:3])).astype(BF)


def _gated_conv(dims, bcu, conv_w):
    rows = dims.l
    hr = 16
    nblk = dims.t_all // rows
    per = rows // hr
    last = dims.t_all // hr - 1
    cw = CONV_WIDTH
    main = lambda col: pl.BlockSpec((rows, cw), lambda i: (i, col))
    prev = lambda col: pl.BlockSpec((hr, cw), lambda i: (jnp.maximum(i * per - 1, 0), col))
    nxt = lambda col: pl.BlockSpec((hr, cw), lambda i: (jnp.minimum((i + 1) * per, last), col))
    return pl.pallas_call(
        functools.partial(_conv_kernel, n_lat_blocks=dims.t_lat // rows, per_seq=dims.s // rows),
        grid=(nblk,),
        in_specs=[main(0), main(1), main(2), prev(1), prev(2), nxt(1), nxt(2), _full_spec(conv_w.shape)],
        out_specs=pl.BlockSpec((rows, cw), lambda i: (i, 0)),
        out_shape=jax.ShapeDtypeStruct((dims.t_all, cw), BF),
        compiler_params=_cparams(("arbitrary",)),
    )(bcu, bcu, bcu, bcu, bcu, bcu, bcu, conv_w)


def _win_attn_kernel(sink_ref, q_ref, kp_ref, ko_ref, kn_ref, kc_ref, vp_ref, vo_ref, vn_ref, vc_ref,
                     o_ref, *, nb):
    n = pl.program_id(1)
    blk = q_ref.shape[0]
    n_ctx = kc_ref.shape[0]
    is_lat = n < nb
    lo_s = jnp.where(is_lat, jnp.where(n >= 1, 0, blk), 3 * blk)
    hi_s = jnp.where(is_lat, jnp.where(n + 1 < nb, 3 * blk, 2 * blk), 0)
    shape = (blk, 3 * blk + n_ctx)
    c = lax.broadcasted_iota(jnp.int32, shape, 1)
    r = lax.broadcasted_iota(jnp.int32, shape, 0)
    ok = (c >= 3 * blk) | ((c >= r) & (c - 2 * blk <= r) & (c >= lo_s) & (c < hi_s))
    bias = jnp.where(ok, 0.0, NEG)
    kall = jnp.concatenate([kp_ref[...], ko_ref[...], kn_ref[...], kc_ref[...]], axis=0)
    vall = jnp.concatenate([vp_ref[...], vo_ref[...], vn_ref[...], vc_ref[...]], axis=0)
    lane = lax.broadcasted_iota(jnp.int32, (1, LANES), 1)
    keep = ((lane < LANES // 2).astype(F32).astype(BF), (lane >= LANES // 2).astype(F32).astype(BF))
    lo = lax.broadcasted_iota(jnp.int32, (blk, LANES), 1) < LANES // 2
    per_group = SWA_HEADS // SWA_KV_HEADS

    def score(head):
        pair, sub, g = head // 2, head % 2, head // per_group
        q2 = q_ref[:, pair * LANES:(pair + 1) * LANES]
        return _dot_nt(q2 * keep[sub], kall[:, g * LANES:(g + 1) * LANES]) + bias

    s_next = score(0)
    outs = []
    for head in range(SWA_HEADS):
        s = s_next
        if head + 1 < SWA_HEADS:
            s_next = score(head + 1)
        g = head // per_group
        sk = jnp.full((blk, 1), sink_ref[head], F32) * LOG2E
        m = jnp.maximum(jnp.max(s, axis=-1, keepdims=True), sk)
        p = jnp.exp2(s - m)
        den = jnp.sum(p, axis=-1, keepdims=True) + jnp.exp2(sk - m)
        outs.append(_dot(p.astype(BF), vall[:, g * LANES:(g + 1) * LANES]) / den)
        if head % 2 == 1:
            pair = head // 2
            o_ref[:, pair * LANES:(pair + 1) * LANES] = jnp.where(lo, outs[-2], outs[-1]).astype(BF)


def _win_attn(dims, q, k, v, sink):
    nb = dims.s // BLOCK
    nc = dims.l // BLOCK
    lat_blocks = dims.t_lat // BLOCK
    ctx_blk0 = dims.t_lat // dims.l

    def own(b, n):
        return jnp.where(n < nb, b * nb + n, lat_blocks + b * nc + (n - nb))

    def prev(b, n):
        return jnp.where(n < nb, b * nb + jnp.maximum(n - 1, 0), own(b, n))

    def nxt(b, n):
        return jnp.where(n < nb, b * nb + jnp.minimum(n + 1, nb - 1), own(b, n))

    kv = lambda f: pl.BlockSpec((BLOCK, EVEN_KD), lambda b, n: (f(b, n), 0))
    ctx = pl.BlockSpec((dims.l, EVEN_KD), lambda b, n: (ctx_blk0 + b, 0))
    qo = pl.BlockSpec((BLOCK, EVEN_Q), lambda b, n: (own(b, n), 0))
    return pl.pallas_call(
        functools.partial(_win_attn_kernel, nb=nb),
        grid=(dims.bn, nb + nc),
        in_specs=[pl.BlockSpec(memory_space=pltpu.SMEM), qo, kv(prev), kv(own), kv(nxt), ctx,
                  kv(prev), kv(own), kv(nxt), ctx],
        out_specs=qo,
        out_shape=jax.ShapeDtypeStruct((dims.t_all, EVEN_Q), BF),
        compiler_params=_cparams(("arbitrary", "arbitrary")),
    )(sink, q, k, k, k, k, v, v, v, v)


def _out_proj_kernel(a1_ref, a2_ref, w1_ref, w2_ref, x_ref, gate_ref, g_ref, o_ref):
    y = _dot(a1_ref[...], w1_ref[...]) + _dot(a2_ref[...], w2_ref[...])
    o_ref[...] = x_ref[...] + gate_ref[0] * _rms(y, g_ref[0])


def _out_proj(dims, a1, a2, w1, w2, x, mods3, gains3, tm, gate_col=2, gain_row=1):
    row = lambda width: pl.BlockSpec((tm, width), lambda i: (i, 0))
    return pl.pallas_call(
        _out_proj_kernel,
        grid=(a1.shape[0] // tm,),
        in_specs=[row(a1.shape[1]), row(a2.shape[1]), _full_spec(w1.shape), _full_spec(w2.shape),
                  row(dims.d), _mod_spec(dims, tm, gate_col), _gain_spec(dims, gain_row)],
        out_specs=row(dims.d),
        out_shape=jax.ShapeDtypeStruct(x.shape, F32),
        input_output_aliases={4: 0},
        compiler_params=_cparams(("arbitrary",), 40),
    )(a1, a2, w1, w2, x, mods3, gains3)


def _ffn_up_kernel(x_ref, g_ref, sh_ref, sc_ref, w_ref, lo_ref, hi_ref, *, ff):
    h = _norm_mod(x_ref[...], g_ref[0], sh_ref[0], sc_ref[0]).astype(BF)
    c0 = 0
    for ref in (lo_ref, hi_ref):
        c1 = c0 + ref.shape[1]
        ref[...] = (_silu(_dot(h, w_ref[:, c0:c1])) * _dot(h, w_ref[:, ff + c0:ff + c1])).astype(BF)
        c0 = c1


def _ffn_split(ff):
    lo = (ff // 2 + MXU_N - 1) // MXU_N * MXU_N
    return (lo, ff - lo) if 0 < lo < ff else (ff // 2, ff - ff // 2)


def _ffn_up(dims, x, mods3, gains3, w_gu, tm):
    ff = w_gu.shape[1] // 2
    row = lambda width: pl.BlockSpec((tm, width), lambda i: (i, 0))
    widths = _ffn_split(ff)
    return pl.pallas_call(
        functools.partial(_ffn_up_kernel, ff=ff),
        grid=(dims.t_all // tm,),
        in_specs=[row(dims.d), _gain_spec(dims, 2), _mod_spec(dims, tm, 3), _mod_spec(dims, tm, 4),
                  _full_spec(w_gu.shape)],
        out_specs=[row(wd) for wd in widths],
        out_shape=[jax.ShapeDtypeStruct((dims.t_all, wd), BF) for wd in widths],
        compiler_params=_cparams(("arbitrary",), 56),
    )(x, gains3, mods3, mods3, w_gu)


ODD_STAGE1 = MLA_Q_RANK + MLA_KV_RANK + LANES
MLA_QK = MLA_HEADS * LANES
MLA_VW = MLA_HEADS * MLA_V
DIFF_W = DIFF_HEADS * 2 * DIFF_DIM


def _odd_proj_kernel(x_ref, g_ref, sh_ref, sc_ref, w_ref, wq_ref, wkv_ref, qg_ref, kvg_ref,
                     c32_ref, a32_ref, b32_ref, c64_ref, a64_ref, b64_ref,
                     q_ref, k_ref, v_ref, dq_ref, dk_ref, dv_ref, *, half32, half64, mla_scale, diff_scale):
    h = _norm_mod(x_ref[...], g_ref[0], sh_ref[0], sc_ref[0]).astype(BF)
    r32 = (c32_ref[...], a32_ref[...], b32_ref[...], half32)
    r64 = (c64_ref[...], a64_ref[...], b64_ref[...], half64)
    s1 = _dot(h, w_ref[:, :ODD_STAGE1])
    qn = _rms(s1[:, :MLA_Q_RANK], qg_ref[...]).astype(BF)
    kvn = _rms(s1[:, MLA_Q_RANK:MLA_Q_RANK + MLA_KV_RANK], kvg_ref[...]).astype(BF)
    kpe = _rope(s1[:, MLA_Q_RANK + MLA_KV_RANK:], *r32)
    q_ref[...] = (_rope(_dot(qn, wq_ref[...]), *r32) * mla_scale).astype(BF)
    kn = _dot(kvn, wkv_ref[:, :MLA_QK])
    k_ref[...] = (kn + jnp.concatenate([kpe] * MLA_HEADS, axis=1)).astype(BF)
    v_ref[...] = _dot(kvn, wkv_ref[:, MLA_QK:]).astype(BF)
    c0 = ODD_STAGE1
    dq_ref[...] = (_rope(_dot(h, w_ref[:, c0:c0 + DIFF_W]), *r64) * diff_scale).astype(BF)
    dk_ref[...] = _rope(_dot(h, w_ref[:, c0 + DIFF_W:c0 + 2 * DIFF_W]), *r64).astype(BF)
    dv_ref[...] = _dot(h, w_ref[:, c0 + 2 * DIFF_W:]).astype(BF)


def _odd_proj(dims, x, mods3, gains3, w, wq, wkv, qg, kvg, rope32, rope64, tm):
    row = lambda width: pl.BlockSpec((tm, width), lambda i: (i, 0))
    out_w = (MLA_QK, MLA_QK, MLA_VW, DIFF_W, DIFF_W, DIFF_HEADS * DIFF_V)
    rs = _rope_spec(dims, tm)
    return pl.pallas_call(
        functools.partial(_odd_proj_kernel, half32=rope32[3], half64=rope64[3],
                          mla_scale=(MLA_NOPE + MLA_ROPE) ** -0.5 * LOG2E, diff_scale=DIFF_DIM ** -0.5 * LOG2E),
        grid=(dims.t_all // tm,),
        in_specs=[row(dims.d), _gain_spec(dims, 0), _mod_spec(dims, tm, 0), _mod_spec(dims, tm, 1),
                  _full_spec(w.shape), _full_spec(wq.shape), _full_spec(wkv.shape),
                  _full_spec(qg.shape), _full_spec(kvg.shape), rs, rs, rs, rs, rs, rs],
        out_specs=[row(wd) for wd in out_w],
        out_shape=[jax.ShapeDtypeStruct((dims.t_all, wd), BF) for wd in out_w],
        compiler_params=_cparams(("arbitrary",), 48),
    )(x, gains3, mods3, mods3, w, wq, wkv, qg, kvg, *rope32[:3], *rope64[:3])


def _softmax_pv(streams, k_refs, v_refs):
    tasks = []
    for si, (q, cols_k, cols_v) in enumerate(streams):
        for kr, vr in zip(k_refs, v_refs):
            n = kr.shape[0]
            for c0 in range(0, n, KEY_CHUNK):
                tasks.append((si, q, kr, vr, c0, min(c0 + KEY_CHUNK, n), cols_k, cols_v))
    score = lambda t: _dot_nt(t[1], t[2][t[4]:t[5], t[6]])
    state = [None] * len(streams)
    s_next = score(tasks[0])
    for ti, t in enumerate(tasks):
        s = s_next
        if ti + 1 < len(tasks):
            s_next = score(tasks[ti + 1])
        si, _, _, vr, c0, c1, _, cols_v = t
        mc = jnp.max(s, axis=-1, keepdims=True)
        if state[si] is None:
            m_new = mc
        else:
            m, l, acc = state[si]
            m_new = jnp.maximum(m, mc)
        p = jnp.exp2(s - m_new)
        ps = jnp.sum(p, axis=-1, keepdims=True)
        pv = _dot(p.astype(BF), vr[c0:c1, cols_v])
        if state[si] is None:
            state[si] = (m_new, ps, pv)
        else:
            a = jnp.exp2(m - m_new)
            state[si] = (m_new, a * l + ps, a * acc + pv)
    return [acc / l for _, l, acc in state]


def _mla_attn_kernel(q_ref, *refs, n_seg):
    k_refs, v_refs, o_ref = refs[:n_seg], refs[n_seg:2 * n_seg], refs[2 * n_seg]
    cols = [slice(sub * LANES, (sub + 1) * LANES) for sub in range(2)]
    outs = _softmax_pv([(q_ref[:, c], c, slice(None)) for c in cols], k_refs, v_refs)
    lo = lax.broadcasted_iota(jnp.int32, outs[0].shape, 1) < MLA_V
    o_ref[...] = jnp.where(lo, outs[0], outs[1]).astype(BF)


def _diff_attn_kernel(lam_ref, sg_ref, q_ref, *refs, n_seg, lam_init):
    k_refs, v_refs, o_ref = refs[:n_seg], refs[n_seg:2 * n_seg], refs[2 * n_seg]
    lp = lam_ref[...]
    lam = (jnp.exp(jnp.sum(lp[0:1] * lp[1:2], axis=-1, keepdims=True))
           - jnp.exp(jnp.sum(lp[2:3] * lp[3:4], axis=-1, keepdims=True)) + lam_init)
    lane = lax.broadcasted_iota(jnp.int32, (1, LANES), 1)
    q = q_ref[...]
    full = slice(None)
    o1, o2 = _softmax_pv([(q * (lane < DIFF_DIM).astype(F32).astype(BF), full, full),
                          (q * (lane >= DIFF_DIM).astype(F32).astype(BF), full, full)], k_refs, v_refs)
    o_ref[...] = (_rms(o1 - lam * o2, sg_ref[...]) * (1.0 - lam_init)).astype(BF)


def _full_attn_call(dims, kind, q, k, v, ctx_queries, tq, extra, lam_init):
    if kind == "mla":
        n_h, qw, vw = MLA_HEADS // 2, 2 * LANES, LANES
    else:
        n_h, qw, vw = DIFF_HEADS, LANES, LANES
    ctx_blk0 = dims.t_lat // dims.l
    ctx_seg = (dims.l, lambda b: ctx_blk0 + b)
    if ctx_queries:
        tq, nq, out_rows = dims.l, 1, dims.bn * dims.l
        q_row = lambda b, qi: ctx_blk0 + b
        o_row = lambda b, qi: b
        seg = [ctx_seg]
    else:
        nq, out_rows = dims.s // tq, dims.t_lat
        q_row = o_row = lambda b, qi: b * nq + qi
        seg = [(dims.s, lambda b: b), ctx_seg]
    kv_specs = lambda w: [pl.BlockSpec((rows, w), (lambda f: lambda b, hh, qi: (f(b), hh))(f)) for rows, f in seg]
    if kind == "mla":
        body = functools.partial(_mla_attn_kernel, n_seg=len(seg))
    else:
        body = functools.partial(_diff_attn_kernel, n_seg=len(seg), lam_init=lam_init)
    return pl.pallas_call(
        body,
        grid=(dims.bn, n_h, nq),
        in_specs=([_full_spec(e.shape) for e in extra]
                  + [pl.BlockSpec((tq, qw), lambda b, hh, qi: (q_row(b, qi), hh))] + kv_specs(qw) + kv_specs(vw)),
        out_specs=pl.BlockSpec((tq, vw), lambda b, hh, qi: (o_row(b, qi), hh)),
        out_shape=jax.ShapeDtypeStruct((out_rows, n_h * vw), BF),
        compiler_params=_cparams(("arbitrary", "arbitrary", "arbitrary"), 56),
    )(*extra, q, *([k] * len(seg)), *([v] * len(seg)))


def _full_attn(dims, kind, q, k, v, need_ctx, tq, extra=(), lam_init=0.0):
    lat = _full_attn_call(dims, kind, q, k, v, False, tq, extra, lam_init)
    if not need_ctx:
        return lat
    return jnp.concatenate([lat, _full_attn_call(dims, kind, q, k, v, True, tq, extra, lam_init)], axis=0)


def _router_kernel(x_ref, g_ref, sh_ref, sc_ref, wr_ref, f_ref, r_ref, *, n_exp):
    f = _norm_mod(x_ref[...], g_ref[0], sh_ref[0], sc_ref[0])
    f_ref[...] = f
    logits = jnp.dot(f, wr_ref[...], preferred_element_type=F32, precision=lax.Precision.HIGHEST)
    lane = lax.broadcasted_iota(jnp.int32, logits.shape, 1).astype(F32)
    logits = jnp.where(lane < n_exp, logits, NEG)
    m1 = jnp.max(logits, axis=-1, keepdims=True)
    i1 = jnp.min(jnp.where(logits == m1, lane, float(LANES)), axis=-1, keepdims=True)
    rest = jnp.where(lane == i1, NEG, logits)
    m2 = jnp.max(rest, axis=-1, keepdims=True)
    i2 = jnp.min(jnp.where(rest == m2, lane, float(LANES)), axis=-1, keepdims=True)
    e2 = jnp.exp(m2 - m1)
    w1 = 1.0 / (1.0 + e2)
    w2 = e2 / (1.0 + e2)
    r_ref[...] = jnp.where(lane == 0, i1, jnp.where(lane == 1, i2, jnp.where(lane == 2, w1,
                           jnp.where(lane == 3, w2, 0.0))))


def _router(dims, x, rows, mods3, gains3, w_router_pad, n_exp, tm):
    row = lambda width: pl.BlockSpec((tm, width), lambda i: (i, 0))
    return pl.pallas_call(
        functools.partial(_router_kernel, n_exp=n_exp),
        grid=(rows // tm,),
        in_specs=[row(dims.d), _gain_spec(dims, 2), _mod_spec(dims, tm, 3), _mod_spec(dims, tm, 4),
                  _full_spec(w_router_pad.shape)],
        out_specs=[row(dims.d), row(LANES)],
        out_shape=[jax.ShapeDtypeStruct((rows, dims.d), F32),
                   jax.ShapeDtypeStruct((rows, LANES), F32)],
        compiler_params=_cparams(("arbitrary",), 40),
    )(x, gains3, mods3, mods3, w_router_pad)


def _row_copy(src_hbm, dst, s, d, sem):
    return pltpu.make_async_copy(src_hbm.at[pl.ds(s, 1)], dst.at[pl.ds(d, 1)], sem)


def _expert_kernel(be_ref, nu_ref, src_cur, src_nxt, x_hbm, wg_ref, wu_ref, wd_ref, o_ref,
                   xbuf, sem, h_s, acc_s, *, tm, nj):
    blk, j = pl.program_id(0), pl.program_id(1)
    nblk = pl.num_programs(0)
    n_used = nu_ref[0]
    used = blk < n_used
    slot = blk % 2
    per_step = tm // nj
    head = tm - per_step * nj

    def issue(src_ref, dst_slot, r):
        _row_copy(x_hbm, xbuf.at[dst_slot], src_ref[0, 0, r], r, sem.at[dst_slot]).start()

    def wait_slot(s):
        pltpu.make_async_copy(x_hbm.at[pl.ds(0, tm)], xbuf.at[s], sem.at[s]).wait()

    @pl.when(j == 0)
    def _():
        @pl.when(blk == 0)
        def _():
            def first(r, c):
                issue(src_cur, 0, r)
                return c

            lax.fori_loop(0, tm, first, 0, unroll=8)

        @pl.when(blk <= n_used)
        def _():
            wait_slot(slot)

        @pl.when(used)
        def _():
            h_s[...] = xbuf[slot].astype(BF)
            for r in range(head):
                issue(src_nxt, 1 - slot, r)

        acc_s[...] = jnp.zeros_like(acc_s)

    @pl.when(used)
    def _():
        for r in range(per_step):
            issue(src_nxt, 1 - slot, head + j * per_step + r)
        h = h_s[...]
        act = (_silu(_dot(h, wg_ref[0, 0].astype(BF))) * _dot(h, wu_ref[0, 0].astype(BF))).astype(BF)
        acc_s[...] += _dot(act, wd_ref[0, 0].astype(BF))

    @pl.when(j == nj - 1)
    def _():
        o_ref[...] = acc_s[...]

        @pl.when(jnp.logical_and(blk == nblk - 1, used))
        def _():
            wait_slot(1 - slot)


def _experts(f_in, src, block_e, n_used, w_gu, w_down, layer_i, tm, tn):
    d = f_in.shape[1]
    ff = w_down.shape[2]
    nj = ff // tn
    nblk = src.shape[0] // tm
    jj = lambda blk, j, nu: jnp.where(blk < nu[0], j, nj - 1)
    src3 = src.reshape(nblk, 1, tm)
    return pl.pallas_call(
        functools.partial(_expert_kernel, tm=tm, nj=nj),
        grid_spec=pltpu.PrefetchScalarGridSpec(
            num_scalar_prefetch=2,
            grid=(nblk, nj),
            in_specs=[pl.BlockSpec((1, 1, tm), lambda blk, j, be, nu: (blk, 0, 0), memory_space=pltpu.SMEM),
                      pl.BlockSpec((1, 1, tm), lambda blk, j, be, nu: (jnp.minimum(blk + 1, nblk - 1), 0, 0),
                                   memory_space=pltpu.SMEM),
                      pl.BlockSpec(memory_space=pl.ANY),
                      pl.BlockSpec((1, 1, d, tn), lambda blk, j, be, nu: (layer_i, be[blk], 0, jj(blk, j, nu))),
                      pl.BlockSpec((1, 1, d, tn), lambda blk, j, be, nu: (layer_i, be[blk], 0, nj + jj(blk, j, nu))),
                      pl.BlockSpec((1, 1, tn, d), lambda blk, j, be, nu: (layer_i, be[blk], jj(blk, j, nu), 0))],
            out_specs=pl.BlockSpec((tm, d), lambda blk, j, be, nu: (blk, 0)),
            scratch_shapes=[pltpu.VMEM((2, tm, d), F32), pltpu.SemaphoreType.DMA((2,)),
                            pltpu.VMEM((tm, d), BF), pltpu.VMEM((tm, d), F32)]),
        out_shape=jax.ShapeDtypeStruct((nblk * tm, d), F32),
        compiler_params=_cparams(("arbitrary", "arbitrary"), 56),
    )(block_e, n_used, src3, src3, f_in, w_gu, w_gu, w_down)


def _combine_kernel(dest_ref, r_ref, x_ref, gate_ref, g_ref, y_hbm, o_ref, buf, sem, *, tm):
    def issue(t, c):
        for k in range(TOP_K):
            _row_copy(y_hbm, buf.at[k], dest_ref[0, 0, TOP_K * t + k], t, sem.at[k]).start()
        return c

    lax.fori_loop(0, tm, issue, 0, unroll=4)
    for k in range(TOP_K):
        pltpu.make_async_copy(y_hbm.at[pl.ds(0, tm)], buf.at[k], sem.at[k]).wait()
    r = r_ref[...]
    y = buf[0] * r[:, 2:3] + buf[1] * r[:, 3:4]
    o_ref[...] = x_ref[...] + gate_ref[0] * _rms(y, g_ref[0])


def _combine(dims, y, dest, route, x, mods3, gains3, tm):
    rows = route.shape[0]
    nt = rows // tm
    row = lambda width: pl.BlockSpec((tm, width), lambda i: (i, 0))
    return pl.pallas_call(
        functools.partial(_combine_kernel, tm=tm),
        grid=(nt,),
        in_specs=[pl.BlockSpec((1, 1, TOP_K * tm), lambda i: (i, 0, 0), memory_space=pltpu.SMEM),
                  row(LANES), row(dims.d), _mod_spec(dims, tm, 5), _gain_spec(dims, 3),
                  pl.BlockSpec(memory_space=pl.ANY)],
        out_specs=row(dims.d),
        out_shape=jax.ShapeDtypeStruct((rows, dims.d), F32),
        scratch_shapes=[pltpu.VMEM((TOP_K, tm, dims.d), F32), pltpu.SemaphoreType.DMA((TOP_K,))],
        input_output_aliases={2: 0} if rows == x.shape[0] else {},
        compiler_params=_cparams(("arbitrary",)),
    )(dest.reshape(nt, 1, TOP_K * tm), route, x, mods3, gains3, y)


def _routing_tables(route, n_exp, tm_e):
    t = route.shape[0]
    a = t * TOP_K
    flat_e = route[:, :TOP_K].astype(jnp.int32).reshape(a)
    onehot = (flat_e[:, None] == jnp.arange(n_exp)[None, :]).astype(jnp.int32)
    csum = jnp.cumsum(onehot, axis=0)
    rank = jnp.take_along_axis(csum, flat_e[:, None], axis=1)[:, 0] - 1
    counts = csum[-1]
    padded = (counts + tm_e - 1) // tm_e * tm_e
    ends = jnp.cumsum(padded)
    dest = (ends - padded)[flat_e] + rank
    n_blocks = -(-(a + n_exp * (tm_e - 1)) // tm_e)
    src = jnp.zeros((n_blocks * tm_e,), jnp.int32).at[dest].set(jnp.arange(a, dtype=jnp.int32) // TOP_K)
    block_start = jnp.arange(n_blocks, dtype=jnp.int32) * tm_e
    block_e = jnp.minimum(jnp.sum((ends[None, :] <= block_start[:, None]).astype(jnp.int32), axis=1), n_exp - 1)
    n_used = (ends[-1] // tm_e).reshape(1)
    return dest.astype(jnp.int32), src, block_e.astype(jnp.int32), n_used.astype(jnp.int32)


def _even_weights(w_in):
    kw = SWA_KV_HEADS * SWA_HEAD_DIM
    c0 = EVEN_BCU + EVEN_Q
    wk, wv = w_in[:, c0:c0 + kw], w_in[:, c0 + kw:c0 + 2 * kw]
    dup = lambda w: jnp.concatenate(
        [w[:, g * SWA_HEAD_DIM:(g + 1) * SWA_HEAD_DIM] for g in range(SWA_KV_HEADS) for _ in range(2)], axis=1)
    return jnp.concatenate([w_in[:, :c0], dup(wk), dup(wv)], axis=1).astype(BF)


def _odd_weights(w_in, w_q_up, w_kv_up):
    d = w_in.shape[0]
    c = 0
    mq = w_in[:, c:c + MLA_Q_RANK]; c += MLA_Q_RANK
    dq = w_in[:, c:c + DIFF_W]; c += DIFF_W
    kvd = w_in[:, c:c + MLA_KV_RANK]; c += MLA_KV_RANK
    kpe = w_in[:, c:c + MLA_ROPE]; c += MLA_ROPE
    dk = w_in[:, c:c + DIFF_W]; c += DIFF_W
    dv = w_in[:, c:]
    pad_tail = LANES - MLA_NOPE - MLA_ROPE
    kpe_chunk = jnp.concatenate([jnp.zeros((d, MLA_NOPE), F32), kpe, jnp.zeros((d, pad_tail), F32)], axis=1)
    w = jnp.concatenate([mq, kvd, kpe_chunk, dq, dk, dv], axis=1).astype(BF)
    qh = w_q_up.reshape(MLA_Q_RANK, MLA_HEADS, MLA_NOPE + MLA_ROPE)
    wq = jnp.pad(qh, ((0, 0), (0, 0), (0, pad_tail))).reshape(MLA_Q_RANK, MLA_QK).astype(BF)
    kvh = w_kv_up.reshape(MLA_KV_RANK, MLA_HEADS, MLA_NOPE + MLA_V)
    wk = jnp.pad(kvh[:, :, :MLA_NOPE], ((0, 0), (0, 0), (0, LANES - MLA_NOPE))).reshape(MLA_KV_RANK, MLA_QK)
    wv = kvh[:, :, MLA_NOPE:].reshape(MLA_KV_RANK, MLA_VW)
    return w, wq, jnp.concatenate([wk, wv], axis=1).astype(BF)


def kernel(x, c, ctx, c_ctx, w_mod, b_mod, norm_g, w_in_even, conv_w, sink, w_out_even, w_in_odd, mla_q_norm_g,
           mla_kv_norm_g, w_q_up, w_kv_up, diff_lambda, diff_subln_g, w_out_odd, w_ff_gu, w_ff_down, w_router,
           w_exp_gu, w_exp_down):
    bn, s, d = x.shape
    l = ctx.shape[1]
    depth = w_mod.shape[0]
    n_exp = w_router.shape[-1]
    dims = _Dims(bn, s, l, d)
    assert bn < MOD_ROWS and s % l == 0 and l % BLOCK == 0 and s % GRID_W == 0

    tm = _tile(math.gcd(s, bn * l), 512)
    tm_e = 1024 if dims.t_all >= 8192 else 256
    tq = _tile(s, 512)

    cc = jnp.zeros((MOD_ROWS, d), F32).at[:bn].set(c).at[bn].set(c_ctx)
    mods = _modulation(cc, w_mod, b_mod)
    rope64 = _rope_tables(s, tm, SWA_HEAD_DIM, SWA_HEAD_DIM, 0)
    rope32 = _rope_tables(s, tm, MLA_ROPE, LANES, MLA_NOPE)

    xa = None
    for layer in range(depth):
        i = layer // 2
        mods3 = mods[layer].reshape(MOD_ROWS, 1, N_MOD * d)
        gains3 = norm_g[layer].reshape(4, 1, d)
        if layer % 2 == 0:
            xs = (x.reshape(bn * s, d), ctx.reshape(bn * l, d)) if xa is None else (xa,)
            bcu, q, k, v, *merged = _even_proj(dims, xs, mods3, gains3, _even_weights(w_in_even[i]), rope64, tm)
            xa = merged[0] if merged else xa
            conv = _gated_conv(dims, bcu, conv_w[i])
            attn = _win_attn(dims, q, k, v, sink[i])
            wo = w_out_even[i].astype(BF)
            xa = _out_proj(dims, conv, attn, wo[:CONV_WIDTH], wo[CONV_WIDTH:], xa, mods3, gains3, tm)
            act_lo, act_hi = _ffn_up(dims, xa, mods3, gains3, w_ff_gu[i].astype(BF), tm)
            wd = w_ff_down[i].astype(BF)
            xa = _out_proj(dims, act_lo, act_hi, wd[:act_lo.shape[1]], wd[act_lo.shape[1]:], xa, mods3, gains3, tm,
                           gate_col=5, gain_row=3)
        else:
            w, wq, wkv = _odd_weights(w_in_odd[i], w_q_up[i], w_kv_up[i])
            qm, km, vm, dq, dk, dv = _odd_proj(dims, xa, mods3, gains3, w, wq, wkv,
                                               mla_q_norm_g[i].reshape(1, -1), mla_kv_norm_g[i].reshape(1, -1),
                                               rope32, rope64, tm)
            lam_init = 0.8 - 0.6 * math.exp(-0.3 * layer)
            extra = (diff_lambda[i], diff_subln_g[i].reshape(1, -1))
            need_ctx = layer < depth - 1
            rows = dims.t_all if need_ctx else dims.t_lat
            o_m = _full_attn(dims, "mla", qm, km, vm, need_ctx, tq)
            o_d = _full_attn(dims, "diff", dq, dk, dv, need_ctx, tq, extra, lam_init)
            wo = w_out_odd[i].astype(BF)
            xa = _out_proj(dims, o_m, o_d, wo[:MLA_VW], wo[MLA_VW:], xa, mods3, gains3, tm)
            wr = jnp.pad(w_router[i], ((0, 0), (0, LANES - n_exp)))
            f_in, route = _router(dims, xa, rows, mods3, gains3, wr, n_exp, tm)
            dest, src, block_e, n_used = _routing_tables(route, n_exp, tm_e)
            y = _experts(f_in, src, block_e, n_used, w_exp_gu, w_exp_down, i, tm_e,
                         _tile(w_exp_down.shape[2], 512))
            xa = _combine(dims, y, dest, route, xa, mods3, gains3, tm)
    return xa[:bn * s].reshape(bn, s, d)
```

```python
import functools
import math

import jax
import jax.numpy as jnp
from jax import lax
from jax.experimental import pallas as pl
from jax.experimental.pallas import tpu as pltpu

F32 = jnp.float32
BF = jnp.bfloat16

EPS = 1e-6
ROPE_THETA = 10000.0
GRID_W = 64
BLOCK = 128
N_MOD = 6
LANES = 128
MXU_N = 256
MOD_ROWS = 16

CONV_WIDTH = 512
SWA_HEADS = 8
SWA_KV_HEADS = 2
SWA_HEAD_DIM = 64
MLA_HEADS = 8
MLA_Q_RANK = 256
MLA_KV_RANK = 128
MLA_NOPE = 64
MLA_ROPE = 32
MLA_V = 64
DIFF_HEADS = 4
DIFF_DIM = 64
DIFF_V = 2 * DIFF_DIM
TOP_K = 2

NEG = -1e30
LOG2E = math.log2(math.e)
KEY_CHUNK = 1024


def _tile(n, pref):
    if n <= pref:
        return n
    t = pref - pref % LANES
    while t >= LANES:
        if n % t == 0:
            return t
        t -= LANES
    raise ValueError((n, pref))


def _cparams(sem, vmem_mb=None):
    kw = dict(dimension_semantics=sem)
    if vmem_mb is not None:
        kw["vmem_limit_bytes"] = vmem_mb << 20
    return pltpu.CompilerParams(**kw)


def _dot(a, b):
    return jnp.dot(a, b, preferred_element_type=F32)


def _dot_nt(a, b):
    return lax.dot_general(a, b, (((1,), (1,)), ((), ())), preferred_element_type=F32)


def _rms(x, g):
    return x * lax.rsqrt(jnp.mean(x * x, axis=-1, keepdims=True) + EPS) * g


def _norm_mod(x, g, shift, scale):
    return _rms(x, g) * (1.0 + scale) + shift


def _silu(x):
    return x / (1.0 + jnp.exp(-x))


def _rope(x, cos, sa, sb, half):
    out = []
    for c in range(x.shape[1] // LANES):
        xc = x[:, c * LANES:(c + 1) * LANES]
        out.append(xc * cos + pltpu.roll(xc, LANES - half, 1) * sa + pltpu.roll(xc, half, 1) * sb)
    return out[0] if len(out) == 1 else jnp.concatenate(out, axis=1)


def _mod_kernel(cc_ref, w_ref, b_ref, o_ref):
    a = _silu(cc_ref[...]).astype(BF)
    o_ref[0] = _dot(a, w_ref[0].astype(BF)) + b_ref[0]


def _modulation(cc, w_mod, b_mod):
    depth, d, n = w_mod.shape
    tn = _tile(n, 1536)
    return pl.pallas_call(
        _mod_kernel,
        grid=(depth, n // tn),
        in_specs=[
            pl.BlockSpec((MOD_ROWS, d), lambda l, j: (0, 0)),
            pl.BlockSpec((1, d, tn), lambda l, j: (l, 0, j)),
            pl.BlockSpec((1, 1, tn), lambda l, j: (l, 0, j)),
        ],
        out_specs=pl.BlockSpec((1, MOD_ROWS, tn), lambda l, j: (l, 0, j)),
        out_shape=jax.ShapeDtypeStruct((depth, MOD_ROWS, n), F32),
        compiler_params=_cparams(("arbitrary", "arbitrary"), 40),
    )(cc, w_mod, b_mod.reshape(depth, 1, n))


class _Dims:
    def __init__(self, bn, s, l, d):
        self.bn, self.s, self.l, self.d = bn, s, l, d
        self.t_lat = bn * s
        self.t_all = bn * s + bn * l

    def mod_row(self, i, tm):
        return jnp.minimum(i * tm // self.s, self.bn)


def _mod_spec(dims, tm, col):
    return pl.BlockSpec((1, 1, dims.d), lambda i, *_: (dims.mod_row(i, tm), 0, col))


def _gain_spec(dims, k):
    return pl.BlockSpec((1, 1, dims.d), lambda i, *_: (k, 0, 0))


def _rope_spec(dims, tm):
    n_lat, per_seq = dims.t_lat // tm, dims.s // tm
    return pl.BlockSpec((tm, LANES), lambda i: (jnp.where(i < n_lat, i % per_seq, per_seq), 0))


def _full_spec(shape):
    return pl.BlockSpec(shape, lambda *_: (0,) * len(shape))


def _rope_tables(s, pad_rows, rot_dim, period, lane_off):
    axis_dim = rot_dim // 2
    half = axis_dim // 2
    inv = 1.0 / (ROPE_THETA ** (jnp.arange(0, axis_dim, 2, dtype=F32) / axis_dim))
    pos = jnp.arange(s)
    rows = (pos // GRID_W).astype(F32)[:, None]
    cols = (pos % GRID_W).astype(F32)[:, None]
    lane = jnp.arange(LANES)
    dd = lane % period - lane_off
    active = (dd >= 0) & (dd < rot_dim)
    dd = jnp.clip(dd, 0, rot_dim - 1)
    j = dd % axis_dim
    ang = jnp.where((dd // axis_dim == 0)[None, :], rows, cols) * inv[j % half][None, :]
    first = (j < half)[None, :]
    act = active[None, :]
    cos = jnp.where(act, jnp.cos(ang), 1.0)
    sin = jnp.where(act, jnp.sin(ang), 0.0)
    sa = jnp.where(first, -sin, 0.0)
    sb = jnp.where(first, 0.0, sin)
    pad = lambda t, v: jnp.concatenate([t, jnp.full((pad_rows, LANES), v, F32)], axis=0)
    return pad(cos, 1.0), pad(sa, 0.0), pad(sb, 0.0), half


EVEN_BCU = 3 * CONV_WIDTH
EVEN_Q = SWA_HEADS * SWA_HEAD_DIM
EVEN_KD = SWA_KV_HEADS * LANES


def _even_proj_kernel(*refs, half, q_scale, n_lat_tiles):
    if n_lat_tiles is None:
        x_ref, g_ref, sh_ref, sc_ref, w_ref, cos_ref, sa_ref, sb_ref, bcu_ref, q_ref, k_ref, v_ref = refs
        x = x_ref[...]
    else:
        (xl_ref, xc_ref, g_ref, sh_ref, sc_ref, w_ref, cos_ref, sa_ref, sb_ref,
         bcu_ref, q_ref, k_ref, v_ref, xa_ref) = refs
        x = jnp.where(pl.program_id(0) < n_lat_tiles, xl_ref[...], xc_ref[...])
        xa_ref[...] = x
    h = _norm_mod(x, g_ref[0], sh_ref[0], sc_ref[0]).astype(BF)
    c0, c1, c2 = EVEN_BCU, EVEN_BCU + EVEN_Q, EVEN_BCU + EVEN_Q + EVEN_KD
    bcu_ref[...] = _dot(h, w_ref[:, :c0]).astype(BF)
    qk = _rope(_dot(h, w_ref[:, c0:c2]), cos_ref[...], sa_ref[...], sb_ref[...], half)
    q_ref[...] = (qk[:, :EVEN_Q] * q_scale).astype(BF)
    k_ref[...] = qk[:, EVEN_Q:].astype(BF)
    v_ref[...] = _dot(h, w_ref[:, c2:]).astype(BF)


def _even_proj(dims, xs, mods3, gains3, w, rope, tm):
    n = w.shape[1]
    cos, sa, sb, half = rope
    row = lambda width: pl.BlockSpec((tm, width), lambda i: (i, 0))
    out_w = (EVEN_BCU, EVEN_Q, EVEN_KD, EVEN_KD)
    out_specs = [row(wd) for wd in out_w]
    out_shape = [jax.ShapeDtypeStruct((dims.t_all, wd), BF) for wd in out_w]
    if len(xs) == 1:
        n_lat, x_specs = None, [row(dims.d)]
    else:
        n_lat = dims.t_lat // tm
        x_specs = [pl.BlockSpec((tm, dims.d), lambda i: (jnp.minimum(i, n_lat - 1), 0)),
                   pl.BlockSpec((tm, dims.d), lambda i: (jnp.maximum(i - n_lat, 0), 0))]
        out_specs.append(row(dims.d))
        out_shape.append(jax.ShapeDtypeStruct((dims.t_all, dims.d), F32))
    return pl.pallas_call(
        functools.partial(_even_proj_kernel, half=half, q_scale=SWA_HEAD_DIM ** -0.5 * LOG2E, n_lat_tiles=n_lat),
        grid=(dims.t_all // tm,),
        in_specs=x_specs + [_gain_spec(dims, 0), _mod_spec(dims, tm, 0), _mod_spec(dims, tm, 1),
                            _full_spec((dims.d, n)), _rope_spec(dims, tm), _rope_spec(dims, tm),
                            _rope_spec(dims, tm)],
        out_specs=out_specs,
        out_shape=out_shape,
        compiler_params=_cparams(("arbitrary",), 48),
    )(*xs, gains3, mods3, mods3, w, cos, sa, sb)


def _conv_kernel(b_ref, c_ref, u_ref, cp_ref, up_ref, cn_ref, un_ref, w_ref, o_ref, *, n_lat_blocks, per_seq):
    i = pl.program_id(0)
    rows = b_ref.shape[0]
    hr = cp_ref.shape[0]
    pos = i % per_seq
    is_lat = i < n_lat_blocks
    has_prev = jnp.logical_and(is_lat, pos > 0).astype(F32)
    has_next = jnp.logical_and(is_lat, pos < per_seq - 1).astype(F32)
    cu = c_ref[...].astype(F32) * u_ref[...].astype(F32)
    cu_p = (cp_ref[...].astype(F32) * up_ref[...].astype(F32))[hr - 1:hr] * has_prev
    cu_n = (cn_ref[...].astype(F32) * un_ref[...].astype(F32))[0:1] * has_next
    r = lax.broadcasted_iota(jnp.int32, cu.shape, 0)
    prev = jnp.where(r == 0, cu_p, pltpu.roll(cu, 1, 0))
    nxt = jnp.where(r == rows - 1, cu_n, pltpu.roll(cu, rows - 1, 0))
    w = w_ref[...]
    o_ref[...] = (b_ref[...].astype(F32) * (prev * w[0:1] + cu * w[1:2] + nxt * w[2:3])).astype(BF)


def _gated_conv(dims, bcu, conv_w):
    rows = dims.l
    hr = 16
    nblk = dims.t_all // rows
    per = rows // hr
    last = dims.t_all // hr - 1
    cw = CONV_WIDTH
    main = lambda col: pl.BlockSpec((rows, cw), lambda i: (i, col))
    prev = lambda col: pl.BlockSpec((hr, cw), lambda i: (jnp.maximum(i * per - 1, 0), col))
    nxt = lambda col: pl.BlockSpec((hr, cw), lambda i: (jnp.minimum((i + 1) * per, last), col))
    return pl.pallas_call(
        functools.partial(_conv_kernel, n_lat_blocks=dims.t_lat // rows, per_seq=dims.s // rows),
        grid=(nblk,),
        in_specs=[main(0), main(1), main(2), prev(1), prev(2), nxt(1), nxt(2), _full_spec(conv_w.shape)],
        out_specs=pl.BlockSpec((rows, cw), lambda i: (i, 0)),
        out_shape=jax.ShapeDtypeStruct((dims.t_all, cw), BF),
        compiler_params=_cparams(("arbitrary",)),
    )(bcu, bcu, bcu, bcu, bcu, bcu, bcu, conv_w)


def _win_attn_kernel(sink_ref, q_ref, kp_ref, ko_ref, kn_ref, kc_ref, vp_ref, vo_ref, vn_ref, vc_ref,
                     o_ref, *, nb):
    n = pl.program_id(1)
    blk = q_ref.shape[0]
    n_ctx = kc_ref.shape[0]
    is_lat = n < nb
    lo_s = jnp.where(is_lat, jnp.where(n >= 1, 0, blk), 3 * blk)
    hi_s = jnp.where(is_lat, jnp.where(n + 1 < nb, 3 * blk, 2 * blk), 0)
    shape = (blk, 3 * blk + n_ctx)
    c = lax.broadcasted_iota(jnp.int32, shape, 1)
    r = lax.broadcasted_iota(jnp.int32, shape, 0)
    ok = (c >= 3 * blk) | ((c >= r) & (c - 2 * blk <= r) & (c >= lo_s) & (c < hi_s))
    bias = jnp.where(ok, 0.0, NEG)
    kall = jnp.concatenate([kp_ref[...], ko_ref[...], kn_ref[...], kc_ref[...]], axis=0)
    vall = jnp.concatenate([vp_ref[...], vo_ref[...], vn_ref[...], vc_ref[...]], axis=0)
    lane = lax.broadcasted_iota(jnp.int32, (1, LANES), 1)
    keep = ((lane < LANES // 2).astype(F32).astype(BF), (lane >= LANES // 2).astype(F32).astype(BF))
    lo = lax.broadcasted_iota(jnp.int32, (blk, LANES), 1) < LANES // 2
    per_group = SWA_HEADS // SWA_KV_HEADS

    def score(head):
        pair, sub, g = head // 2, head % 2, head // per_group
        q2 = q_ref[:, pair * LANES:(pair + 1) * LANES]
        return _dot_nt(q2 * keep[sub], kall[:, g * LANES:(g + 1) * LANES]) + bias

    s_next = score(0)
    outs = []
    for head in range(SWA_HEADS):
        s = s_next
        if head + 1 < SWA_HEADS:
            s_next = score(head + 1)
        g = head // per_group
        sk = jnp.full((blk, 1), sink_ref[head], F32) * LOG2E
        m = jnp.maximum(jnp.max(s, axis=-1, keepdims=True), sk)
        p = jnp.exp2(s - m)
        den = jnp.sum(p, axis=-1, keepdims=True) + jnp.exp2(sk - m)
        outs.append(_dot(p.astype(BF), vall[:, g * LANES:(g + 1) * LANES]) / den)
        if head % 2 == 1:
            pair = head // 2
            o_ref[:, pair * LANES:(pair + 1) * LANES] = jnp.where(lo, outs[-2], outs[-1]).astype(BF)


def _win_attn(dims, q, k, v, sink):
    nb = dims.s // BLOCK
    nc = dims.l // BLOCK
    lat_blocks = dims.t_lat // BLOCK
    ctx_blk0 = dims.t_lat // dims.l

    def own(b, n):
        return jnp.where(n < nb, b * nb + n, lat_blocks + b * nc + (n - nb))

    def prev(b, n):
        return jnp.where(n < nb, b * nb + jnp.maximum(n - 1, 0), own(b, n))

    def nxt(b, n):
        return jnp.where(n < nb, b * nb + jnp.minimum(n + 1, nb - 1), own(b, n))

    kv = lambda f: pl.BlockSpec((BLOCK, EVEN_KD), lambda b, n: (f(b, n), 0))
    ctx = pl.BlockSpec((dims.l, EVEN_KD), lambda b, n: (ctx_blk0 + b, 0))
    qo = pl.BlockSpec((BLOCK, EVEN_Q), lambda b, n: (own(b, n), 0))
    return pl.pallas_call(
        functools.partial(_win_attn_kernel, nb=nb),
        grid=(dims.bn, nb + nc),
        in_specs=[pl.BlockSpec(memory_space=pltpu.SMEM), qo, kv(prev), kv(own), kv(nxt), ctx,
                  kv(prev), kv(own), kv(nxt), ctx],
        out_specs=qo,
        out_shape=jax.ShapeDtypeStruct((dims.t_all, EVEN_Q), BF),
        compiler_params=_cparams(("arbitrary", "arbitrary")),
    )(sink, q, k, k, k, k, v, v, v, v)


def _out_proj_kernel(a1_ref, a2_ref, w1_ref, w2_ref, x_ref, gate_ref, g_ref, o_ref):
    y = _dot(a1_ref[...], w1_ref[...]) + _dot(a2_ref[...], w2_ref[...])
    o_ref[...] = x_ref[...] + gate_ref[0] * _rms(y, g_ref[0])


def _out_proj(dims, a1, a2, w1, w2, x, mods3, gains3, tm, gate_col=2, gain_row=1):
    row = lambda width: pl.BlockSpec((tm, width), lambda i: (i, 0))
    return pl.pallas_call(
        _out_proj_kernel,
        grid=(a1.shape[0] // tm,),
        in_specs=[row(a1.shape[1]), row(a2.shape[1]), _full_spec(w1.shape), _full_spec(w2.shape),
                  row(dims.d), _mod_spec(dims, tm, gate_col), _gain_spec(dims, gain_row)],
        out_specs=row(dims.d),
        out_shape=jax.ShapeDtypeStruct(x.shape, F32),
        input_output_aliases={4: 0},
        compiler_params=_cparams(("arbitrary",), 40),
    )(a1, a2, w1, w2, x, mods3, gains3)


def _ffn_up_kernel(x_ref, g_ref, sh_ref, sc_ref, w_ref, lo_ref, hi_ref, *, ff):
    h = _norm_mod(x_ref[...], g_ref[0], sh_ref[0], sc_ref[0]).astype(BF)
    c0 = 0
    for ref in (lo_ref, hi_ref):
        c1 = c0 + ref.shape[1]
        ref[...] = (_silu(_dot(h, w_ref[:, c0:c1])) * _dot(h, w_ref[:, ff + c0:ff + c1])).astype(BF)
        c0 = c1


def _ffn_split(ff):
    lo = (ff // 2 + MXU_N - 1) // MXU_N * MXU_N
    return (lo, ff - lo) if 0 < lo < ff else (ff // 2, ff - ff // 2)


def _ffn_up(dims, x, mods3, gains3, w_gu, tm):
    ff = w_gu.shape[1] // 2
    row = lambda width: pl.BlockSpec((tm, width), lambda i: (i, 0))
    widths = _ffn_split(ff)
    return pl.pallas_call(
        functools.partial(_ffn_up_kernel, ff=ff),
        grid=(dims.t_all // tm,),
        in_specs=[row(dims.d), _gain_spec(dims, 2), _mod_spec(dims, tm, 3), _mod_spec(dims, tm, 4),
                  _full_spec(w_gu.shape)],
        out_specs=[row(wd) for wd in widths],
        out_shape=[jax.ShapeDtypeStruct((dims.t_all, wd), BF) for wd in widths],
        compiler_params=_cparams(("arbitrary",), 56),
    )(x, gains3, mods3, mods3, w_gu)


ODD_STAGE1 = MLA_Q_RANK + MLA_KV_RANK + LANES
MLA_QK = MLA_HEADS * LANES
MLA_VW = MLA_HEADS * MLA_V
DIFF_W = DIFF_HEADS * 2 * DIFF_DIM


def _odd_proj_kernel(x_ref, g_ref, sh_ref, sc_ref, w_ref, wq_ref, wkv_ref, qg_ref, kvg_ref,
                     c32_ref, a32_ref, b32_ref, c64_ref, a64_ref, b64_ref,
                     q_ref, k_ref, v_ref, dq_ref, dk_ref, dv_ref, *, half32, half64, mla_scale, diff_scale):
    h = _norm_mod(x_ref[...], g_ref[0], sh_ref[0], sc_ref[0]).astype(BF)
    r32 = (c32_ref[...], a32_ref[...], b32_ref[...], half32)
    r64 = (c64_ref[...], a64_ref[...], b64_ref[...], half64)
    s1 = _dot(h, w_ref[:, :ODD_STAGE1])
    qn = _rms(s1[:, :MLA_Q_RANK], qg_ref[...]).astype(BF)
    kvn = _rms(s1[:, MLA_Q_RANK:MLA_Q_RANK + MLA_KV_RANK], kvg_ref[...]).astype(BF)
    kpe = _rope(s1[:, MLA_Q_RANK + MLA_KV_RANK:], *r32)
    q_ref[...] = (_rope(_dot(qn, wq_ref[...]), *r32) * mla_scale).astype(BF)
    kn = _dot(kvn, wkv_ref[:, :MLA_QK])
    k_ref[...] = (kn + jnp.concatenate([kpe] * MLA_HEADS, axis=1)).astype(BF)
    v_ref[...] = _dot(kvn, wkv_ref[:, MLA_QK:]).astype(BF)
    c0 = ODD_STAGE1
    dq_ref[...] = (_rope(_dot(h, w_ref[:, c0:c0 + DIFF_W]), *r64) * diff_scale).astype(BF)
    dk_ref[...] = _rope(_dot(h, w_ref[:, c0 + DIFF_W:c0 + 2 * DIFF_W]), *r64).astype(BF)
    dv_ref[...] = _dot(h, w_ref[:, c0 + 2 * DIFF_W:]).astype(BF)


def _odd_proj(dims, x, mods3, gains3, w, wq, wkv, qg, kvg, rope32, rope64, tm):
    row = lambda width: pl.BlockSpec((tm, width), lambda i: (i, 0))
    out_w = (MLA_QK, MLA_QK, MLA_VW, DIFF_W, DIFF_W, DIFF_HEADS * DIFF_V)
    rs = _rope_spec(dims, tm)
    return pl.pallas_call(
        functools.partial(_odd_proj_kernel, half32=rope32[3], half64=rope64[3],
                          mla_scale=(MLA_NOPE + MLA_ROPE) ** -0.5 * LOG2E, diff_scale=DIFF_DIM ** -0.5 * LOG2E),
        grid=(dims.t_all // tm,),
        in_specs=[row(dims.d), _gain_spec(dims, 0), _mod_spec(dims, tm, 0), _mod_spec(dims, tm, 1),
                  _full_spec(w.shape), _full_spec(wq.shape), _full_spec(wkv.shape),
                  _full_spec(qg.shape), _full_spec(kvg.shape), rs, rs, rs, rs, rs, rs],
        out_specs=[row(wd) for wd in out_w],
        out_shape=[jax.ShapeDtypeStruct((dims.t_all, wd), BF) for wd in out_w],
        compiler_params=_cparams(("arbitrary",), 48),
    )(x, gains3, mods3, mods3, w, wq, wkv, qg, kvg, *rope32[:3], *rope64[:3])


def _softmax_pv(streams, k_refs, v_refs):
    tasks = []
    for si, (q, cols_k, cols_v) in enumerate(streams):
        for kr, vr in zip(k_refs, v_refs):
            n = kr.shape[0]
            for c0 in range(0, n, KEY_CHUNK):
                tasks.append((si, q, kr, vr, c0, min(c0 + KEY_CHUNK, n), cols_k, cols_v))
    score = lambda t: _dot_nt(t[1], t[2][t[4]:t[5], t[6]])
    state = [None] * len(streams)
    s_next = score(tasks[0])
    for ti, t in enumerate(tasks):
        s = s_next
        if ti + 1 < len(tasks):
            s_next = score(tasks[ti + 1])
        si, _, _, vr, c0, c1, _, cols_v = t
        mc = jnp.max(s, axis=-1, keepdims=True)
        if state[si] is None:
            m_new = mc
        else:
            m, l, acc = state[si]
            m_new = jnp.maximum(m, mc)
        p = jnp.exp2(s - m_new)
        ps = jnp.sum(p, axis=-1, keepdims=True)
        pv = _dot(p.astype(BF), vr[c0:c1, cols_v])
        if state[si] is None:
            state[si] = (m_new, ps, pv)
        else:
            a = jnp.exp2(m - m_new)
            state[si] = (m_new, a * l + ps, a * acc + pv)
    return [acc / l for _, l, acc in state]


def _mla_attn_kernel(q_ref, *refs, n_seg):
    k_refs, v_refs, o_ref = refs[:n_seg], refs[n_seg:2 * n_seg], refs[2 * n_seg]
    cols = [slice(sub * LANES, (sub + 1) * LANES) for sub in range(2)]
    outs = _softmax_pv([(q_ref[:, c], c, slice(None)) for c in cols], k_refs, v_refs)
    lo = lax.broadcasted_iota(jnp.int32, outs[0].shape, 1) < MLA_V
    o_ref[...] = jnp.where(lo, outs[0], outs[1]).astype(BF)


def _diff_attn_kernel(lam_ref, sg_ref, q_ref, *refs, n_seg, lam_init):
    k_refs, v_refs, o_ref = refs[:n_seg], refs[n_seg:2 * n_seg], refs[2 * n_seg]
    lp = lam_ref[...]
    lam = (jnp.exp(jnp.sum(lp[0:1] * lp[1:2], axis=-1, keepdims=True))
           - jnp.exp(jnp.sum(lp[2:3] * lp[3:4], axis=-1, keepdims=True)) + lam_init)
    lane = lax.broadcasted_iota(jnp.int32, (1, LANES), 1)
    q = q_ref[...]
    full = slice(None)
    o1, o2 = _softmax_pv([(q * (lane < DIFF_DIM).astype(F32).astype(BF), full, full),
                          (q * (lane >= DIFF_DIM).astype(F32).astype(BF), full, full)], k_refs, v_refs)
    o_ref[...] = (_rms(o1 - lam * o2, sg_ref[...]) * (1.0 - lam_init)).astype(BF)


def _full_attn_call(dims, kind, q, k, v, ctx_queries, tq, extra, lam_init):
    if kind == "mla":
        n_h, qw, vw = MLA_HEADS // 2, 2 * LANES, LANES
    else:
        n_h, qw, vw = DIFF_HEADS, LANES, LANES
    ctx_blk0 = dims.t_lat // dims.l
    ctx_seg = (dims.l, lambda b: ctx_blk0 + b)
    if ctx_queries:
        tq, nq, out_rows = dims.l, 1, dims.bn * dims.l
        q_row = lambda b, qi: ctx_blk0 + b
        o_row = lambda b, qi: b
        seg = [ctx_seg]
    else:
        nq, out_rows = dims.s // tq, dims.t_lat
        q_row = o_row = lambda b, qi: b * nq + qi
        seg = [(dims.s, lambda b: b), ctx_seg]
    kv_specs = lambda w: [pl.BlockSpec((rows, w), (lambda f: lambda b, hh, qi: (f(b), hh))(f)) for rows, f in seg]
    if kind == "mla":
        body = functools.partial(_mla_attn_kernel, n_seg=len(seg))
    else:
        body = functools.partial(_diff_attn_kernel, n_seg=len(seg), lam_init=lam_init)
    return pl.pallas_call(
        body,
        grid=(dims.bn, n_h, nq),
        in_specs=([_full_spec(e.shape) for e in extra]
                  + [pl.BlockSpec((tq, qw), lambda b, hh, qi: (q_row(b, qi), hh))] + kv_specs(qw) + kv_specs(vw)),
        out_specs=pl.BlockSpec((tq, vw), lambda b, hh, qi: (o_row(b, qi), hh)),
        out_shape=jax.ShapeDtypeStruct((out_rows, n_h * vw), BF),
        compiler_params=_cparams(("arbitrary", "arbitrary", "arbitrary"), 56),
    )(*extra, q, *([k] * len(seg)), *([v] * len(seg)))


def _full_attn(dims, kind, q, k, v, need_ctx, tq, extra=(), lam_init=0.0):
    lat = _full_attn_call(dims, kind, q, k, v, False, tq, extra, lam_init)
    if not need_ctx:
        return lat
    return jnp.concatenate([lat, _full_attn_call(dims, kind, q, k, v, True, tq, extra, lam_init)], axis=0)


def _router_kernel(x_ref, g_ref, sh_ref, sc_ref, wr_ref, f_ref, r_ref, *, n_exp):
    f = _norm_mod(x_ref[...], g_ref[0], sh_ref[0], sc_ref[0])
    f_ref[...] = f
    logits = jnp.dot(f, wr_ref[...], preferred_element_type=F32, precision=lax.Precision.HIGHEST)
    lane = lax.broadcasted_iota(jnp.int32, logits.shape, 1).astype(F32)
    logits = jnp.where(lane < n_exp, logits, NEG)
    m1 = jnp.max(logits, axis=-1, keepdims=True)
    i1 = jnp.min(jnp.where(logits == m1, lane, float(LANES)), axis=-1, keepdims=True)
    rest = jnp.where(lane == i1, NEG, logits)
    m2 = jnp.max(rest, axis=-1, keepdims=True)
    i2 = jnp.min(jnp.where(rest == m2, lane, float(LANES)), axis=-1, keepdims=True)
    e2 = jnp.exp(m2 - m1)
    w1 = 1.0 / (1.0 + e2)
    w2 = e2 / (1.0 + e2)
    r_ref[...] = jnp.where(lane == 0, i1, jnp.where(lane == 1, i2, jnp.where(lane == 2, w1,
                           jnp.where(lane == 3, w2, 0.0))))


def _router(dims, x, rows, mods3, gains3, w_router_pad, n_exp, tm):
    row = lambda width: pl.BlockSpec((tm, width), lambda i: (i, 0))
    return pl.pallas_call(
        functools.partial(_router_kernel, n_exp=n_exp),
        grid=(rows // tm,),
        in_specs=[row(dims.d), _gain_spec(dims, 2), _mod_spec(dims, tm, 3), _mod_spec(dims, tm, 4),
                  _full_spec(w_router_pad.shape)],
        out_specs=[row(dims.d), row(LANES)],
        out_shape=[jax.ShapeDtypeStruct((rows, dims.d), F32),
                   jax.ShapeDtypeStruct((rows, LANES), F32)],
        compiler_params=_cparams(("arbitrary",), 40),
    )(x, gains3, mods3, mods3, w_router_pad)


def _row_copy(src_hbm, dst, s, d, sem):
    return pltpu.make_async_copy(src_hbm.at[pl.ds(s, 1)], dst.at[pl.ds(d, 1)], sem)


def _expert_kernel(be_ref, nu_ref, src_cur, src_nxt, x_hbm, wg_ref, wu_ref, wd_ref, o_ref,
                   xbuf, sem, h_s, acc_s, *, tm, nj):
    blk, j = pl.program_id(0), pl.program_id(1)
    nblk = pl.num_programs(0)
    n_used = nu_ref[0]
    used = blk < n_used
    slot = blk % 2
    per_step = tm // nj
    head = tm - per_step * nj

    def issue(src_ref, dst_slot, r):
        _row_copy(x_hbm, xbuf.at[dst_slot], src_ref[0, 0, r], r, sem.at[dst_slot]).start()

    def wait_slot(s):
        pltpu.make_async_copy(x_hbm.at[pl.ds(0, tm)], xbuf.at[s], sem.at[s]).wait()

    @pl.when(j == 0)
    def _():
        @pl.when(blk == 0)
        def _():
            def first(r, c):
                issue(src_cur, 0, r)
                return c

            lax.fori_loop(0, tm, first, 0, unroll=8)

        @pl.when(blk <= n_used)
        def _():
            wait_slot(slot)

        @pl.when(used)
        def _():
            h_s[...] = xbuf[slot].astype(BF)
            for r in range(head):
                issue(src_nxt, 1 - slot, r)

        acc_s[...] = jnp.zeros_like(acc_s)

    for step in range(nj):
        @pl.when(jnp.logical_and(used, j == step))
        def _():
            for r in range(per_step):
                issue(src_nxt, 1 - slot, head + step * per_step + r)

    @pl.when(used)
    def _():
        h = h_s[...]
        act = (_silu(_dot(h, wg_ref[0, 0].astype(BF))) * _dot(h, wu_ref[0, 0].astype(BF))).astype(BF)
        acc_s[...] += _dot(act, wd_ref[0, 0].astype(BF))

    @pl.when(j == nj - 1)
    def _():
        o_ref[...] = acc_s[...]

        @pl.when(jnp.logical_and(blk == nblk - 1, used))
        def _():
            wait_slot(1 - slot)


def _experts(f_in, src, block_e, n_used, w_gu, w_down, layer_i, tm, tn):
    d = f_in.shape[1]
    ff = w_down.shape[2]
    nj = ff // tn
    nblk = src.shape[0] // tm
    jj = lambda blk, j, nu: jnp.where(blk < nu[0], j, nj - 1)
    src3 = src.reshape(nblk, 1, tm)
    return pl.pallas_call(
        functools.partial(_expert_kernel, tm=tm, nj=nj),
        grid_spec=pltpu.PrefetchScalarGridSpec(
            num_scalar_prefetch=2,
            grid=(nblk, nj),
            in_specs=[pl.BlockSpec((1, 1, tm), lambda blk, j, be, nu: (blk, 0, 0), memory_space=pltpu.SMEM),
                      pl.BlockSpec((1, 1, tm), lambda blk, j, be, nu: (jnp.minimum(blk + 1, nblk - 1), 0, 0),
                                   memory_space=pltpu.SMEM),
                      pl.BlockSpec(memory_space=pl.ANY),
                      pl.BlockSpec((1, 1, d, tn), lambda blk, j, be, nu: (layer_i, be[blk], 0, jj(blk, j, nu))),
                      pl.BlockSpec((1, 1, d, tn), lambda blk, j, be, nu: (layer_i, be[blk], 0, nj + jj(blk, j, nu))),
                      pl.BlockSpec((1, 1, tn, d), lambda blk, j, be, nu: (layer_i, be[blk], jj(blk, j, nu), 0))],
            out_specs=pl.BlockSpec((tm, d), lambda blk, j, be, nu: (blk, 0)),
            scratch_shapes=[pltpu.VMEM((2, tm, d), F32), pltpu.SemaphoreType.DMA((2,)),
                            pltpu.VMEM((tm, d), BF), pltpu.VMEM((tm, d), F32)]),
        out_shape=jax.ShapeDtypeStruct((nblk * tm, d), F32),
        compiler_params=_cparams(("arbitrary", "arbitrary"), 56),
    )(block_e, n_used, src3, src3, f_in, w_gu, w_gu, w_down)


def _combine_kernel(dest_ref, r_ref, x_ref, gate_ref, g_ref, y_hbm, o_ref, buf, sem, *, tm):
    for t in range(tm):
        for k in range(TOP_K):
            _row_copy(y_hbm, buf.at[k], dest_ref[0, 0, TOP_K * t + k], t, sem.at[k]).start()
    for k in range(TOP_K):
        pltpu.make_async_copy(y_hbm.at[pl.ds(0, tm)], buf.at[k], sem.at[k]).wait()
    r = r_ref[...]
    y = buf[0] * r[:, 2:3] + buf[1] * r[:, 3:4]
    o_ref[...] = x_ref[...] + gate_ref[0] * _rms(y, g_ref[0])


def _combine(dims, y, dest, route, x, mods3, gains3, tm):
    rows = route.shape[0]
    nt = rows // tm
    row = lambda width: pl.BlockSpec((tm, width), lambda i: (i, 0))
    return pl.pallas_call(
        functools.partial(_combine_kernel, tm=tm),
        grid=(nt,),
        in_specs=[pl.BlockSpec((1, 1, TOP_K * tm), lambda i: (i, 0, 0), memory_space=pltpu.SMEM),
                  row(LANES), row(dims.d), _mod_spec(dims, tm, 5), _gain_spec(dims, 3),
                  pl.BlockSpec(memory_space=pl.ANY)],
        out_specs=row(dims.d),
        out_shape=jax.ShapeDtypeStruct((rows, dims.d), F32),
        scratch_shapes=[pltpu.VMEM((TOP_K, tm, dims.d), F32), pltpu.SemaphoreType.DMA((TOP_K,))],
        input_output_aliases={2: 0} if rows == x.shape[0] else {},
        compiler_params=_cparams(("arbitrary",)),
    )(dest.reshape(nt, 1, TOP_K * tm), route, x, mods3, gains3, y)


def _routing_tables(route, n_exp, tm_e):
    t = route.shape[0]
    a = t * TOP_K
    flat_e = route[:, :TOP_K].astype(jnp.int32).reshape(a)
    onehot = (flat_e[:, None] == jnp.arange(n_exp)[None, :]).astype(jnp.int32)
    csum = jnp.cumsum(onehot, axis=0)
    rank = jnp.take_along_axis(csum, flat_e[:, None], axis=1)[:, 0] - 1
    counts = csum[-1]
    padded = (counts + tm_e - 1) // tm_e * tm_e
    ends = jnp.cumsum(padded)
    dest = (ends - padded)[flat_e] + rank
    n_blocks = -(-(a + n_exp * (tm_e - 1)) // tm_e)
    src = jnp.zeros((n_blocks * tm_e,), jnp.int32).at[dest].set(jnp.arange(a, dtype=jnp.int32) // TOP_K)
    block_start = jnp.arange(n_blocks, dtype=jnp.int32) * tm_e
    block_e = jnp.minimum(jnp.sum((ends[None, :] <= block_start[:, None]).astype(jnp.int32), axis=1), n_exp - 1)
    n_used = (ends[-1] // tm_e).reshape(1)
    return dest.astype(jnp.int32), src, block_e.astype(jnp.int32), n_used.astype(jnp.int32)


def _even_weights(w_in):
    kw = SWA_KV_HEADS * SWA_HEAD_DIM
    c0 = EVEN_BCU + EVEN_Q
    wk, wv = w_in[:, c0:c0 + kw], w_in[:, c0 + kw:c0 + 2 * kw]
    dup = lambda w: jnp.concatenate(
        [w[:, g * SWA_HEAD_DIM:(g + 1) * SWA_HEAD_DIM] for g in range(SWA_KV_HEADS) for _ in range(2)], axis=1)
    return jnp.concatenate([w_in[:, :c0], dup(wk), dup(wv)], axis=1).astype(BF)


def _odd_weights(w_in, w_q_up, w_kv_up):
    d = w_in.shape[0]
    c = 0
    mq = w_in[:, c:c + MLA_Q_RANK]; c += MLA_Q_RANK
    dq = w_in[:, c:c + DIFF_W]; c += DIFF_W
    kvd = w_in[:, c:c + MLA_KV_RANK]; c += MLA_KV_RANK
    kpe = w_in[:, c:c + MLA_ROPE]; c += MLA_ROPE
    dk = w_in[:, c:c + DIFF_W]; c += DIFF_W
    dv = w_in[:, c:]
    pad_tail = LANES - MLA_NOPE - MLA_ROPE
    kpe_chunk = jnp.concatenate([jnp.zeros((d, MLA_NOPE), F32), kpe, jnp.zeros((d, pad_tail), F32)], axis=1)
    w = jnp.concatenate([mq, kvd, kpe_chunk, dq, dk, dv], axis=1).astype(BF)
    qh = w_q_up.reshape(MLA_Q_RANK, MLA_HEADS, MLA_NOPE + MLA_ROPE)
    wq = jnp.pad(qh, ((0, 0), (0, 0), (0, pad_tail))).reshape(MLA_Q_RANK, MLA_QK).astype(BF)
    kvh = w_kv_up.reshape(MLA_KV_RANK, MLA_HEADS, MLA_NOPE + MLA_V)
    wk = jnp.pad(kvh[:, :, :MLA_NOPE], ((0, 0), (0, 0), (0, LANES - MLA_NOPE))).reshape(MLA_KV_RANK, MLA_QK)
    wv = kvh[:, :, MLA_NOPE:].reshape(MLA_KV_RANK, MLA_VW)
    return w, wq, jnp.concatenate([wk, wv], axis=1).astype(BF)


def kernel(x, c, ctx, c_ctx, w_mod, b_mod, norm_g, w_in_even, conv_w, sink, w_out_even, w_in_odd, mla_q_norm_g,
           mla_kv_norm_g, w_q_up, w_kv_up, diff_lambda, diff_subln_g, w_out_odd, w_ff_gu, w_ff_down, w_router,
           w_exp_gu, w_exp_down):
    bn, s, d = x.shape
    l = ctx.shape[1]
    depth = w_mod.shape[0]
    n_exp = w_router.shape[-1]
    dims = _Dims(bn, s, l, d)
    assert bn < MOD_ROWS and s % l == 0 and l % BLOCK == 0 and s % GRID_W == 0

    tm = _tile(math.gcd(s, bn * l), 512)
    tm_e = 1024 if dims.t_all >= 8192 else 256
    tq = _tile(s, 512)

    cc = jnp.zeros((MOD_ROWS, d), F32).at[:bn].set(c).at[bn].set(c_ctx)
    mods = _modulation(cc, w_mod, b_mod)
    rope64 = _rope_tables(s, tm, SWA_HEAD_DIM, SWA_HEAD_DIM, 0)
    rope32 = _rope_tables(s, tm, MLA_ROPE, LANES, MLA_NOPE)

    xa = None
    for layer in range(depth):
        i = layer // 2
        mods3 = mods[layer].reshape(MOD_ROWS, 1, N_MOD * d)
        gains3 = norm_g[layer].reshape(4, 1, d)
        if layer % 2 == 0:
            xs = (x.reshape(bn * s, d), ctx.reshape(bn * l, d)) if xa is None else (xa,)
            bcu, q, k, v, *merged = _even_proj(dims, xs, mods3, gains3, _even_weights(w_in_even[i]), rope64, tm)
            xa = merged[0] if merged else xa
            conv = _gated_conv(dims, bcu, conv_w[i])
            attn = _win_attn(dims, q, k, v, sink[i])
            wo = w_out_even[i].astype(BF)
            xa = _out_proj(dims, conv, attn, wo[:CONV_WIDTH], wo[CONV_WIDTH:], xa, mods3, gains3, tm)
            act_lo, act_hi = _ffn_up(dims, xa, mods3, gains3, w_ff_gu[i].astype(BF), tm)
            wd = w_ff_down[i].astype(BF)
            xa = _out_proj(dims, act_lo, act_hi, wd[:act_lo.shape[1]], wd[act_lo.shape[1]:], xa, mods3, gains3, tm,
                           gate_col=5, gain_row=3)
        else:
            w, wq, wkv = _odd_weights(w_in_odd[i], w_q_up[i], w_kv_up[i])
            qm, km, vm, dq, dk, dv = _odd_proj(dims, xa, mods3, gains3, w, wq, wkv,
                                               mla_q_norm_g[i].reshape(1, -1), mla_kv_norm_g[i].reshape(1, -1),
                                               rope32, rope64, tm)
            lam_init = 0.8 - 0.6 * math.exp(-0.3 * layer)
            extra = (diff_lambda[i], diff_subln_g[i].reshape(1, -1))
            need_ctx = layer < depth - 1
            rows = dims.t_all if need_ctx else dims.t_lat
            o_m = _full_attn(dims, "mla", qm, km, vm, need_ctx, tq)
            o_d = _full_attn(dims, "diff", dq, dk, dv, need_ctx, tq, extra, lam_init)
            wo = w_out_odd[i].astype(BF)
            xa = _out_proj(dims, o_m, o_d, wo[:MLA_VW], wo[MLA_VW:], xa, mods3, gains3, tm)
            wr = jnp.pad(w_router[i], ((0, 0), (0, LANES - n_exp)))
            f_in, route = _router(dims, xa, rows, mods3, gains3, wr, n_exp, tm)
            dest, src, block_e, n_used = _routing_tables(route, n_exp, tm_e)
            y = _experts(f_in, src, block_e, n_used, w_exp_gu, w_exp_down, i, tm_e,
                         _tile(w_exp_down.shape[2], 512))
            xa = _combine(dims, y, dest, route, xa, mods3, gains3, tm)
    return xa[:bn * s].reshape(bn, s, d)
```

```python
import functools
import math

import jax
import jax.numpy as jnp
from jax import lax
from jax.experimental import pallas as pl
from jax.experimental.pallas import tpu as pltpu

F32 = jnp.float32
BF = jnp.bfloat16

EPS = 1e-6
ROPE_THETA = 10000.0
GRID_W = 64
BLOCK = 128
N_MOD = 6
LANES = 128
MXU_N = 256
MOD_ROWS = 16

CONV_WIDTH = 512
SWA_HEADS = 8
SWA_KV_HEADS = 2
SWA_HEAD_DIM = 64
MLA_HEADS = 8
MLA_Q_RANK = 256
MLA_KV_RANK = 128
MLA_NOPE = 64
MLA_ROPE = 32
MLA_V = 64
DIFF_HEADS = 4
DIFF_DIM = 64
DIFF_V = 2 * DIFF_DIM
TOP_K = 2

NEG = -1e30
LOG2E = math.log2(math.e)
KEY_CHUNK = 1024


def _tile(n, pref):
    if n <= pref:
        return n
    t = pref - pref % LANES
    while t >= LANES:
        if n % t == 0:
            return t
        t -= LANES
    raise ValueError((n, pref))


def _cparams(sem, vmem_mb=None):
    kw = dict(dimension_semantics=sem)
    if vmem_mb is not None:
        kw["vmem_limit_bytes"] = vmem_mb << 20
    return pltpu.CompilerParams(**kw)


def _dot(a, b):
    return jnp.dot(a, b, preferred_element_type=F32)


def _dot_nt(a, b):
    return lax.dot_general(a, b, (((1,), (1,)), ((), ())), preferred_element_type=F32)


def _rms(x, g):
    return x * lax.rsqrt(jnp.mean(x * x, axis=-1, keepdims=True) + EPS) * g


def _norm_mod(x, g, shift, scale):
    return _rms(x, g) * (1.0 + scale) + shift


def _silu(x):
    return x / (1.0 + jnp.exp(-x))


def _rope(x, cos, sa, sb, half):
    out = []
    for c in range(x.shape[1] // LANES):
        xc = x[:, c * LANES:(c + 1) * LANES]
        out.append(xc * cos + pltpu.roll(xc, LANES - half, 1) * sa + pltpu.roll(xc, half, 1) * sb)
    return out[0] if len(out) == 1 else jnp.concatenate(out, axis=1)


def _mod_kernel(cc_ref, w_ref, b_ref, o_ref):
    a = _silu(cc_ref[...]).astype(BF)
    o_ref[0] = _dot(a, w_ref[0].astype(BF)) + b_ref[0]


def _modulation(cc, w_mod, b_mod):
    depth, d, n = w_mod.shape
    tn = _tile(n, 1536)
    return pl.pallas_call(
        _mod_kernel,
        grid=(depth, n // tn),
        in_specs=[
            pl.BlockSpec((MOD_ROWS, d), lambda l, j: (0, 0)),
            pl.BlockSpec((1, d, tn), lambda l, j: (l, 0, j)),
            pl.BlockSpec((1, 1, tn), lambda l, j: (l, 0, j)),
        ],
        out_specs=pl.BlockSpec((1, MOD_ROWS, tn), lambda l, j: (l, 0, j)),
        out_shape=jax.ShapeDtypeStruct((depth, MOD_ROWS, n), F32),
        compiler_params=_cparams(("arbitrary", "arbitrary"), 40),
    )(cc, w_mod, b_mod.reshape(depth, 1, n))


class _Dims:
    def __init__(self, bn, s, l, d):
        self.bn, self.s, self.l, self.d = bn, s, l, d
        self.t_lat = bn * s
        self.t_all = bn * s + bn * l

    def mod_row(self, i, tm):
        return jnp.minimum(i * tm // self.s, self.bn)


def _mod_spec(dims, tm, col):
    return pl.BlockSpec((1, 1, dims.d), lambda i, *_: (dims.mod_row(i, tm), 0, col))


def _gain_spec(dims, k):
    return pl.BlockSpec((1, 1, dims.d), lambda i, *_: (k, 0, 0))


def _rope_spec(dims, tm):
    n_lat, per_seq = dims.t_lat // tm, dims.s // tm
    return pl.BlockSpec((tm, LANES), lambda i: (jnp.where(i < n_lat, i % per_seq, per_seq), 0))


def _full_spec(shape):
    return pl.BlockSpec(shape, lambda *_: (0,) * len(shape))


def _rope_tables(s, pad_rows, rot_dim, period, lane_off):
    axis_dim = rot_dim // 2
    half = axis_dim // 2
    inv = 1.0 / (ROPE_THETA ** (jnp.arange(0, axis_dim, 2, dtype=F32) / axis_dim))
    pos = jnp.arange(s)
    rows = (pos // GRID_W).astype(F32)[:, None]
    cols = (pos % GRID_W).astype(F32)[:, None]
    lane = jnp.arange(LANES)
    dd = lane % period - lane_off
    active = (dd >= 0) & (dd < rot_dim)
    dd = jnp.clip(dd, 0, rot_dim - 1)
    j = dd % axis_dim
    ang = jnp.where((dd // axis_dim == 0)[None, :], rows, cols) * inv[j % half][None, :]
    first = (j < half)[None, :]
    act = active[None, :]
    cos = jnp.where(act, jnp.cos(ang), 1.0)
    sin = jnp.where(act, jnp.sin(ang), 0.0)
    sa = jnp.where(first, -sin, 0.0)
    sb = jnp.where(first, 0.0, sin)
    pad = lambda t, v: jnp.concatenate([t, jnp.full((pad_rows, LANES), v, F32)], axis=0)
    return pad(cos, 1.0), pad(sa, 0.0), pad(sb, 0.0), half


EVEN_BCU = 3 * CONV_WIDTH
EVEN_Q = SWA_HEADS * SWA_HEAD_DIM
EVEN_KD = SWA_KV_HEADS * LANES


def _even_proj_kernel(*refs, half, q_scale, n_lat_tiles):
    if n_lat_tiles is None:
        x_ref, g_ref, sh_ref, sc_ref, w_ref, cos_ref, sa_ref, sb_ref, bcu_ref, q_ref, k_ref, v_ref = refs
        x = x_ref[...]
    else:
        (xl_ref, xc_ref, g_ref, sh_ref, sc_ref, w_ref, cos_ref, sa_ref, sb_ref,
         bcu_ref, q_ref, k_ref, v_ref, xa_ref) = refs
        x = jnp.where(pl.program_id(0) < n_lat_tiles, xl_ref[...], xc_ref[...])
        xa_ref[...] = x
    h = _norm_mod(x, g_ref[0], sh_ref[0], sc_ref[0]).astype(BF)
    c0, c1, c2 = EVEN_BCU, EVEN_BCU + EVEN_Q, EVEN_BCU + EVEN_Q + EVEN_KD
    bcu_ref[...] = _dot(h, w_ref[:, :c0]).astype(BF)
    qk = _rope(_dot(h, w_ref[:, c0:c2]), cos_ref[...], sa_ref[...], sb_ref[...], half)
    q_ref[...] = (qk[:, :EVEN_Q] * q_scale).astype(BF)
    k_ref[...] = qk[:, EVEN_Q:].astype(BF)
    v_ref[...] = _dot(h, w_ref[:, c2:]).astype(BF)


def _even_proj(dims, xs, mods3, gains3, w, rope, tm):
    n = w.shape[1]
    cos, sa, sb, half = rope
    row = lambda width: pl.BlockSpec((tm, width), lambda i: (i, 0))
    out_w = (EVEN_BCU, EVEN_Q, EVEN_KD, EVEN_KD)
    out_specs = [row(wd) for wd in out_w]
    out_shape = [jax.ShapeDtypeStruct((dims.t_all, wd), BF) for wd in out_w]
    if len(xs) == 1:
        n_lat, x_specs = None, [row(dims.d)]
    else:
        n_lat = dims.t_lat // tm
        x_specs = [pl.BlockSpec((tm, dims.d), lambda i: (jnp.minimum(i, n_lat - 1), 0)),
                   pl.BlockSpec((tm, dims.d), lambda i: (jnp.maximum(i - n_lat, 0), 0))]
        out_specs.append(row(dims.d))
        out_shape.append(jax.ShapeDtypeStruct((dims.t_all, dims.d), F32))
    return pl.pallas_call(
        functools.partial(_even_proj_kernel, half=half, q_scale=SWA_HEAD_DIM ** -0.5 * LOG2E, n_lat_tiles=n_lat),
        grid=(dims.t_all // tm,),
        in_specs=x_specs + [_gain_spec(dims, 0), _mod_spec(dims, tm, 0), _mod_spec(dims, tm, 1),
                            _full_spec((dims.d, n)), _rope_spec(dims, tm), _rope_spec(dims, tm),
                            _rope_spec(dims, tm)],
        out_specs=out_specs,
        out_shape=out_shape,
        compiler_params=_cparams(("arbitrary",), 48),
    )(*xs, gains3, mods3, mods3, w, cos, sa, sb)


def _conv_kernel(b_ref, c_ref, u_ref, cp_ref, up_ref, cn_ref, un_ref, w_ref, o_ref, *, n_lat_blocks, per_seq):
    i = pl.program_id(0)
    rows = b_ref.shape[0]
    hr = cp_ref.shape[0]
    pos = i % per_seq
    is_lat = i < n_lat_blocks
    has_prev = jnp.logical_and(is_lat, pos > 0).astype(F32)
    has_next = jnp.logical_and(is_lat, pos < per_seq - 1).astype(F32)
    cu = c_ref[...].astype(F32) * u_ref[...].astype(F32)
    cu_p = (cp_ref[...].astype(F32) * up_ref[...].astype(F32))[hr - 1:hr] * has_prev
    cu_n = (cn_ref[...].astype(F32) * un_ref[...].astype(F32))[0:1] * has_next
    r = lax.broadcasted_iota(jnp.int32, cu.shape, 0)
    prev = jnp.where(r == 0, cu_p, pltpu.roll(cu, 1, 0))
    nxt = jnp.where(r == rows - 1, cu_n, pltpu.roll(cu, rows - 1, 0))
    w = w_ref[...]
    o_ref[...] = (b_ref[...].astype(F32) * (prev * w[0:1] + cu * w[1:2] + nxt * w[2:3])).astype(BF)


def _gated_conv(dims, bcu, conv_w):
    rows = dims.l
    hr = 16
    nblk = dims.t_all // rows
    per = rows // hr
    last = dims.t_all // hr - 1
    cw = CONV_WIDTH
    main = lambda col: pl.BlockSpec((rows, cw), lambda i: (i, col))
    prev = lambda col: pl.BlockSpec((hr, cw), lambda i: (jnp.maximum(i * per - 1, 0), col))
    nxt = lambda col: pl.BlockSpec((hr, cw), lambda i: (jnp.minimum((i + 1) * per, last), col))
    return pl.pallas_call(
        functools.partial(_conv_kernel, n_lat_blocks=dims.t_lat // rows, per_seq=dims.s // rows),
        grid=(nblk,),
        in_specs=[main(0), main(1), main(2), prev(1), prev(2), nxt(1), nxt(2), _full_spec(conv_w.shape)],
        out_specs=pl.BlockSpec((rows, cw), lambda i: (i, 0)),
        out_shape=jax.ShapeDtypeStruct((dims.t_all, cw), BF),
        compiler_params=_cparams(("arbitrary",)),
    )(bcu, bcu, bcu, bcu, bcu, bcu, bcu, conv_w)


def _win_attn_kernel(sink_ref, q_ref, kp_ref, ko_ref, kn_ref, kc_ref, vp_ref, vo_ref, vn_ref, vc_ref,
                     o_ref, *, nb):
    n = pl.program_id(1)
    blk = q_ref.shape[0]
    n_ctx = kc_ref.shape[0]
    is_lat = n < nb
    lo_s = jnp.where(is_lat, jnp.where(n >= 1, 0, blk), 3 * blk)
    hi_s = jnp.where(is_lat, jnp.where(n + 1 < nb, 3 * blk, 2 * blk), 0)
    shape = (blk, 3 * blk + n_ctx)
    c = lax.broadcasted_iota(jnp.int32, shape, 1)
    r = lax.broadcasted_iota(jnp.int32, shape, 0)
    ok = (c >= 3 * blk) | ((c >= r) & (c - 2 * blk <= r) & (c >= lo_s) & (c < hi_s))
    bias = jnp.where(ok, 0.0, NEG)
    kall = jnp.concatenate([kp_ref[...], ko_ref[...], kn_ref[...], kc_ref[...]], axis=0)
    vall = jnp.concatenate([vp_ref[...], vo_ref[...], vn_ref[...], vc_ref[...]], axis=0)
    lane = lax.broadcasted_iota(jnp.int32, (1, LANES), 1)
    keep = ((lane < LANES // 2).astype(F32).astype(BF), (lane >= LANES // 2).astype(F32).astype(BF))
    lo = lax.broadcasted_iota(jnp.int32, (blk, LANES), 1) < LANES // 2
    per_group = SWA_HEADS // SWA_KV_HEADS

    def score(head):
        pair, sub, g = head // 2, head % 2, head // per_group
        q2 = q_ref[:, pair * LANES:(pair + 1) * LANES]
        return _dot_nt(q2 * keep[sub], kall[:, g * LANES:(g + 1) * LANES]) + bias

    s_next = score(0)
    outs = []
    for head in range(SWA_HEADS):
        s = s_next
        if head + 1 < SWA_HEADS:
            s_next = score(head + 1)
        g = head // per_group
        sk = jnp.full((blk, 1), sink_ref[head], F32) * LOG2E
        m = jnp.maximum(jnp.max(s, axis=-1, keepdims=True), sk)
        p = jnp.exp2(s - m)
        den = jnp.sum(p, axis=-1, keepdims=True) + jnp.exp2(sk - m)
        outs.append(_dot(p.astype(BF), vall[:, g * LANES:(g + 1) * LANES]) / den)
        if head % 2 == 1:
            pair = head // 2
            o_ref[:, pair * LANES:(pair + 1) * LANES] = jnp.where(lo, outs[-2], outs[-1]).astype(BF)


def _win_attn(dims, q, k, v, sink):
    nb = dims.s // BLOCK
    nc = dims.l // BLOCK
    lat_blocks = dims.t_lat // BLOCK
    ctx_blk0 = dims.t_lat // dims.l

    def own(b, n):
        return jnp.where(n < nb, b * nb + n, lat_blocks + b * nc + (n - nb))

    def prev(b, n):
        return jnp.where(n < nb, b * nb + jnp.maximum(n - 1, 0), own(b, n))

    def nxt(b, n):
        return jnp.where(n < nb, b * nb + jnp.minimum(n + 1, nb - 1), own(b, n))

    kv = lambda f: pl.BlockSpec((BLOCK, EVEN_KD), lambda b, n: (f(b, n), 0))
    ctx = pl.BlockSpec((dims.l, EVEN_KD), lambda b, n: (ctx_blk0 + b, 0))
    qo = pl.BlockSpec((BLOCK, EVEN_Q), lambda b, n: (own(b, n), 0))
    return pl.pallas_call(
        functools.partial(_win_attn_kernel, nb=nb),
        grid=(dims.bn, nb + nc),
        in_specs=[pl.BlockSpec(memory_space=pltpu.SMEM), qo, kv(prev), kv(own), kv(nxt), ctx,
                  kv(prev), kv(own), kv(nxt), ctx],
        out_specs=qo,
        out_shape=jax.ShapeDtypeStruct((dims.t_all, EVEN_Q), BF),
        compiler_params=_cparams(("arbitrary", "arbitrary")),
    )(sink, q, k, k, k, k, v, v, v, v)


def _out_proj_kernel(a1_ref, a2_ref, w1_ref, w2_ref, x_ref, gate_ref, g_ref, o_ref):
    y = _dot(a1_ref[...], w1_ref[...]) + _dot(a2_ref[...], w2_ref[...])
    o_ref[...] = x_ref[...] + gate_ref[0] * _rms(y, g_ref[0])


def _out_proj(dims, a1, a2, w1, w2, x, mods3, gains3, tm, gate_col=2, gain_row=1):
    row = lambda width: pl.BlockSpec((tm, width), lambda i: (i, 0))
    return pl.pallas_call(
        _out_proj_kernel,
        grid=(a1.shape[0] // tm,),
        in_specs=[row(a1.shape[1]), row(a2.shape[1]), _full_spec(w1.shape), _full_spec(w2.shape),
                  row(dims.d), _mod_spec(dims, tm, gate_col), _gain_spec(dims, gain_row)],
        out_specs=row(dims.d),
        out_shape=jax.ShapeDtypeStruct(x.shape, F32),
        input_output_aliases={4: 0},
        compiler_params=_cparams(("arbitrary",), 40),
    )(a1, a2, w1, w2, x, mods3, gains3)


def _ffn_up_kernel(x_ref, g_ref, sh_ref, sc_ref, w_ref, lo_ref, hi_ref, *, ff):
    h = _norm_mod(x_ref[...], g_ref[0], sh_ref[0], sc_ref[0]).astype(BF)
    c0 = 0
    for ref in (lo_ref, hi_ref):
        c1 = c0 + ref.shape[1]
        ref[...] = (_silu(_dot(h, w_ref[:, c0:c1])) * _dot(h, w_ref[:, ff + c0:ff + c1])).astype(BF)
        c0 = c1


def _ffn_split(ff):
    lo = (ff // 2 + MXU_N - 1) // MXU_N * MXU_N
    return (lo, ff - lo) if 0 < lo < ff else (ff // 2, ff - ff // 2)


def _ffn_up(dims, x, mods3, gains3, w_gu, tm):
    ff = w_gu.shape[1] // 2
    row = lambda width: pl.BlockSpec((tm, width), lambda i: (i, 0))
    widths = _ffn_split(ff)
    return pl.pallas_call(
        functools.partial(_ffn_up_kernel, ff=ff),
        grid=(dims.t_all // tm,),
        in_specs=[row(dims.d), _gain_spec(dims, 2), _mod_spec(dims, tm, 3), _mod_spec(dims, tm, 4),
                  _full_spec(w_gu.shape)],
        out_specs=[row(wd) for wd in widths],
        out_shape=[jax.ShapeDtypeStruct((dims.t_all, wd), BF) for wd in widths],
        compiler_params=_cparams(("arbitrary",), 56),
    )(x, gains3, mods3, mods3, w_gu)


ODD_STAGE1 = MLA_Q_RANK + MLA_KV_RANK + LANES
MLA_QK = MLA_HEADS * LANES
MLA_VW = MLA_HEADS * MLA_V
DIFF_W = DIFF_HEADS * 2 * DIFF_DIM


def _odd_proj_kernel(x_ref, g_ref, sh_ref, sc_ref, w_ref, wq_ref, wkv_ref, qg_ref, kvg_ref,
                     c32_ref, a32_ref, b32_ref, c64_ref, a64_ref, b64_ref,
                     q_ref, k_ref, v_ref, dq_ref, dk_ref, dv_ref, *, half32, half64, mla_scale, diff_scale):
    h = _norm_mod(x_ref[...], g_ref[0], sh_ref[0], sc_ref[0]).astype(BF)
    r32 = (c32_ref[...], a32_ref[...], b32_ref[...], half32)
    r64 = (c64_ref[...], a64_ref[...], b64_ref[...], half64)
    s1 = _dot(h, w_ref[:, :ODD_STAGE1])
    qn = _rms(s1[:, :MLA_Q_RANK], qg_ref[...]).astype(BF)
    kvn = _rms(s1[:, MLA_Q_RANK:MLA_Q_RANK + MLA_KV_RANK], kvg_ref[...]).astype(BF)
    kpe = _rope(s1[:, MLA_Q_RANK + MLA_KV_RANK:], *r32)
    q_ref[...] = (_rope(_dot(qn, wq_ref[...]), *r32) * mla_scale).astype(BF)
    kn = _dot(kvn, wkv_ref[:, :MLA_QK])
    k_ref[...] = (kn + jnp.concatenate([kpe] * MLA_HEADS, axis=1)).astype(BF)
    v_ref[...] = _dot(kvn, wkv_ref[:, MLA_QK:]).astype(BF)
    c0 = ODD_STAGE1
    dq_ref[...] = (_rope(_dot(h, w_ref[:, c0:c0 + DIFF_W]), *r64) * diff_scale).astype(BF)
    dk_ref[...] = _rope(_dot(h, w_ref[:, c0 + DIFF_W:c0 + 2 * DIFF_W]), *r64).astype(BF)
    dv_ref[...] = _dot(h, w_ref[:, c0 + 2 * DIFF_W:]).astype(BF)


def _odd_proj(dims, x, mods3, gains3, w, wq, wkv, qg, kvg, rope32, rope64, tm):
    row = lambda width: pl.BlockSpec((tm, width), lambda i: (i, 0))
    out_w = (MLA_QK, MLA_QK, MLA_VW, DIFF_W, DIFF_W, DIFF_HEADS * DIFF_V)
    rs = _rope_spec(dims, tm)
    return pl.pallas_call(
        functools.partial(_odd_proj_kernel, half32=rope32[3], half64=rope64[3],
                          mla_scale=(MLA_NOPE + MLA_ROPE) ** -0.5 * LOG2E, diff_scale=DIFF_DIM ** -0.5 * LOG2E),
        grid=(dims.t_all // tm,),
        in_specs=[row(dims.d), _gain_spec(dims, 0), _mod_spec(dims, tm, 0), _mod_spec(dims, tm, 1),
                  _full_spec(w.shape), _full_spec(wq.shape), _full_spec(wkv.shape),
                  _full_spec(qg.shape), _full_spec(kvg.shape), rs, rs, rs, rs, rs, rs],
        out_specs=[row(wd) for wd in out_w],
        out_shape=[jax.ShapeDtypeStruct((dims.t_all, wd), BF) for wd in out_w],
        compiler_params=_cparams(("arbitrary",), 48),
    )(x, gains3, mods3, mods3, w, wq, wkv, qg, kvg, *rope32[:3], *rope64[:3])


def _softmax_pv(streams, k_refs, v_refs):
    pieces = []
    for kr, vr in zip(k_refs, v_refs):
        n = kr.shape[0]
        if pieces and n <= KEY_CHUNK // 2:
            pieces[-1].append((kr, vr, 0, n))
        else:
            pieces.extend([[(kr, vr, c0, min(c0 + KEY_CHUNK, n))] for c0 in range(0, n, KEY_CHUNK)])
    tasks = [(si, q, cols_k, cols_v, pc) for si, (q, cols_k, cols_v) in enumerate(streams) for pc in pieces]

    def score(t):
        parts = [_dot_nt(t[1], kr[c0:c1, t[2]]) for kr, _, c0, c1 in t[4]]
        return parts[0] if len(parts) == 1 else jnp.concatenate(parts, axis=1)

    state = [None] * len(streams)
    s_next = score(tasks[0])
    for ti, t in enumerate(tasks):
        s = s_next
        if ti + 1 < len(tasks):
            s_next = score(tasks[ti + 1])
        si, _, _, cols_v, pc = t
        mc = jnp.max(s, axis=-1, keepdims=True)
        if state[si] is None:
            m_new = mc
        else:
            m, l, acc = state[si]
            m_new = jnp.maximum(m, mc)
        p = jnp.exp2(s - m_new)
        ps = jnp.sum(p, axis=-1, keepdims=True)
        pb = p.astype(BF)
        pv, off = None, 0
        for _, vr, c0, c1 in pc:
            term = _dot(pb[:, off:off + c1 - c0], vr[c0:c1, cols_v])
            pv = term if pv is None else pv + term
            off += c1 - c0
        if state[si] is None:
            state[si] = (m_new, ps, pv)
        else:
            a = jnp.exp2(m - m_new)
            state[si] = (m_new, a * l + ps, a * acc + pv)
    return [acc / l for _, l, acc in state]


def _mla_attn_kernel(q_ref, *refs, n_seg):
    k_refs, v_refs, o_ref = refs[:n_seg], refs[n_seg:2 * n_seg], refs[2 * n_seg]
    cols = [slice(sub * LANES, (sub + 1) * LANES) for sub in range(2)]
    outs = _softmax_pv([(q_ref[:, c], c, slice(None)) for c in cols], k_refs, v_refs)
    lo = lax.broadcasted_iota(jnp.int32, outs[0].shape, 1) < MLA_V
    o_ref[...] = jnp.where(lo, outs[0], outs[1]).astype(BF)


def _diff_attn_kernel(lam_ref, sg_ref, q_ref, *refs, n_seg, lam_init):
    k_refs, v_refs, o_ref = refs[:n_seg], refs[n_seg:2 * n_seg], refs[2 * n_seg]
    lp = lam_ref[...]
    lam = (jnp.exp(jnp.sum(lp[0:1] * lp[1:2], axis=-1, keepdims=True))
           - jnp.exp(jnp.sum(lp[2:3] * lp[3:4], axis=-1, keepdims=True)) + lam_init)
    lane = lax.broadcasted_iota(jnp.int32, (1, LANES), 1)
    q = q_ref[...]
    full = slice(None)
    o1, o2 = _softmax_pv([(q * (lane < DIFF_DIM).astype(F32).astype(BF), full, full),
                          (q * (lane >= DIFF_DIM).astype(F32).astype(BF), full, full)], k_refs, v_refs)
    o_ref[...] = (_rms(o1 - lam * o2, sg_ref[...]) * (1.0 - lam_init)).astype(BF)


def _full_attn_call(dims, kind, q, k, v, ctx_queries, tq, extra, lam_init):
    if kind == "mla":
        n_h, qw, vw = MLA_HEADS // 2, 2 * LANES, LANES
    else:
        n_h, qw, vw = DIFF_HEADS, LANES, LANES
    ctx_blk0 = dims.t_lat // dims.l
    ctx_seg = (dims.l, lambda b: ctx_blk0 + b)
    if ctx_queries:
        tq, nq, out_rows = dims.l, 1, dims.bn * dims.l
        q_row = lambda b, qi: ctx_blk0 + b
        o_row = lambda b, qi: b
        seg = [ctx_seg]
    else:
        nq, out_rows = dims.s // tq, dims.t_lat
        q_row = o_row = lambda b, qi: b * nq + qi
        seg = [(dims.s, lambda b: b), ctx_seg]
    kv_specs = lambda w: [pl.BlockSpec((rows, w), (lambda f: lambda b, hh, qi: (f(b), hh))(f)) for rows, f in seg]
    if kind == "mla":
        body = functools.partial(_mla_attn_kernel, n_seg=len(seg))
    else:
        body = functools.partial(_diff_attn_kernel, n_seg=len(seg), lam_init=lam_init)
    return pl.pallas_call(
        body,
        grid=(dims.bn, n_h, nq),
        in_specs=([_full_spec(e.shape) for e in extra]
                  + [pl.BlockSpec((tq, qw), lambda b, hh, qi: (q_row(b, qi), hh))] + kv_specs(qw) + kv_specs(vw)),
        out_specs=pl.BlockSpec((tq, vw), lambda b, hh, qi: (o_row(b, qi), hh)),
        out_shape=jax.ShapeDtypeStruct((out_rows, n_h * vw), BF),
        compiler_params=_cparams(("arbitrary", "arbitrary", "arbitrary"), 56),
    )(*extra, q, *([k] * len(seg)), *([v] * len(seg)))


def _full_attn(dims, kind, q, k, v, need_ctx, tq, extra=(), lam_init=0.0):
    lat = _full_attn_call(dims, kind, q, k, v, False, tq, extra, lam_init)
    if not need_ctx:
        return lat
    return jnp.concatenate([lat, _full_attn_call(dims, kind, q, k, v, True, tq, extra, lam_init)], axis=0)


def _route_rows(x, g, shift, scale, wr_ref, f_ref, r_ref, n_exp):
    f = _norm_mod(x, g, shift, scale)
    f_ref[...] = f
    logits = jnp.dot(f, wr_ref[...], preferred_element_type=F32, precision=lax.Precision.HIGHEST)
    lane = lax.broadcasted_iota(jnp.int32, logits.shape, 1).astype(F32)
    logits = jnp.where(lane < n_exp, logits, NEG)
    m1 = jnp.max(logits, axis=-1, keepdims=True)
    i1 = jnp.min(jnp.where(logits == m1, lane, float(LANES)), axis=-1, keepdims=True)
    rest = jnp.where(lane == i1, NEG, logits)
    m2 = jnp.max(rest, axis=-1, keepdims=True)
    i2 = jnp.min(jnp.where(rest == m2, lane, float(LANES)), axis=-1, keepdims=True)
    e2 = jnp.exp(m2 - m1)
    w1 = 1.0 / (1.0 + e2)
    w2 = e2 / (1.0 + e2)
    r_ref[...] = jnp.where(lane == 0, i1, jnp.where(lane == 1, i2, jnp.where(lane == 2, w1,
                           jnp.where(lane == 3, w2, 0.0))))


def _out_proj_router_kernel(a1_ref, a2_ref, w1_ref, w2_ref, x_ref, gate_ref, g_ref, g2_ref, sh_ref, sc_ref, wr_ref,
                            o_ref, f_ref, r_ref, *, n_exp):
    y = _dot(a1_ref[...], w1_ref[...]) + _dot(a2_ref[...], w2_ref[...])
    x = x_ref[...] + gate_ref[0] * _rms(y, g_ref[0])
    o_ref[...] = x
    _route_rows(x, g2_ref[0], sh_ref[0], sc_ref[0], wr_ref, f_ref, r_ref, n_exp)


def _out_proj_router(dims, a1, a2, w1, w2, x, mods3, gains3, w_router_pad, n_exp, tm):
    rows = a1.shape[0]
    row = lambda width: pl.BlockSpec((tm, width), lambda i: (i, 0))
    return pl.pallas_call(
        functools.partial(_out_proj_router_kernel, n_exp=n_exp),
        grid=(rows // tm,),
        in_specs=[row(a1.shape[1]), row(a2.shape[1]), _full_spec(w1.shape), _full_spec(w2.shape),
                  row(dims.d), _mod_spec(dims, tm, 2), _gain_spec(dims, 1),
                  _gain_spec(dims, 2), _mod_spec(dims, tm, 3), _mod_spec(dims, tm, 4),
                  _full_spec(w_router_pad.shape)],
        out_specs=[row(dims.d), row(dims.d), row(LANES)],
        out_shape=[jax.ShapeDtypeStruct(x.shape, F32), jax.ShapeDtypeStruct((rows, dims.d), F32),
                   jax.ShapeDtypeStruct((rows, LANES), F32)],
        input_output_aliases={4: 0},
        compiler_params=_cparams(("arbitrary",), 48),
    )(a1, a2, w1, w2, x, mods3, gains3, gains3, mods3, mods3, w_router_pad)


def _row_copy(src_hbm, dst, s, d, sem):
    return pltpu.make_async_copy(src_hbm.at[pl.ds(s, 1)], dst.at[pl.ds(d, 1)], sem)


def _expert_kernel(be_ref, nu_ref, src_cur, src_nxt, x_hbm, wg_ref, wu_ref, wd_ref, o_ref,
                   xbuf, sem, h_s, acc_s, *, tm, nj):
    blk, j = pl.program_id(0), pl.program_id(1)
    nblk = pl.num_programs(0)
    n_used = nu_ref[0]
    used = blk < n_used
    slot = blk % 2
    per_step = tm // nj
    head = tm - per_step * nj

    def issue(src_ref, dst_slot, r):
        _row_copy(x_hbm, xbuf.at[dst_slot], src_ref[0, 0, r], r, sem.at[dst_slot]).start()

    def wait_slot(s):
        pltpu.make_async_copy(x_hbm.at[pl.ds(0, tm)], xbuf.at[s], sem.at[s]).wait()

    @pl.when(j == 0)
    def _():
        @pl.when(blk == 0)
        def _():
            def first(r, c):
                issue(src_cur, 0, r)
                return c

            lax.fori_loop(0, tm, first, 0, unroll=8)

        @pl.when(blk <= n_used)
        def _():
            wait_slot(slot)

        @pl.when(used)
        def _():
            h_s[...] = xbuf[slot].astype(BF)
            for r in range(head):
                issue(src_nxt, 1 - slot, r)

        acc_s[...] = jnp.zeros_like(acc_s)

    @pl.when(used)
    def _():
        for r in range(per_step):
            issue(src_nxt, 1 - slot, head + j * per_step + r)
        h = h_s[...]
        act = (_silu(_dot(h, wg_ref[0, 0].astype(BF))) * _dot(h, wu_ref[0, 0].astype(BF))).astype(BF)
        acc_s[...] += _dot(act, wd_ref[0, 0].astype(BF))

    @pl.when(j == nj - 1)
    def _():
        o_ref[...] = acc_s[...]

        @pl.when(jnp.logical_and(blk == nblk - 1, used))
        def _():
            wait_slot(1 - slot)


def _experts(f_in, src, block_e, n_used, w_gu, w_down, layer_i, tm, tn):
    d = f_in.shape[1]
    ff = w_down.shape[2]
    nj = ff // tn
    nblk = src.shape[0] // tm
    jj = lambda blk, j, nu: jnp.where(blk < nu[0], j, nj - 1)
    src3 = src.reshape(nblk, 1, tm)
    return pl.pallas_call(
        functools.partial(_expert_kernel, tm=tm, nj=nj),
        grid_spec=pltpu.PrefetchScalarGridSpec(
            num_scalar_prefetch=2,
            grid=(nblk, nj),
            in_specs=[pl.BlockSpec((1, 1, tm), lambda blk, j, be, nu: (blk, 0, 0), memory_space=pltpu.SMEM),
                      pl.BlockSpec((1, 1, tm), lambda blk, j, be, nu: (jnp.minimum(blk + 1, nblk - 1), 0, 0),
                                   memory_space=pltpu.SMEM),
                      pl.BlockSpec(memory_space=pl.ANY),
                      pl.BlockSpec((1, 1, d, tn), lambda blk, j, be, nu: (layer_i, be[blk], 0, jj(blk, j, nu))),
                      pl.BlockSpec((1, 1, d, tn), lambda blk, j, be, nu: (layer_i, be[blk], 0, nj + jj(blk, j, nu))),
                      pl.BlockSpec((1, 1, tn, d), lambda blk, j, be, nu: (layer_i, be[blk], jj(blk, j, nu), 0))],
            out_specs=pl.BlockSpec((tm, d), lambda blk, j, be, nu: (blk, 0)),
            scratch_shapes=[pltpu.VMEM((2, tm, d), F32), pltpu.SemaphoreType.DMA((2,)),
                            pltpu.VMEM((tm, d), BF), pltpu.VMEM((tm, d), F32)]),
        out_shape=jax.ShapeDtypeStruct((nblk * tm, d), F32),
        compiler_params=_cparams(("arbitrary", "arbitrary"), 56),
    )(block_e, n_used, src3, src3, f_in, w_gu, w_gu, w_down)


def _combine_kernel(dest_ref, r_ref, x_ref, gate_ref, g_ref, y_hbm, o_ref, buf, sem, *, tm):
    def issue(t, c):
        for k in range(TOP_K):
            _row_copy(y_hbm, buf.at[k], dest_ref[0, 0, TOP_K * t + k], t, sem.at[k]).start()
        return c

    lax.fori_loop(0, tm, issue, 0, unroll=4)
    for k in range(TOP_K):
        pltpu.make_async_copy(y_hbm.at[pl.ds(0, tm)], buf.at[k], sem.at[k]).wait()
    r = r_ref[...]
    y = buf[0] * r[:, 2:3] + buf[1] * r[:, 3:4]
    o_ref[...] = x_ref[...] + gate_ref[0] * _rms(y, g_ref[0])


def _combine(dims, y, dest, route, x, mods3, gains3, tm):
    rows = route.shape[0]
    nt = rows // tm
    row = lambda width: pl.BlockSpec((tm, width), lambda i: (i, 0))
    return pl.pallas_call(
        functools.partial(_combine_kernel, tm=tm),
        grid=(nt,),
        in_specs=[pl.BlockSpec((1, 1, TOP_K * tm), lambda i: (i, 0, 0), memory_space=pltpu.SMEM),
                  row(LANES), row(dims.d), _mod_spec(dims, tm, 5), _gain_spec(dims, 3),
                  pl.BlockSpec(memory_space=pl.ANY)],
        out_specs=row(dims.d),
        out_shape=jax.ShapeDtypeStruct((rows, dims.d), F32),
        scratch_shapes=[pltpu.VMEM((TOP_K, tm, dims.d), F32), pltpu.SemaphoreType.DMA((TOP_K,))],
        input_output_aliases={2: 0} if rows == x.shape[0] else {},
        compiler_params=_cparams(("arbitrary",)),
    )(dest.reshape(nt, 1, TOP_K * tm), route, x, mods3, gains3, y)


def _routing_tables(route, n_exp, tm_e):
    t = route.shape[0]
    a = t * TOP_K
    flat_e = route[:, :TOP_K].astype(jnp.int32).reshape(a)
    onehot = (flat_e[:, None] == jnp.arange(n_exp)[None, :]).astype(jnp.int32)
    csum = jnp.cumsum(onehot, axis=0)
    rank = jnp.take_along_axis(csum, flat_e[:, None], axis=1)[:, 0] - 1
    counts = csum[-1]
    padded = (counts + tm_e - 1) // tm_e * tm_e
    ends = jnp.cumsum(padded)
    dest = (ends - padded)[flat_e] + rank
    n_blocks = -(-(a + n_exp * (tm_e - 1)) // tm_e)
    src = jnp.zeros((n_blocks * tm_e,), jnp.int32).at[dest].set(jnp.arange(a, dtype=jnp.int32) // TOP_K)
    block_start = jnp.arange(n_blocks, dtype=jnp.int32) * tm_e
    block_e = jnp.minimum(jnp.sum((ends[None, :] <= block_start[:, None]).astype(jnp.int32), axis=1), n_exp - 1)
    n_used = (ends[-1] // tm_e).reshape(1)
    return dest.astype(jnp.int32), src, block_e.astype(jnp.int32), n_used.astype(jnp.int32)


def _even_weights(w_in):
    kw = SWA_KV_HEADS * SWA_HEAD_DIM
    c0 = EVEN_BCU + EVEN_Q
    wk, wv = w_in[:, c0:c0 + kw], w_in[:, c0 + kw:c0 + 2 * kw]
    dup = lambda w: jnp.concatenate(
        [w[:, g * SWA_HEAD_DIM:(g + 1) * SWA_HEAD_DIM] for g in range(SWA_KV_HEADS) for _ in range(2)], axis=1)
    return jnp.concatenate([w_in[:, :c0], dup(wk), dup(wv)], axis=1).astype(BF)


def _odd_weights(w_in, w_q_up, w_kv_up):
    d = w_in.shape[0]
    c = 0
    mq = w_in[:, c:c + MLA_Q_RANK]; c += MLA_Q_RANK
    dq = w_in[:, c:c + DIFF_W]; c += DIFF_W
    kvd = w_in[:, c:c + MLA_KV_RANK]; c += MLA_KV_RANK
    kpe = w_in[:, c:c + MLA_ROPE]; c += MLA_ROPE
    dk = w_in[:, c:c + DIFF_W]; c += DIFF_W
    dv = w_in[:, c:]
    pad_tail = LANES - MLA_NOPE - MLA_ROPE
    kpe_chunk = jnp.concatenate([jnp.zeros((d, MLA_NOPE), F32), kpe, jnp.zeros((d, pad_tail), F32)], axis=1)
    w = jnp.concatenate([mq, kvd, kpe_chunk, dq, dk, dv], axis=1).astype(BF)
    qh = w_q_up.reshape(MLA_Q_RANK, MLA_HEADS, MLA_NOPE + MLA_ROPE)
    wq = jnp.pad(qh, ((0, 0), (0, 0), (0, pad_tail))).reshape(MLA_Q_RANK, MLA_QK).astype(BF)
    kvh = w_kv_up.reshape(MLA_KV_RANK, MLA_HEADS, MLA_NOPE + MLA_V)
    wk = jnp.pad(kvh[:, :, :MLA_NOPE], ((0, 0), (0, 0), (0, LANES - MLA_NOPE))).reshape(MLA_KV_RANK, MLA_QK)
    wv = kvh[:, :, MLA_NOPE:].reshape(MLA_KV_RANK, MLA_VW)
    return w, wq, jnp.concatenate([wk, wv], axis=1).astype(BF)


def kernel(x, c, ctx, c_ctx, w_mod, b_mod, norm_g, w_in_even, conv_w, sink, w_out_even, w_in_odd, mla_q_norm_g,
           mla_kv_norm_g, w_q_up, w_kv_up, diff_lambda, diff_subln_g, w_out_odd, w_ff_gu, w_ff_down, w_router,
           w_exp_gu, w_exp_down):
    bn, s, d = x.shape
    l = ctx.shape[1]
    depth = w_mod.shape[0]
    n_exp = w_router.shape[-1]
    dims = _Dims(bn, s, l, d)
    assert bn < MOD_ROWS and s % l == 0 and l % BLOCK == 0 and s % GRID_W == 0

    tm = _tile(math.gcd(s, bn * l), 512)
    tm_e = 1024 if dims.t_all >= 8192 else 256
    tq = _tile(s, 512)

    cc = jnp.zeros((MOD_ROWS, d), F32).at[:bn].set(c).at[bn].set(c_ctx)
    mods = _modulation(cc, w_mod, b_mod)
    rope64 = _rope_tables(s, tm, SWA_HEAD_DIM, SWA_HEAD_DIM, 0)
    rope32 = _rope_tables(s, tm, MLA_ROPE, LANES, MLA_NOPE)

    xa = None
    for layer in range(depth):
        i = layer // 2
        mods3 = mods[layer].reshape(MOD_ROWS, 1, N_MOD * d)
        gains3 = norm_g[layer].reshape(4, 1, d)
        if layer % 2 == 0:
            xs = (x.reshape(bn * s, d), ctx.reshape(bn * l, d)) if xa is None else (xa,)
            bcu, q, k, v, *merged = _even_proj(dims, xs, mods3, gains3, _even_weights(w_in_even[i]), rope64, tm)
            xa = merged[0] if merged else xa
            conv = _gated_conv(dims, bcu, conv_w[i])
            attn = _win_attn(dims, q, k, v, sink[i])
            wo = w_out_even[i].astype(BF)
            xa = _out_proj(dims, conv, attn, wo[:CONV_WIDTH], wo[CONV_WIDTH:], xa, mods3, gains3, tm)
            act_lo, act_hi = _ffn_up(dims, xa, mods3, gains3, w_ff_gu[i].astype(BF), tm)
            wd = w_ff_down[i].astype(BF)
            xa = _out_proj(dims, act_lo, act_hi, wd[:act_lo.shape[1]], wd[act_lo.shape[1]:], xa, mods3, gains3, tm,
                           gate_col=5, gain_row=3)
        else:
            w, wq, wkv = _odd_weights(w_in_odd[i], w_q_up[i], w_kv_up[i])
            qm, km, vm, dq, dk, dv = _odd_proj(dims, xa, mods3, gains3, w, wq, wkv,
                                               mla_q_norm_g[i].reshape(1, -1), mla_kv_norm_g[i].reshape(1, -1),
                                               rope32, rope64, tm)
            lam_init = 0.8 - 0.6 * math.exp(-0.3 * layer)
            extra = (diff_lambda[i], diff_subln_g[i].reshape(1, -1))
            need_ctx = layer < depth - 1
            o_m = _full_attn(dims, "mla", qm, km, vm, need_ctx, tq)
            o_d = _full_attn(dims, "diff", dq, dk, dv, need_ctx, tq, extra, lam_init)
            wo = w_out_odd[i].astype(BF)
            wr = jnp.pad(w_router[i], ((0, 0), (0, LANES - n_exp)))
            xa, f_in, route = _out_proj_router(dims, o_m, o_d, wo[:MLA_VW], wo[MLA_VW:], xa, mods3, gains3,
                                               wr, n_exp, tm)
            dest, src, block_e, n_used = _routing_tables(route, n_exp, tm_e)
            y = _experts(f_in, src, block_e, n_used, w_exp_gu, w_exp_down, i, tm_e,
                         _tile(w_exp_down.shape[2], 512))
            xa = _combine(dims, y, dest, route, xa, mods3, gains3, tm)
    return xa[:bn * s].reshape(bn, s, d)
```

```python
import functools
import math

import jax
import jax.numpy as jnp
from jax import lax
from jax.experimental import pallas as pl
from jax.experimental.pallas import tpu as pltpu

F32 = jnp.float32
BF = jnp.bfloat16

EPS = 1e-6
ROPE_THETA = 10000.0
GRID_W = 64
BLOCK = 128
N_MOD = 6
LANES = 128
MXU_N = 256
MOD_ROWS = 16

CONV_WIDTH = 512
SWA_HEADS = 8
SWA_KV_HEADS = 2
SWA_HEAD_DIM = 64
MLA_HEADS = 8
MLA_Q_RANK = 256
MLA_KV_RANK = 128
MLA_NOPE = 64
MLA_ROPE = 32
MLA_V = 64
DIFF_HEADS = 4
DIFF_DIM = 64
DIFF_V = 2 * DIFF_DIM
TOP_K = 2

NEG = -1e30
LOG2E = math.log2(math.e)
KEY_CHUNK = 1024


def _tile(n, pref):
    if n <= pref:
        return n
    t = pref - pref % LANES
    while t >= LANES:
        if n % t == 0:
            return t
        t -= LANES
    raise ValueError((n, pref))


def _cparams(sem, vmem_mb=None):
    kw = dict(dimension_semantics=sem)
    if vmem_mb is not None:
        kw["vmem_limit_bytes"] = vmem_mb << 20
    return pltpu.CompilerParams(**kw)


def _dot(a, b):
    return jnp.dot(a, b, preferred_element_type=F32)


def _dot_nt(a, b):
    return lax.dot_general(a, b, (((1,), (1,)), ((), ())), preferred_element_type=F32)


def _rms(x, g):
    return x * lax.rsqrt(jnp.mean(x * x, axis=-1, keepdims=True) + EPS) * g


def _norm_mod(x, g, shift, scale):
    return _rms(x, g) * (1.0 + scale) + shift


def _silu(x):
    return x / (1.0 + jnp.exp(-x))


def _rope(x, cos, sa, sb, half):
    out = []
    for c in range(x.shape[1] // LANES):
        xc = x[:, c * LANES:(c + 1) * LANES]
        out.append(xc * cos + pltpu.roll(xc, LANES - half, 1) * sa + pltpu.roll(xc, half, 1) * sb)
    return out[0] if len(out) == 1 else jnp.concatenate(out, axis=1)


def _mod_kernel(cc_ref, w_ref, b_ref, o_ref):
    a = _silu(cc_ref[...]).astype(BF)
    o_ref[0] = _dot(a, w_ref[0].astype(BF)) + b_ref[0]


def _modulation(cc, w_mod, b_mod):
    depth, d, n = w_mod.shape
    tn = _tile(n, 1536)
    return pl.pallas_call(
        _mod_kernel,
        grid=(depth, n // tn),
        in_specs=[
            pl.BlockSpec((MOD_ROWS, d), lambda l, j: (0, 0)),
            pl.BlockSpec((1, d, tn), lambda l, j: (l, 0, j)),
            pl.BlockSpec((1, 1, tn), lambda l, j: (l, 0, j)),
        ],
        out_specs=pl.BlockSpec((1, MOD_ROWS, tn), lambda l, j: (l, 0, j)),
        out_shape=jax.ShapeDtypeStruct((depth, MOD_ROWS, n), F32),
        compiler_params=_cparams(("arbitrary", "arbitrary"), 40),
    )(cc, w_mod, b_mod.reshape(depth, 1, n))


class _Dims:
    def __init__(self, bn, s, l, d):
        self.bn, self.s, self.l, self.d = bn, s, l, d
        self.t_lat = bn * s
        self.t_all = bn * s + bn * l

    def mod_row(self, i, tm):
        return jnp.minimum(i * tm // self.s, self.bn)


def _mod_spec(dims, tm, col):
    return pl.BlockSpec((1, 1, dims.d), lambda i, *_: (dims.mod_row(i, tm), 0, col))


def _gain_spec(dims, k):
    return pl.BlockSpec((1, 1, dims.d), lambda i, *_: (k, 0, 0))


def _rope_spec(dims, tm):
    n_lat, per_seq = dims.t_lat // tm, dims.s // tm
    return pl.BlockSpec((tm, LANES), lambda i: (jnp.where(i < n_lat, i % per_seq, per_seq), 0))


def _full_spec(shape):
    return pl.BlockSpec(shape, lambda *_: (0,) * len(shape))


def _rope_tables(s, pad_rows, rot_dim, period, lane_off):
    axis_dim = rot_dim // 2
    half = axis_dim // 2
    inv = 1.0 / (ROPE_THETA ** (jnp.arange(0, axis_dim, 2, dtype=F32) / axis_dim))
    pos = jnp.arange(s)
    rows = (pos // GRID_W).astype(F32)[:, None]
    cols = (pos % GRID_W).astype(F32)[:, None]
    lane = jnp.arange(LANES)
    dd = lane % period - lane_off
    active = (dd >= 0) & (dd < rot_dim)
    dd = jnp.clip(dd, 0, rot_dim - 1)
    j = dd % axis_dim
    ang = jnp.where((dd // axis_dim == 0)[None, :], rows, cols) * inv[j % half][None, :]
    first = (j < half)[None, :]
    act = active[None, :]
    cos = jnp.where(act, jnp.cos(ang), 1.0)
    sin = jnp.where(act, jnp.sin(ang), 0.0)
    sa = jnp.where(first, -sin, 0.0)
    sb = jnp.where(first, 0.0, sin)
    pad = lambda t, v: jnp.concatenate([t, jnp.full((pad_rows, LANES), v, F32)], axis=0)
    return pad(cos, 1.0), pad(sa, 0.0), pad(sb, 0.0), half


EVEN_BCU = 3 * CONV_WIDTH
EVEN_Q = SWA_HEADS * SWA_HEAD_DIM
EVEN_KD = SWA_KV_HEADS * LANES


def _even_proj_kernel(*refs, half, q_scale, n_lat_tiles):
    if n_lat_tiles is None:
        x_ref, g_ref, sh_ref, sc_ref, w_ref, cos_ref, sa_ref, sb_ref, bcu_ref, q_ref, k_ref, v_ref = refs
        x = x_ref[...]
    else:
        (xl_ref, xc_ref, g_ref, sh_ref, sc_ref, w_ref, cos_ref, sa_ref, sb_ref,
         bcu_ref, q_ref, k_ref, v_ref, xa_ref) = refs
        x = jnp.where(pl.program_id(0) < n_lat_tiles, xl_ref[...], xc_ref[...])
        xa_ref[...] = x
    h = _norm_mod(x, g_ref[0], sh_ref[0], sc_ref[0]).astype(BF)
    c0, c1, c2 = EVEN_BCU, EVEN_BCU + EVEN_Q, EVEN_BCU + EVEN_Q + EVEN_KD
    bcu_ref[...] = _dot(h, w_ref[:, :c0]).astype(BF)
    qk = _rope(_dot(h, w_ref[:, c0:c2]), cos_ref[...], sa_ref[...], sb_ref[...], half)
    q_ref[...] = (qk[:, :EVEN_Q] * q_scale).astype(BF)
    k_ref[...] = qk[:, EVEN_Q:].astype(BF)
    v_ref[...] = _dot(h, w_ref[:, c2:]).astype(BF)


def _even_proj(dims, xs, mods3, gains3, w, rope, tm):
    n = w.shape[1]
    cos, sa, sb, half = rope
    row = lambda width: pl.BlockSpec((tm, width), lambda i: (i, 0))
    out_w = (EVEN_BCU, EVEN_Q, EVEN_KD, EVEN_KD)
    out_specs = [row(wd) for wd in out_w]
    out_shape = [jax.ShapeDtypeStruct((dims.t_all, wd), BF) for wd in out_w]
    if len(xs) == 1:
        n_lat, x_specs = None, [row(dims.d)]
    else:
        n_lat = dims.t_lat // tm
        x_specs = [pl.BlockSpec((tm, dims.d), lambda i: (jnp.minimum(i, n_lat - 1), 0)),
                   pl.BlockSpec((tm, dims.d), lambda i: (jnp.maximum(i - n_lat, 0), 0))]
        out_specs.append(row(dims.d))
        out_shape.append(jax.ShapeDtypeStruct((dims.t_all, dims.d), F32))
    return pl.pallas_call(
        functools.partial(_even_proj_kernel, half=half, q_scale=SWA_HEAD_DIM ** -0.5 * LOG2E, n_lat_tiles=n_lat),
        grid=(dims.t_all // tm,),
        in_specs=x_specs + [_gain_spec(dims, 0), _mod_spec(dims, tm, 0), _mod_spec(dims, tm, 1),
                            _full_spec((dims.d, n)), _rope_spec(dims, tm), _rope_spec(dims, tm),
                            _rope_spec(dims, tm)],
        out_specs=out_specs,
        out_shape=out_shape,
        compiler_params=_cparams(("arbitrary",), 48),
    )(*xs, gains3, mods3, mods3, w, cos, sa, sb)


def _conv_kernel(b_ref, c_ref, u_ref, cp_ref, up_ref, cn_ref, un_ref, w_ref, o_ref, *, n_lat_blocks, per_seq):
    i = pl.program_id(0)
    rows = b_ref.shape[0]
    hr = cp_ref.shape[0]
    pos = i % per_seq
    is_lat = i < n_lat_blocks
    has_prev = jnp.logical_and(is_lat, pos > 0).astype(F32)
    has_next = jnp.logical_and(is_lat, pos < per_seq - 1).astype(F32)
    cu = c_ref[...].astype(F32) * u_ref[...].astype(F32)
    cu_p = (cp_ref[...].astype(F32) * up_ref[...].astype(F32))[hr - 1:hr] * has_prev
    cu_n = (cn_ref[...].astype(F32) * un_ref[...].astype(F32))[0:1] * has_next
    r = lax.broadcasted_iota(jnp.int32, cu.shape, 0)
    prev = jnp.where(r == 0, cu_p, pltpu.roll(cu, 1, 0))
    nxt = jnp.where(r == rows - 1, cu_n, pltpu.roll(cu, rows - 1, 0))
    w = w_ref[...]
    o_ref[...] = (b_ref[...].astype(F32) * (prev * w[0:1] + cu * w[1:2] + nxt * w[2:3])).astype(BF)


def _gated_conv(dims, bcu, conv_w):
    rows = dims.l
    hr = 16
    nblk = dims.t_all // rows
    per = rows // hr
    last = dims.t_all // hr - 1
    cw = CONV_WIDTH
    main = lambda col: pl.BlockSpec((rows, cw), lambda i: (i, col))
    prev = lambda col: pl.BlockSpec((hr, cw), lambda i: (jnp.maximum(i * per - 1, 0), col))
    nxt = lambda col: pl.BlockSpec((hr, cw), lambda i: (jnp.minimum((i + 1) * per, last), col))
    return pl.pallas_call(
        functools.partial(_conv_kernel, n_lat_blocks=dims.t_lat // rows, per_seq=dims.s // rows),
        grid=(nblk,),
        in_specs=[main(0), main(1), main(2), prev(1), prev(2), nxt(1), nxt(2), _full_spec(conv_w.shape)],
        out_specs=pl.BlockSpec((rows, cw), lambda i: (i, 0)),
        out_shape=jax.ShapeDtypeStruct((dims.t_all, cw), BF),
        compiler_params=_cparams(("arbitrary",)),
    )(bcu, bcu, bcu, bcu, bcu, bcu, bcu, conv_w)


def _win_attn_kernel(sink_ref, q_ref, kp_ref, ko_ref, kn_ref, kc_ref, vp_ref, vo_ref, vn_ref, vc_ref,
                     o_ref, *, nb):
    n = pl.program_id(1)
    blk = q_ref.shape[0]
    n_ctx = kc_ref.shape[0]
    is_lat = n < nb
    lo_s = jnp.where(is_lat, jnp.where(n >= 1, 0, blk), 3 * blk)
    hi_s = jnp.where(is_lat, jnp.where(n + 1 < nb, 3 * blk, 2 * blk), 0)
    shape = (blk, 3 * blk + n_ctx)
    c = lax.broadcasted_iota(jnp.int32, shape, 1)
    r = lax.broadcasted_iota(jnp.int32, shape, 0)
    ok = (c >= 3 * blk) | ((c >= r) & (c - 2 * blk <= r) & (c >= lo_s) & (c < hi_s))
    bias = jnp.where(ok, 0.0, NEG)
    kall = jnp.concatenate([kp_ref[...], ko_ref[...], kn_ref[...], kc_ref[...]], axis=0)
    vall = jnp.concatenate([vp_ref[...], vo_ref[...], vn_ref[...], vc_ref[...]], axis=0)
    lane = lax.broadcasted_iota(jnp.int32, (1, LANES), 1)
    keep = ((lane < LANES // 2).astype(F32).astype(BF), (lane >= LANES // 2).astype(F32).astype(BF))
    lo = lax.broadcasted_iota(jnp.int32, (blk, LANES), 1) < LANES // 2
    per_group = SWA_HEADS // SWA_KV_HEADS

    def score(head):
        pair, sub, g = head // 2, head % 2, head // per_group
        q2 = q_ref[:, pair * LANES:(pair + 1) * LANES]
        return _dot_nt(q2 * keep[sub], kall[:, g * LANES:(g + 1) * LANES]) + bias

    s_next = score(0)
    outs = []
    for head in range(SWA_HEADS):
        s = s_next
        if head + 1 < SWA_HEADS:
            s_next = score(head + 1)
        g = head // per_group
        sk = jnp.full((blk, 1), sink_ref[head], F32) * LOG2E
        m = jnp.maximum(jnp.max(s, axis=-1, keepdims=True), sk)
        p = jnp.exp2(s - m)
        den = jnp.sum(p, axis=-1, keepdims=True) + jnp.exp2(sk - m)
        outs.append(_dot(p.astype(BF), vall[:, g * LANES:(g + 1) * LANES]) / den)
        if head % 2 == 1:
            pair = head // 2
            o_ref[:, pair * LANES:(pair + 1) * LANES] = jnp.where(lo, outs[-2], outs[-1]).astype(BF)


def _win_attn(dims, q, k, v, sink):
    nb = dims.s // BLOCK
    nc = dims.l // BLOCK
    lat_blocks = dims.t_lat // BLOCK
    ctx_blk0 = dims.t_lat // dims.l

    def own(b, n):
        return jnp.where(n < nb, b * nb + n, lat_blocks + b * nc + (n - nb))

    def prev(b, n):
        return jnp.where(n < nb, b * nb + jnp.maximum(n - 1, 0), own(b, n))

    def nxt(b, n):
        return jnp.where(n < nb, b * nb + jnp.minimum(n + 1, nb - 1), own(b, n))

    kv = lambda f: pl.BlockSpec((BLOCK, EVEN_KD), lambda b, n: (f(b, n), 0))
    ctx = pl.BlockSpec((dims.l, EVEN_KD), lambda b, n: (ctx_blk0 + b, 0))
    qo = pl.BlockSpec((BLOCK, EVEN_Q), lambda b, n: (own(b, n), 0))
    return pl.pallas_call(
        functools.partial(_win_attn_kernel, nb=nb),
        grid=(dims.bn, nb + nc),
        in_specs=[pl.BlockSpec(memory_space=pltpu.SMEM), qo, kv(prev), kv(own), kv(nxt), ctx,
                  kv(prev), kv(own), kv(nxt), ctx],
        out_specs=qo,
        out_shape=jax.ShapeDtypeStruct((dims.t_all, EVEN_Q), BF),
        compiler_params=_cparams(("arbitrary", "arbitrary")),
    )(sink, q, k, k, k, k, v, v, v, v)


def _out_proj_kernel(a1_ref, a2_ref, w1_ref, w2_ref, x_ref, gate_ref, g_ref, o_ref):
    y = _dot(a1_ref[...], w1_ref[...]) + _dot(a2_ref[...], w2_ref[...])
    o_ref[...] = x_ref[...] + gate_ref[0] * _rms(y, g_ref[0])


def _out_proj(dims, a1, a2, w1, w2, x, mods3, gains3, tm, gate_col=2, gain_row=1):
    row = lambda width: pl.BlockSpec((tm, width), lambda i: (i, 0))
    return pl.pallas_call(
        _out_proj_kernel,
        grid=(a1.shape[0] // tm,),
        in_specs=[row(a1.shape[1]), row(a2.shape[1]), _full_spec(w1.shape), _full_spec(w2.shape),
                  row(dims.d), _mod_spec(dims, tm, gate_col), _gain_spec(dims, gain_row)],
        out_specs=row(dims.d),
        out_shape=jax.ShapeDtypeStruct(x.shape, F32),
        input_output_aliases={4: 0},
        compiler_params=_cparams(("arbitrary",), 40),
    )(a1, a2, w1, w2, x, mods3, gains3)


def _ffn_up_kernel(x_ref, g_ref, sh_ref, sc_ref, w_ref, lo_ref, hi_ref, *, ff):
    h = _norm_mod(x_ref[...], g_ref[0], sh_ref[0], sc_ref[0]).astype(BF)
    c0 = 0
    for ref in (lo_ref, hi_ref):
        c1 = c0 + ref.shape[1]
        ref[...] = (_silu(_dot(h, w_ref[:, c0:c1])) * _dot(h, w_ref[:, ff + c0:ff + c1])).astype(BF)
        c0 = c1


def _ffn_split(ff):
    lo = (ff // 2 + MXU_N - 1) // MXU_N * MXU_N
    return (lo, ff - lo) if 0 < lo < ff else (ff // 2, ff - ff // 2)


def _ffn_up(dims, x, mods3, gains3, w_gu, tm):
    ff = w_gu.shape[1] // 2
    row = lambda width: pl.BlockSpec((tm, width), lambda i: (i, 0))
    widths = _ffn_split(ff)
    return pl.pallas_call(
        functools.partial(_ffn_up_kernel, ff=ff),
        grid=(dims.t_all // tm,),
        in_specs=[row(dims.d), _gain_spec(dims, 2), _mod_spec(dims, tm, 3), _mod_spec(dims, tm, 4),
                  _full_spec(w_gu.shape)],
        out_specs=[row(wd) for wd in widths],
        out_shape=[jax.ShapeDtypeStruct((dims.t_all, wd), BF) for wd in widths],
        compiler_params=_cparams(("arbitrary",), 56),
    )(x, gains3, mods3, mods3, w_gu)


ODD_STAGE1 = MLA_Q_RANK + MLA_KV_RANK + LANES
MLA_QK = MLA_HEADS * LANES
MLA_VW = MLA_HEADS * MLA_V
DIFF_W = DIFF_HEADS * 2 * DIFF_DIM


def _odd_proj_kernel(x_ref, g_ref, sh_ref, sc_ref, w_ref, wq_ref, wkv_ref, qg_ref, kvg_ref,
                     c32_ref, a32_ref, b32_ref, c64_ref, a64_ref, b64_ref,
                     q_ref, k_ref, v_ref, dq_ref, dk_ref, dv_ref, *, half32, half64, mla_scale, diff_scale):
    h = _norm_mod(x_ref[...], g_ref[0], sh_ref[0], sc_ref[0]).astype(BF)
    r32 = (c32_ref[...], a32_ref[...], b32_ref[...], half32)
    r64 = (c64_ref[...], a64_ref[...], b64_ref[...], half64)
    s1 = _dot(h, w_ref[:, :ODD_STAGE1])
    qn = _rms(s1[:, :MLA_Q_RANK], qg_ref[...]).astype(BF)
    kvn = _rms(s1[:, MLA_Q_RANK:MLA_Q_RANK + MLA_KV_RANK], kvg_ref[...]).astype(BF)
    kpe = _rope(s1[:, MLA_Q_RANK + MLA_KV_RANK:], *r32)
    q_ref[...] = (_rope(_dot(qn, wq_ref[...]), *r32) * mla_scale).astype(BF)
    kn = _dot(kvn, wkv_ref[:, :MLA_QK])
    k_ref[...] = (kn + jnp.concatenate([kpe] * MLA_HEADS, axis=1)).astype(BF)
    v_ref[...] = _dot(kvn, wkv_ref[:, MLA_QK:]).astype(BF)
    c0 = ODD_STAGE1
    dq_ref[...] = (_rope(_dot(h, w_ref[:, c0:c0 + DIFF_W]), *r64) * diff_scale).astype(BF)
    dk_ref[...] = _rope(_dot(h, w_ref[:, c0 + DIFF_W:c0 + 2 * DIFF_W]), *r64).astype(BF)
    dv_ref[...] = _dot(h, w_ref[:, c0 + 2 * DIFF_W:]).astype(BF)


def _odd_proj(dims, x, mods3, gains3, w, wq, wkv, qg, kvg, rope32, rope64, tm):
    row = lambda width: pl.BlockSpec((tm, width), lambda i: (i, 0))
    out_w = (MLA_QK, MLA_QK, MLA_VW, DIFF_W, DIFF_W, DIFF_HEADS * DIFF_V)
    rs = _rope_spec(dims, tm)
    return pl.pallas_call(
        functools.partial(_odd_proj_kernel, half32=rope32[3], half64=rope64[3],
                          mla_scale=(MLA_NOPE + MLA_ROPE) ** -0.5 * LOG2E, diff_scale=DIFF_DIM ** -0.5 * LOG2E),
        grid=(dims.t_all // tm,),
        in_specs=[row(dims.d), _gain_spec(dims, 0), _mod_spec(dims, tm, 0), _mod_spec(dims, tm, 1),
                  _full_spec(w.shape), _full_spec(wq.shape), _full_spec(wkv.shape),
                  _full_spec(qg.shape), _full_spec(kvg.shape), rs, rs, rs, rs, rs, rs],
        out_specs=[row(wd) for wd in out_w],
        out_shape=[jax.ShapeDtypeStruct((dims.t_all, wd), BF) for wd in out_w],
        compiler_params=_cparams(("arbitrary",), 48),
    )(x, gains3, mods3, mods3, w, wq, wkv, qg, kvg, *rope32[:3], *rope64[:3])


def _softmax_pv(streams, k_refs, v_refs):
    pieces = []
    for kr, vr in zip(k_refs, v_refs):
        n = kr.shape[0]
        if pieces and n <= KEY_CHUNK // 2:
            pieces[-1].append((kr, vr, 0, n))
        else:
            pieces.extend([[(kr, vr, c0, min(c0 + KEY_CHUNK, n))] for c0 in range(0, n, KEY_CHUNK)])
    tasks = [(si, q, cols_k, cols_v, pc) for si, (q, cols_k, cols_v) in enumerate(streams) for pc in pieces]

    def score(t):
        parts = [_dot_nt(t[1], kr[c0:c1, t[2]]) for kr, _, c0, c1 in t[4]]
        return parts[0] if len(parts) == 1 else jnp.concatenate(parts, axis=1)

    state = [None] * len(streams)
    s_next = score(tasks[0])
    for ti, t in enumerate(tasks):
        s = s_next
        if ti + 1 < len(tasks):
            s_next = score(tasks[ti + 1])
        si, _, _, cols_v, pc = t
        mc = jnp.max(s, axis=-1, keepdims=True)
        if state[si] is None:
            m_new = mc
        else:
            m, l, acc = state[si]
            m_new = jnp.maximum(m, mc)
        p = jnp.exp2(s - m_new)
        ps = jnp.sum(p, axis=-1, keepdims=True)
        pb = p.astype(BF)
        pv, off = None, 0
        for _, vr, c0, c1 in pc:
            term = _dot(pb[:, off:off + c1 - c0], vr[c0:c1, cols_v])
            pv = term if pv is None else pv + term
            off += c1 - c0
        if state[si] is None:
            state[si] = (m_new, ps, pv)
        else:
            a = jnp.exp2(m - m_new)
            state[si] = (m_new, a * l + ps, a * acc + pv)
    return [acc / l for _, l, acc in state]


def _mla_attn_kernel(q_ref, *refs, n_seg):
    k_refs, v_refs, o_ref = refs[:n_seg], refs[n_seg:2 * n_seg], refs[2 * n_seg]
    cols = [slice(sub * LANES, (sub + 1) * LANES) for sub in range(2)]
    outs = _softmax_pv([(q_ref[:, c], c, slice(None)) for c in cols], k_refs, v_refs)
    lo = lax.broadcasted_iota(jnp.int32, outs[0].shape, 1) < MLA_V
    o_ref[...] = jnp.where(lo, outs[0], outs[1]).astype(BF)


def _diff_attn_kernel(lam_ref, sg_ref, q_ref, *refs, n_seg, lam_init):
    k_refs, v_refs, o_ref = refs[:n_seg], refs[n_seg:2 * n_seg], refs[2 * n_seg]
    lp = lam_ref[...]
    lam = (jnp.exp(jnp.sum(lp[0:1] * lp[1:2], axis=-1, keepdims=True))
           - jnp.exp(jnp.sum(lp[2:3] * lp[3:4], axis=-1, keepdims=True)) + lam_init)
    lane = lax.broadcasted_iota(jnp.int32, (1, LANES), 1)
    q = q_ref[...]
    full = slice(None)
    o1, o2 = _softmax_pv([(q * (lane < DIFF_DIM).astype(F32).astype(BF), full, full),
                          (q * (lane >= DIFF_DIM).astype(F32).astype(BF), full, full)], k_refs, v_refs)
    o_ref[...] = (_rms(o1 - lam * o2, sg_ref[...]) * (1.0 - lam_init)).astype(BF)


def _full_attn_call(dims, kind, q, k, v, ctx_queries, tq, extra, lam_init):
    if kind == "mla":
        n_h, qw, vw = MLA_HEADS // 2, 2 * LANES, LANES
    else:
        n_h, qw, vw = DIFF_HEADS, LANES, LANES
    ctx_blk0 = dims.t_lat // dims.l
    ctx_seg = (dims.l, lambda b: ctx_blk0 + b)
    if ctx_queries:
        tq, nq, out_rows = dims.l, 1, dims.bn * dims.l
        q_row = lambda b, qi: ctx_blk0 + b
        o_row = lambda b, qi: b
        seg = [ctx_seg]
    else:
        nq, out_rows = dims.s // tq, dims.t_lat
        q_row = o_row = lambda b, qi: b * nq + qi
        seg = [(dims.s, lambda b: b), ctx_seg]
    kv_specs = lambda w: [pl.BlockSpec((rows, w), (lambda f: lambda b, hh, qi: (f(b), hh))(f)) for rows, f in seg]
    if kind == "mla":
        body = functools.partial(_mla_attn_kernel, n_seg=len(seg))
    else:
        body = functools.partial(_diff_attn_kernel, n_seg=len(seg), lam_init=lam_init)
    return pl.pallas_call(
        body,
        grid=(dims.bn, n_h, nq),
        in_specs=([_full_spec(e.shape) for e in extra]
                  + [pl.BlockSpec((tq, qw), lambda b, hh, qi: (q_row(b, qi), hh))] + kv_specs(qw) + kv_specs(vw)),
        out_specs=pl.BlockSpec((tq, vw), lambda b, hh, qi: (o_row(b, qi), hh)),
        out_shape=jax.ShapeDtypeStruct((out_rows, n_h * vw), BF),
        compiler_params=_cparams(("arbitrary", "arbitrary", "arbitrary"), 56),
    )(*extra, q, *([k] * len(seg)), *([v] * len(seg)))


def _full_attn(dims, kind, q, k, v, need_ctx, tq, extra=(), lam_init=0.0):
    lat = _full_attn_call(dims, kind, q, k, v, False, tq, extra, lam_init)
    if not need_ctx:
        return lat
    return jnp.concatenate([lat, _full_attn_call(dims, kind, q, k, v, True, tq, extra, lam_init)], axis=0)


def _route_rows(x, g, shift, scale, wr_ref, f_ref, r_ref, n_exp):
    f = _norm_mod(x, g, shift, scale)
    f_ref[...] = f
    logits = jnp.dot(f, wr_ref[...], preferred_element_type=F32, precision=lax.Precision.HIGHEST)
    lane = lax.broadcasted_iota(jnp.int32, logits.shape, 1).astype(F32)
    logits = jnp.where(lane < n_exp, logits, NEG)
    m1 = jnp.max(logits, axis=-1, keepdims=True)
    i1 = jnp.min(jnp.where(logits == m1, lane, float(LANES)), axis=-1, keepdims=True)
    rest = jnp.where(lane == i1, NEG, logits)
    m2 = jnp.max(rest, axis=-1, keepdims=True)
    i2 = jnp.min(jnp.where(rest == m2, lane, float(LANES)), axis=-1, keepdims=True)
    e2 = jnp.exp(m2 - m1)
    w1 = 1.0 / (1.0 + e2)
    w2 = e2 / (1.0 + e2)
    r_ref[...] = jnp.where(lane == 0, i1, jnp.where(lane == 1, i2, jnp.where(lane == 2, w1,
                           jnp.where(lane == 3, w2, 0.0))))


def _router_kernel(x_ref, g_ref, sh_ref, sc_ref, wr_ref, f_ref, r_ref, *, n_exp):
    _route_rows(x_ref[...], g_ref[0], sh_ref[0], sc_ref[0], wr_ref, f_ref, r_ref, n_exp)


def _router(dims, x, rows, mods3, gains3, w_router_pad, n_exp, tm):
    row = lambda width: pl.BlockSpec((tm, width), lambda i: (i, 0))
    return pl.pallas_call(
        functools.partial(_router_kernel, n_exp=n_exp),
        grid=(rows // tm,),
        in_specs=[row(dims.d), _gain_spec(dims, 2), _mod_spec(dims, tm, 3), _mod_spec(dims, tm, 4),
                  _full_spec(w_router_pad.shape)],
        out_specs=[row(dims.d), row(LANES)],
        out_shape=[jax.ShapeDtypeStruct((rows, dims.d), F32),
                   jax.ShapeDtypeStruct((rows, LANES), F32)],
        compiler_params=_cparams(("arbitrary",), 40),
    )(x, gains3, mods3, mods3, w_router_pad)


def _row_copy(src_hbm, dst, s, d, sem):
    return pltpu.make_async_copy(src_hbm.at[pl.ds(s, 1)], dst.at[pl.ds(d, 1)], sem)


def _expert_kernel(be_ref, nu_ref, src_cur, src_nxt, x_hbm, wg_ref, wu_ref, wd_ref, o_ref,
                   xbuf, sem, h_s, acc_s, *, tm, nj):
    blk, j = pl.program_id(0), pl.program_id(1)
    nblk = pl.num_programs(0)
    n_used = nu_ref[0]
    used = blk < n_used
    slot = blk % 2
    per_step = tm // nj
    head = tm - per_step * nj

    def issue(src_ref, dst_slot, r):
        _row_copy(x_hbm, xbuf.at[dst_slot], src_ref[0, 0, r], r, sem.at[dst_slot]).start()

    def wait_slot(s):
        pltpu.make_async_copy(x_hbm.at[pl.ds(0, tm)], xbuf.at[s], sem.at[s]).wait()

    @pl.when(j == 0)
    def _():
        @pl.when(blk == 0)
        def _():
            def first(r, c):
                issue(src_cur, 0, r)
                return c

            lax.fori_loop(0, tm, first, 0, unroll=8)

        @pl.when(blk <= n_used)
        def _():
            wait_slot(slot)

        @pl.when(used)
        def _():
            h_s[...] = xbuf[slot].astype(BF)
            for r in range(head):
                issue(src_nxt, 1 - slot, r)

        acc_s[...] = jnp.zeros_like(acc_s)

    @pl.when(used)
    def _():
        for r in range(per_step):
            issue(src_nxt, 1 - slot, head + j * per_step + r)
        h = h_s[...]
        act = (_silu(_dot(h, wg_ref[0, 0].astype(BF))) * _dot(h, wu_ref[0, 0].astype(BF))).astype(BF)
        acc_s[...] += _dot(act, wd_ref[0, 0].astype(BF))

    @pl.when(j == nj - 1)
    def _():
        o_ref[...] = acc_s[...]

        @pl.when(jnp.logical_and(blk == nblk - 1, used))
        def _():
            wait_slot(1 - slot)


def _experts(f_in, src, block_e, n_used, w_gu, w_down, layer_i, tm, tn):
    d = f_in.shape[1]
    ff = w_down.shape[2]
    nj = ff // tn
    nblk = src.shape[0] // tm
    jj = lambda blk, j, nu: jnp.where(blk < nu[0], j, nj - 1)
    src3 = src.reshape(nblk, 1, tm)
    return pl.pallas_call(
        functools.partial(_expert_kernel, tm=tm, nj=nj),
        grid_spec=pltpu.PrefetchScalarGridSpec(
            num_scalar_prefetch=2,
            grid=(nblk, nj),
            in_specs=[pl.BlockSpec((1, 1, tm), lambda blk, j, be, nu: (blk, 0, 0), memory_space=pltpu.SMEM),
                      pl.BlockSpec((1, 1, tm), lambda blk, j, be, nu: (jnp.minimum(blk + 1, nblk - 1), 0, 0),
                                   memory_space=pltpu.SMEM),
                      pl.BlockSpec(memory_space=pl.ANY),
                      pl.BlockSpec((1, 1, d, tn), lambda blk, j, be, nu: (layer_i, be[blk], 0, jj(blk, j, nu))),
                      pl.BlockSpec((1, 1, d, tn), lambda blk, j, be, nu: (layer_i, be[blk], 0, nj + jj(blk, j, nu))),
                      pl.BlockSpec((1, 1, tn, d), lambda blk, j, be, nu: (layer_i, be[blk], jj(blk, j, nu), 0))],
            out_specs=pl.BlockSpec((tm, d), lambda blk, j, be, nu: (blk, 0)),
            scratch_shapes=[pltpu.VMEM((2, tm, d), F32), pltpu.SemaphoreType.DMA((2,)),
                            pltpu.VMEM((tm, d), BF), pltpu.VMEM((tm, d), F32)]),
        out_shape=jax.ShapeDtypeStruct((nblk * tm, d), F32),
        compiler_params=_cparams(("arbitrary", "arbitrary"), 56),
    )(block_e, n_used, src3, src3, f_in, w_gu, w_gu, w_down)


def _combine_kernel(dest_ref, r_ref, x_ref, gate_ref, g_ref, y_hbm, o_ref, buf, sem, *, tm):
    def issue(t, c):
        for k in range(TOP_K):
            _row_copy(y_hbm, buf.at[k], dest_ref[0, 0, TOP_K * t + k], t, sem.at[k]).start()
        return c

    lax.fori_loop(0, tm, issue, 0, unroll=4)
    for k in range(TOP_K):
        pltpu.make_async_copy(y_hbm.at[pl.ds(0, tm)], buf.at[k], sem.at[k]).wait()
    r = r_ref[...]
    y = buf[0] * r[:, 2:3] + buf[1] * r[:, 3:4]
    o_ref[...] = x_ref[...] + gate_ref[0] * _rms(y, g_ref[0])


def _combine(dims, y, dest, route, x, mods3, gains3, tm):
    rows = route.shape[0]
    nt = rows // tm
    row = lambda width: pl.BlockSpec((tm, width), lambda i: (i, 0))
    return pl.pallas_call(
        functools.partial(_combine_kernel, tm=tm),
        grid=(nt,),
        in_specs=[pl.BlockSpec((1, 1, TOP_K * tm), lambda i: (i, 0, 0), memory_space=pltpu.SMEM),
                  row(LANES), row(dims.d), _mod_spec(dims, tm, 5), _gain_spec(dims, 3),
                  pl.BlockSpec(memory_space=pl.ANY)],
        out_specs=row(dims.d),
        out_shape=jax.ShapeDtypeStruct((rows, dims.d), F32),
        scratch_shapes=[pltpu.VMEM((TOP_K, tm, dims.d), F32), pltpu.SemaphoreType.DMA((TOP_K,))],
        input_output_aliases={2: 0} if rows == x.shape[0] else {},
        compiler_params=_cparams(("arbitrary",)),
    )(dest.reshape(nt, 1, TOP_K * tm), route, x, mods3, gains3, y)


def _routing_tables(route, n_exp, tm_e):
    t = route.shape[0]
    a = t * TOP_K
    flat_e = route[:, :TOP_K].astype(jnp.int32).reshape(a)
    onehot = (flat_e[:, None] == jnp.arange(n_exp)[None, :]).astype(jnp.int32)
    csum = jnp.cumsum(onehot, axis=0)
    rank = jnp.take_along_axis(csum, flat_e[:, None], axis=1)[:, 0] - 1
    counts = csum[-1]
    padded = (counts + tm_e - 1) // tm_e * tm_e
    ends = jnp.cumsum(padded)
    dest = (ends - padded)[flat_e] + rank
    n_blocks = -(-(a + n_exp * (tm_e - 1)) // tm_e)
    src = jnp.zeros((n_blocks * tm_e,), jnp.int32).at[dest].set(jnp.arange(a, dtype=jnp.int32) // TOP_K)
    block_start = jnp.arange(n_blocks, dtype=jnp.int32) * tm_e
    block_e = jnp.minimum(jnp.sum((ends[None, :] <= block_start[:, None]).astype(jnp.int32), axis=1), n_exp - 1)
    n_used = (ends[-1] // tm_e).reshape(1)
    return dest.astype(jnp.int32), src, block_e.astype(jnp.int32), n_used.astype(jnp.int32)


def _even_weights(w_in):
    kw = SWA_KV_HEADS * SWA_HEAD_DIM
    c0 = EVEN_BCU + EVEN_Q
    wk, wv = w_in[:, c0:c0 + kw], w_in[:, c0 + kw:c0 + 2 * kw]
    dup = lambda w: jnp.concatenate(
        [w[:, g * SWA_HEAD_DIM:(g + 1) * SWA_HEAD_DIM] for g in range(SWA_KV_HEADS) for _ in range(2)], axis=1)
    return jnp.concatenate([w_in[:, :c0], dup(wk), dup(wv)], axis=1).astype(BF)


def _odd_weights(w_in, w_q_up, w_kv_up):
    d = w_in.shape[0]
    c = 0
    mq = w_in[:, c:c + MLA_Q_RANK]; c += MLA_Q_RANK
    dq = w_in[:, c:c + DIFF_W]; c += DIFF_W
    kvd = w_in[:, c:c + MLA_KV_RANK]; c += MLA_KV_RANK
    kpe = w_in[:, c:c + MLA_ROPE]; c += MLA_ROPE
    dk = w_in[:, c:c + DIFF_W]; c += DIFF_W
    dv = w_in[:, c:]
    pad_tail = LANES - MLA_NOPE - MLA_ROPE
    kpe_chunk = jnp.concatenate([jnp.zeros((d, MLA_NOPE), F32), kpe, jnp.zeros((d, pad_tail), F32)], axis=1)
    w = jnp.concatenate([mq, kvd, kpe_chunk, dq, dk, dv], axis=1).astype(BF)
    qh = w_q_up.reshape(MLA_Q_RANK, MLA_HEADS, MLA_NOPE + MLA_ROPE)
    wq = jnp.pad(qh, ((0, 0), (0, 0), (0, pad_tail))).reshape(MLA_Q_RANK, MLA_QK).astype(BF)
    kvh = w_kv_up.reshape(MLA_KV_RANK, MLA_HEADS, MLA_NOPE + MLA_V)
    wk = jnp.pad(kvh[:, :, :MLA_NOPE], ((0, 0), (0, 0), (0, LANES - MLA_NOPE))).reshape(MLA_KV_RANK, MLA_QK)
    wv = kvh[:, :, MLA_NOPE:].reshape(MLA_KV_RANK, MLA_VW)
    return w, wq, jnp.concatenate([wk, wv], axis=1).astype(BF)


def kernel(x, c, ctx, c_ctx, w_mod, b_mod, norm_g, w_in_even, conv_w, sink, w_out_even, w_in_odd, mla_q_norm_g,
           mla_kv_norm_g, w_q_up, w_kv_up, diff_lambda, diff_subln_g, w_out_odd, w_ff_gu, w_ff_down, w_router,
           w_exp_gu, w_exp_down):
    bn, s, d = x.shape
    l = ctx.shape[1]
    depth = w_mod.shape[0]
    n_exp = w_router.shape[-1]
    dims = _Dims(bn, s, l, d)
    assert bn < MOD_ROWS and s % l == 0 and l % BLOCK == 0 and s % GRID_W == 0

    tm = _tile(math.gcd(s, bn * l), 512)
    tm_e = 1024 if dims.t_all >= 8192 else 256
    tq = _tile(s, 1024)

    cc = jnp.zeros((MOD_ROWS, d), F32).at[:bn].set(c).at[bn].set(c_ctx)
    mods = _modulation(cc, w_mod, b_mod)
    rope64 = _rope_tables(s, tm, SWA_HEAD_DIM, SWA_HEAD_DIM, 0)
    rope32 = _rope_tables(s, tm, MLA_ROPE, LANES, MLA_NOPE)

    xa = None
    for layer in range(depth):
        i = layer // 2
        mods3 = mods[layer].reshape(MOD_ROWS, 1, N_MOD * d)
        gains3 = norm_g[layer].reshape(4, 1, d)
        if layer % 2 == 0:
            xs = (x.reshape(bn * s, d), ctx.reshape(bn * l, d)) if xa is None else (xa,)
            bcu, q, k, v, *merged = _even_proj(dims, xs, mods3, gains3, _even_weights(w_in_even[i]), rope64, tm)
            xa = merged[0] if merged else xa
            conv = _gated_conv(dims, bcu, conv_w[i])
            attn = _win_attn(dims, q, k, v, sink[i])
            wo = w_out_even[i].astype(BF)
            xa = _out_proj(dims, conv, attn, wo[:CONV_WIDTH], wo[CONV_WIDTH:], xa, mods3, gains3, tm)
            act_lo, act_hi = _ffn_up(dims, xa, mods3, gains3, w_ff_gu[i].astype(BF), tm)
            wd = w_ff_down[i].astype(BF)
            xa = _out_proj(dims, act_lo, act_hi, wd[:act_lo.shape[1]], wd[act_lo.shape[1]:], xa, mods3, gains3, tm,
                           gate_col=5, gain_row=3)
        else:
            w, wq, wkv = _odd_weights(w_in_odd[i], w_q_up[i], w_kv_up[i])
            qm, km, vm, dq, dk, dv = _odd_proj(dims, xa, mods3, gains3, w, wq, wkv,
                                               mla_q_norm_g[i].reshape(1, -1), mla_kv_norm_g[i].reshape(1, -1),
                                               rope32, rope64, tm)
            lam_init = 0.8 - 0.6 * math.exp(-0.3 * layer)
            extra = (diff_lambda[i], diff_subln_g[i].reshape(1, -1))
            need_ctx = layer < depth - 1
            o_m = _full_attn(dims, "mla", qm, km, vm, need_ctx, tq)
            o_d = _full_attn(dims, "diff", dq, dk, dv, need_ctx, tq, extra, lam_init)
            wo = w_out_odd[i].astype(BF)
            xa = _out_proj(dims, o_m, o_d, wo[:MLA_VW], wo[MLA_VW:], xa, mods3, gains3, tm)
            wr = jnp.pad(w_router[i], ((0, 0), (0, LANES - n_exp)))
            f_in, route = _router(dims, xa, o_m.shape[0], mods3, gains3, wr, n_exp, tm)
            dest, src, block_e, n_used = _routing_tables(route, n_exp, tm_e)
            y = _experts(f_in, src, block_e, n_used, w_exp_gu, w_exp_down, i, tm_e,
                         _tile(w_exp_down.shape[2], 512))
            xa = _combine(dims, y, dest, route, xa, mods3, gains3, tm)
    return xa[:bn * s].reshape(bn, s, d)
```

```python
import functools
import math

import jax
import jax.numpy as jnp
from jax import lax
from jax.experimental import pallas as pl
from jax.experimental.pallas import tpu as pltpu

F32 = jnp.float32
BF = jnp.bfloat16

EPS = 1e-6
ROPE_THETA = 10000.0
GRID_W = 64
BLOCK = 128
N_MOD = 6
LANES = 128
MXU_N = 256
MOD_ROWS = 16

CONV_WIDTH = 512
SWA_HEADS = 8
SWA_KV_HEADS = 2
SWA_HEAD_DIM = 64
MLA_HEADS = 8
MLA_Q_RANK = 256
MLA_KV_RANK = 128
MLA_NOPE = 64
MLA_ROPE = 32
MLA_V = 64
DIFF_HEADS = 4
DIFF_DIM = 64
DIFF_V = 2 * DIFF_DIM
TOP_K = 2

NEG = -1e30
LOG2E = math.log2(math.e)
KEY_CHUNK = 1024


def _tile(n, pref):
    if n <= pref:
        return n
    t = pref - pref % LANES
    while t >= LANES:
        if n % t == 0:
            return t
        t -= LANES
    raise ValueError((n, pref))


def _cparams(sem, vmem_mb=None):
    kw = dict(dimension_semantics=sem)
    if vmem_mb is not None:
        kw["vmem_limit_bytes"] = vmem_mb << 20
    return pltpu.CompilerParams(**kw)


def _dot(a, b):
    return jnp.dot(a, b, preferred_element_type=F32)


def _dot_nt(a, b):
    return lax.dot_general(a, b, (((1,), (1,)), ((), ())), preferred_element_type=F32)


def _rms(x, g):
    return x * lax.rsqrt(jnp.mean(x * x, axis=-1, keepdims=True) + EPS) * g


def _norm_mod(x, g, shift, scale):
    return _rms(x, g) * (1.0 + scale) + shift


def _silu(x):
    return x / (1.0 + jnp.exp(-x))


def _rope(x, cos, sa, sb, half):
    out = []
    for c in range(x.shape[1] // LANES):
        xc = x[:, c * LANES:(c + 1) * LANES]
        out.append(xc * cos + pltpu.roll(xc, LANES - half, 1) * sa + pltpu.roll(xc, half, 1) * sb)
    return out[0] if len(out) == 1 else jnp.concatenate(out, axis=1)


def _mod_kernel(cc_ref, w_ref, b_ref, o_ref):
    a = _silu(cc_ref[...]).astype(BF)
    o_ref[0] = _dot(a, w_ref[0].astype(BF)) + b_ref[0]


def _modulation(cc, w_mod, b_mod):
    depth, d, n = w_mod.shape
    tn = _tile(n, 1536)
    return pl.pallas_call(
        _mod_kernel,
        grid=(depth, n // tn),
        in_specs=[
            pl.BlockSpec((MOD_ROWS, d), lambda l, j: (0, 0)),
            pl.BlockSpec((1, d, tn), lambda l, j: (l, 0, j)),
            pl.BlockSpec((1, 1, tn), lambda l, j: (l, 0, j)),
        ],
        out_specs=pl.BlockSpec((1, MOD_ROWS, tn), lambda l, j: (l, 0, j)),
        out_shape=jax.ShapeDtypeStruct((depth, MOD_ROWS, n), F32),
        compiler_params=_cparams(("arbitrary", "arbitrary"), 40),
    )(cc, w_mod, b_mod.reshape(depth, 1, n))


class _Dims:
    def __init__(self, bn, s, l, d):
        self.bn, self.s, self.l, self.d = bn, s, l, d
        self.t_lat = bn * s
        self.t_all = bn * s + bn * l

    def mod_row(self, i, tm):
        return jnp.minimum(i * tm // self.s, self.bn)


def _mod_spec(dims, tm, col):
    return pl.BlockSpec((1, 1, dims.d), lambda i, *_: (dims.mod_row(i, tm), 0, col))


def _gain_spec(dims, k):
    return pl.BlockSpec((1, 1, dims.d), lambda i, *_: (k, 0, 0))


def _rope_spec(dims, tm):
    n_lat, per_seq = dims.t_lat // tm, dims.s // tm
    return pl.BlockSpec((tm, LANES), lambda i: (jnp.where(i < n_lat, i % per_seq, per_seq), 0))


def _full_spec(shape):
    return pl.BlockSpec(shape, lambda *_: (0,) * len(shape))


def _rope_tables(s, pad_rows, rot_dim, period, lane_off):
    axis_dim = rot_dim // 2
    half = axis_dim // 2
    inv = 1.0 / (ROPE_THETA ** (jnp.arange(0, axis_dim, 2, dtype=F32) / axis_dim))
    pos = jnp.arange(s)
    rows = (pos // GRID_W).astype(F32)[:, None]
    cols = (pos % GRID_W).astype(F32)[:, None]
    lane = jnp.arange(LANES)
    dd = lane % period - lane_off
    active = (dd >= 0) & (dd < rot_dim)
    dd = jnp.clip(dd, 0, rot_dim - 1)
    j = dd % axis_dim
    ang = jnp.where((dd // axis_dim == 0)[None, :], rows, cols) * inv[j % half][None, :]
    first = (j < half)[None, :]
    act = active[None, :]
    cos = jnp.where(act, jnp.cos(ang), 1.0)
    sin = jnp.where(act, jnp.sin(ang), 0.0)
    sa = jnp.where(first, -sin, 0.0)
    sb = jnp.where(first, 0.0, sin)
    pad = lambda t, v: jnp.concatenate([t, jnp.full((pad_rows, LANES), v, F32)], axis=0)
    return pad(cos, 1.0), pad(sa, 0.0), pad(sb, 0.0), half


EVEN_BCU = 3 * CONV_WIDTH
EVEN_Q = SWA_HEADS * SWA_HEAD_DIM
EVEN_KD = SWA_KV_HEADS * LANES


def _even_proj_kernel(*refs, half, q_scale, n_lat_tiles):
    if n_lat_tiles is None:
        x_ref, g_ref, sh_ref, sc_ref, w_ref, cos_ref, sa_ref, sb_ref, bcu_ref, q_ref, k_ref, v_ref = refs
        x = x_ref[...]
    else:
        (xl_ref, xc_ref, g_ref, sh_ref, sc_ref, w_ref, cos_ref, sa_ref, sb_ref,
         bcu_ref, q_ref, k_ref, v_ref, xa_ref) = refs
        x = jnp.where(pl.program_id(0) < n_lat_tiles, xl_ref[...], xc_ref[...])
        xa_ref[...] = x
    h = _norm_mod(x, g_ref[0], sh_ref[0], sc_ref[0]).astype(BF)
    c0, c1, c2 = EVEN_BCU, EVEN_BCU + EVEN_Q, EVEN_BCU + EVEN_Q + EVEN_KD
    bcu_ref[...] = _dot(h, w_ref[:, :c0]).astype(BF)
    qk = _rope(_dot(h, w_ref[:, c0:c2]), cos_ref[...], sa_ref[...], sb_ref[...], half)
    q_ref[...] = (qk[:, :EVEN_Q] * q_scale).astype(BF)
    k_ref[...] = qk[:, EVEN_Q:].astype(BF)
    v_ref[...] = _dot(h, w_ref[:, c2:]).astype(BF)


def _even_proj(dims, xs, mods3, gains3, w, rope, tm):
    n = w.shape[1]
    cos, sa, sb, half = rope
    row = lambda width: pl.BlockSpec((tm, width), lambda i: (i, 0))
    out_w = (EVEN_BCU, EVEN_Q, EVEN_KD, EVEN_KD)
    out_specs = [row(wd) for wd in out_w]
    out_shape = [jax.ShapeDtypeStruct((dims.t_all, wd), BF) for wd in out_w]
    if len(xs) == 1:
        n_lat, x_specs = None, [row(dims.d)]
    else:
        n_lat = dims.t_lat // tm
        x_specs = [pl.BlockSpec((tm, dims.d), lambda i: (jnp.minimum(i, n_lat - 1), 0)),
                   pl.BlockSpec((tm, dims.d), lambda i: (jnp.maximum(i - n_lat, 0), 0))]
        out_specs.append(row(dims.d))
        out_shape.append(jax.ShapeDtypeStruct((dims.t_all, dims.d), F32))
    return pl.pallas_call(
        functools.partial(_even_proj_kernel, half=half, q_scale=SWA_HEAD_DIM ** -0.5 * LOG2E, n_lat_tiles=n_lat),
        grid=(dims.t_all // tm,),
        in_specs=x_specs + [_gain_spec(dims, 0), _mod_spec(dims, tm, 0), _mod_spec(dims, tm, 1),
                            _full_spec((dims.d, n)), _rope_spec(dims, tm), _rope_spec(dims, tm),
                            _rope_spec(dims, tm)],
        out_specs=out_specs,
        out_shape=out_shape,
        compiler_params=_cparams(("arbitrary",), 48),
    )(*xs, gains3, mods3, mods3, w, cos, sa, sb)


def _conv_kernel(b_ref, c_ref, u_ref, cp_ref, up_ref, cn_ref, un_ref, w_ref, o_ref, *, n_lat_blocks, per_seq):
    i = pl.program_id(0)
    rows = b_ref.shape[0]
    hr = cp_ref.shape[0]
    pos = i % per_seq
    is_lat = i < n_lat_blocks
    has_prev = jnp.logical_and(is_lat, pos > 0).astype(F32)
    has_next = jnp.logical_and(is_lat, pos < per_seq - 1).astype(F32)
    cu = c_ref[...].astype(F32) * u_ref[...].astype(F32)
    cu_p = (cp_ref[...].astype(F32) * up_ref[...].astype(F32))[hr - 1:hr] * has_prev
    cu_n = (cn_ref[...].astype(F32) * un_ref[...].astype(F32))[0:1] * has_next
    r = lax.broadcasted_iota(jnp.int32, cu.shape, 0)
    prev = jnp.where(r == 0, cu_p, pltpu.roll(cu, 1, 0))
    nxt = jnp.where(r == rows - 1, cu_n, pltpu.roll(cu, rows - 1, 0))
    w = w_ref[...]
    o_ref[...] = (b_ref[...].astype(F32) * (prev * w[0:1] + cu * w[1:2] + nxt * w[2:3])).astype(BF)


def _gated_conv(dims, bcu, conv_w):
    rows = dims.l
    hr = 16
    nblk = dims.t_all // rows
    per = rows // hr
    last = dims.t_all // hr - 1
    cw = CONV_WIDTH
    main = lambda col: pl.BlockSpec((rows, cw), lambda i: (i, col))
    prev = lambda col: pl.BlockSpec((hr, cw), lambda i: (jnp.maximum(i * per - 1, 0), col))
    nxt = lambda col: pl.BlockSpec((hr, cw), lambda i: (jnp.minimum((i + 1) * per, last), col))
    return pl.pallas_call(
        functools.partial(_conv_kernel, n_lat_blocks=dims.t_lat // rows, per_seq=dims.s // rows),
        grid=(nblk,),
        in_specs=[main(0), main(1), main(2), prev(1), prev(2), nxt(1), nxt(2), _full_spec(conv_w.shape)],
        out_specs=pl.BlockSpec((rows, cw), lambda i: (i, 0)),
        out_shape=jax.ShapeDtypeStruct((dims.t_all, cw), BF),
        compiler_params=_cparams(("arbitrary",)),
    )(bcu, bcu, bcu, bcu, bcu, bcu, bcu, conv_w)


def _win_attn_kernel(sink_ref, q_ref, kp_ref, ko_ref, kn_ref, kc_ref, vp_ref, vo_ref, vn_ref, vc_ref,
                     o_ref, *, nb):
    n = pl.program_id(1)
    blk = q_ref.shape[0]
    n_ctx = kc_ref.shape[0]
    is_lat = n < nb
    lo_s = jnp.where(is_lat, jnp.where(n >= 1, 0, blk), 3 * blk)
    hi_s = jnp.where(is_lat, jnp.where(n + 1 < nb, 3 * blk, 2 * blk), 0)
    shape = (blk, 3 * blk + n_ctx)
    c = lax.broadcasted_iota(jnp.int32, shape, 1)
    r = lax.broadcasted_iota(jnp.int32, shape, 0)
    ok = (c >= 3 * blk) | ((c >= r) & (c - 2 * blk <= r) & (c >= lo_s) & (c < hi_s))
    bias = jnp.where(ok, 0.0, NEG)
    kall = jnp.concatenate([kp_ref[...], ko_ref[...], kn_ref[...], kc_ref[...]], axis=0)
    vall = jnp.concatenate([vp_ref[...], vo_ref[...], vn_ref[...], vc_ref[...]], axis=0)
    lane = lax.broadcasted_iota(jnp.int32, (1, LANES), 1)
    keep = ((lane < LANES // 2).astype(F32).astype(BF), (lane >= LANES // 2).astype(F32).astype(BF))
    lo = lax.broadcasted_iota(jnp.int32, (blk, LANES), 1) < LANES // 2
    per_group = SWA_HEADS // SWA_KV_HEADS

    def score(head):
        pair, sub, g = head // 2, head % 2, head // per_group
        q2 = q_ref[:, pair * LANES:(pair + 1) * LANES]
        return _dot_nt(q2 * keep[sub], kall[:, g * LANES:(g + 1) * LANES]) + bias

    s_next = score(0)
    outs = []
    for head in range(SWA_HEADS):
        s = s_next
        if head + 1 < SWA_HEADS:
            s_next = score(head + 1)
        g = head // per_group
        sk = jnp.full((blk, 1), sink_ref[head], F32) * LOG2E
        m = jnp.maximum(jnp.max(s, axis=-1, keepdims=True), sk)
        p = jnp.exp2(s - m)
        den = jnp.sum(p, axis=-1, keepdims=True) + jnp.exp2(sk - m)
        outs.append(_dot(p.astype(BF), vall[:, g * LANES:(g + 1) * LANES]) / den)
        if head % 2 == 1:
            pair = head // 2
            o_ref[:, pair * LANES:(pair + 1) * LANES] = jnp.where(lo, outs[-2], outs[-1]).astype(BF)


def _win_attn(dims, q, k, v, sink):
    nb = dims.s // BLOCK
    nc = dims.l // BLOCK
    lat_blocks = dims.t_lat // BLOCK
    ctx_blk0 = dims.t_lat // dims.l

    def own(b, n):
        return jnp.where(n < nb, b * nb + n, lat_blocks + b * nc + (n - nb))

    def prev(b, n):
        return jnp.where(n < nb, b * nb + jnp.maximum(n - 1, 0), own(b, n))

    def nxt(b, n):
        return jnp.where(n < nb, b * nb + jnp.minimum(n + 1, nb - 1), own(b, n))

    kv = lambda f: pl.BlockSpec((BLOCK, EVEN_KD), lambda b, n: (f(b, n), 0))
    ctx = pl.BlockSpec((dims.l, EVEN_KD), lambda b, n: (ctx_blk0 + b, 0))
    qo = pl.BlockSpec((BLOCK, EVEN_Q), lambda b, n: (own(b, n), 0))
    return pl.pallas_call(
        functools.partial(_win_attn_kernel, nb=nb),
        grid=(dims.bn, nb + nc),
        in_specs=[pl.BlockSpec(memory_space=pltpu.SMEM), qo, kv(prev), kv(own), kv(nxt), ctx,
                  kv(prev), kv(own), kv(nxt), ctx],
        out_specs=qo,
        out_shape=jax.ShapeDtypeStruct((dims.t_all, EVEN_Q), BF),
        compiler_params=_cparams(("arbitrary", "arbitrary")),
    )(sink, q, k, k, k, k, v, v, v, v)


def _out_proj_kernel(a1_ref, a2_ref, w1_ref, w2_ref, x_ref, gate_ref, g_ref, o_ref):
    y = _dot(a1_ref[...], w1_ref[...]) + _dot(a2_ref[...], w2_ref[...])
    o_ref[...] = x_ref[...] + gate_ref[0] * _rms(y, g_ref[0])


def _out_proj(dims, a1, a2, w1, w2, x, mods3, gains3, tm, gate_col=2, gain_row=1):
    row = lambda width: pl.BlockSpec((tm, width), lambda i: (i, 0))
    return pl.pallas_call(
        _out_proj_kernel,
        grid=(a1.shape[0] // tm,),
        in_specs=[row(a1.shape[1]), row(a2.shape[1]), _full_spec(w1.shape), _full_spec(w2.shape),
                  row(dims.d), _mod_spec(dims, tm, gate_col), _gain_spec(dims, gain_row)],
        out_specs=row(dims.d),
        out_shape=jax.ShapeDtypeStruct(x.shape, F32),
        input_output_aliases={4: 0},
        compiler_params=_cparams(("arbitrary",), 40),
    )(a1, a2, w1, w2, x, mods3, gains3)


def _ffn_up_kernel(x_ref, g_ref, sh_ref, sc_ref, w_ref, lo_ref, hi_ref, *, ff):
    h = _norm_mod(x_ref[...], g_ref[0], sh_ref[0], sc_ref[0]).astype(BF)
    c0 = 0
    for ref in (lo_ref, hi_ref):
        c1 = c0 + ref.shape[1]
        ref[...] = (_silu(_dot(h, w_ref[:, c0:c1])) * _dot(h, w_ref[:, ff + c0:ff + c1])).astype(BF)
        c0 = c1


def _ffn_split(ff):
    lo = (ff // 2 + MXU_N - 1) // MXU_N * MXU_N
    return (lo, ff - lo) if 0 < lo < ff else (ff // 2, ff - ff // 2)


def _ffn_up(dims, x, mods3, gains3, w_gu, tm):
    ff = w_gu.shape[1] // 2
    row = lambda width: pl.BlockSpec((tm, width), lambda i: (i, 0))
    widths = _ffn_split(ff)
    return pl.pallas_call(
        functools.partial(_ffn_up_kernel, ff=ff),
        grid=(dims.t_all // tm,),
        in_specs=[row(dims.d), _gain_spec(dims, 2), _mod_spec(dims, tm, 3), _mod_spec(dims, tm, 4),
                  _full_spec(w_gu.shape)],
        out_specs=[row(wd) for wd in widths],
        out_shape=[jax.ShapeDtypeStruct((dims.t_all, wd), BF) for wd in widths],
        compiler_params=_cparams(("arbitrary",), 56),
    )(x, gains3, mods3, mods3, w_gu)


ODD_STAGE1 = MLA_Q_RANK + MLA_KV_RANK + LANES
MLA_QK = MLA_HEADS * LANES
MLA_VW = MLA_HEADS * MLA_V
DIFF_W = DIFF_HEADS * 2 * DIFF_DIM


def _odd_proj_kernel(x_ref, g_ref, sh_ref, sc_ref, w_ref, wq_ref, wkv_ref, qg_ref, kvg_ref,
                     c32_ref, a32_ref, b32_ref, c64_ref, a64_ref, b64_ref,
                     q_ref, k_ref, v_ref, dq_ref, dk_ref, dv_ref, *, half32, half64, mla_scale, diff_scale):
    h = _norm_mod(x_ref[...], g_ref[0], sh_ref[0], sc_ref[0]).astype(BF)
    r32 = (c32_ref[...], a32_ref[...], b32_ref[...], half32)
    r64 = (c64_ref[...], a64_ref[...], b64_ref[...], half64)
    s1 = _dot(h, w_ref[:, :ODD_STAGE1])
    qn = _rms(s1[:, :MLA_Q_RANK], qg_ref[...]).astype(BF)
    kvn = _rms(s1[:, MLA_Q_RANK:MLA_Q_RANK + MLA_KV_RANK], kvg_ref[...]).astype(BF)
    kpe = _rope(s1[:, MLA_Q_RANK + MLA_KV_RANK:], *r32)
    q_ref[...] = (_rope(_dot(qn, wq_ref[...]), *r32) * mla_scale).astype(BF)
    kn = _dot(kvn, wkv_ref[:, :MLA_QK])
    k_ref[...] = (kn + jnp.concatenate([kpe] * MLA_HEADS, axis=1)).astype(BF)
    v_ref[...] = _dot(kvn, wkv_ref[:, MLA_QK:]).astype(BF)
    c0 = ODD_STAGE1
    dq_ref[...] = (_rope(_dot(h, w_ref[:, c0:c0 + DIFF_W]), *r64) * diff_scale).astype(BF)
    dk_ref[...] = _rope(_dot(h, w_ref[:, c0 + DIFF_W:c0 + 2 * DIFF_W]), *r64).astype(BF)
    dv_ref[...] = _dot(h, w_ref[:, c0 + 2 * DIFF_W:]).astype(BF)


def _odd_proj(dims, x, mods3, gains3, w, wq, wkv, qg, kvg, rope32, rope64, tm):
    row = lambda width: pl.BlockSpec((tm, width), lambda i: (i, 0))
    out_w = (MLA_QK, MLA_QK, MLA_VW, DIFF_W, DIFF_W, DIFF_HEADS * DIFF_V)
    rs = _rope_spec(dims, tm)
    return pl.pallas_call(
        functools.partial(_odd_proj_kernel, half32=rope32[3], half64=rope64[3],
                          mla_scale=(MLA_NOPE + MLA_ROPE) ** -0.5 * LOG2E, diff_scale=DIFF_DIM ** -0.5 * LOG2E),
        grid=(dims.t_all // tm,),
        in_specs=[row(dims.d), _gain_spec(dims, 0), _mod_spec(dims, tm, 0), _mod_spec(dims, tm, 1),
                  _full_spec(w.shape), _full_spec(wq.shape), _full_spec(wkv.shape),
                  _full_spec(qg.shape), _full_spec(kvg.shape), rs, rs, rs, rs, rs, rs],
        out_specs=[row(wd) for wd in out_w],
        out_shape=[jax.ShapeDtypeStruct((dims.t_all, wd), BF) for wd in out_w],
        compiler_params=_cparams(("arbitrary",), 48),
    )(x, gains3, mods3, mods3, w, wq, wkv, qg, kvg, *rope32[:3], *rope64[:3])


def _softmax_pv(streams, k_refs, v_refs):
    pieces = []
    for kr, vr in zip(k_refs, v_refs):
        n = kr.shape[0]
        if pieces and n <= KEY_CHUNK // 2:
            pieces[-1].append((kr, vr, 0, n))
        else:
            pieces.extend([[(kr, vr, c0, min(c0 + KEY_CHUNK, n))] for c0 in range(0, n, KEY_CHUNK)])
    tasks = [(si, q, cols_k, cols_v, pc) for si, (q, cols_k, cols_v) in enumerate(streams) for pc in pieces]

    def score(t):
        parts = [_dot_nt(t[1], kr[c0:c1, t[2]]) for kr, _, c0, c1 in t[4]]
        return parts[0] if len(parts) == 1 else jnp.concatenate(parts, axis=1)

    state = [None] * len(streams)
    s_next = score(tasks[0])
    for ti, t in enumerate(tasks):
        s = s_next
        if ti + 1 < len(tasks):
            s_next = score(tasks[ti + 1])
        si, _, _, cols_v, pc = t
        mc = jnp.max(s, axis=-1, keepdims=True)
        if state[si] is None:
            m_new = mc
        else:
            m, l, acc = state[si]
            m_new = jnp.maximum(m, mc)
        p = jnp.exp2(s - m_new)
        ps = jnp.sum(p, axis=-1, keepdims=True)
        pb = p.astype(BF)
        pv, off = None, 0
        for _, vr, c0, c1 in pc:
            term = _dot(pb[:, off:off + c1 - c0], vr[c0:c1, cols_v])
            pv = term if pv is None else pv + term
            off += c1 - c0
        if state[si] is None:
            state[si] = (m_new, ps, pv)
        else:
            a = jnp.exp2(m - m_new)
            state[si] = (m_new, a * l + ps, a * acc + pv)
    return [acc / l for _, l, acc in state]


def _mla_attn_kernel(q_ref, *refs, n_seg):
    k_refs, v_refs, o_ref = refs[:n_seg], refs[n_seg:2 * n_seg], refs[2 * n_seg]
    cols = [slice(sub * LANES, (sub + 1) * LANES) for sub in range(2)]
    outs = _softmax_pv([(q_ref[:, c], c, slice(None)) for c in cols], k_refs, v_refs)
    lo = lax.broadcasted_iota(jnp.int32, outs[0].shape, 1) < MLA_V
    o_ref[...] = jnp.where(lo, outs[0], outs[1]).astype(BF)


def _diff_attn_kernel(lam_ref, sg_ref, q_ref, *refs, n_seg, lam_init):
    k_refs, v_refs, o_ref = refs[:n_seg], refs[n_seg:2 * n_seg], refs[2 * n_seg]
    lp = lam_ref[...]
    lam = (jnp.exp(jnp.sum(lp[0:1] * lp[1:2], axis=-1, keepdims=True))
           - jnp.exp(jnp.sum(lp[2:3] * lp[3:4], axis=-1, keepdims=True)) + lam_init)
    lane = lax.broadcasted_iota(jnp.int32, (1, LANES), 1)
    q = q_ref[...]
    full = slice(None)
    o1, o2 = _softmax_pv([(q * (lane < DIFF_DIM).astype(F32).astype(BF), full, full),
                          (q * (lane >= DIFF_DIM).astype(F32).astype(BF), full, full)], k_refs, v_refs)
    o_ref[...] = (_rms(o1 - lam * o2, sg_ref[...]) * (1.0 - lam_init)).astype(BF)


def _full_attn_call(dims, kind, q, k, v, ctx_queries, tq, extra, lam_init):
    if kind == "mla":
        n_h, qw, vw = MLA_HEADS // 2, 2 * LANES, LANES
    else:
        n_h, qw, vw = DIFF_HEADS, LANES, LANES
    ctx_blk0 = dims.t_lat // dims.l
    ctx_seg = (dims.l, lambda b: ctx_blk0 + b)
    if ctx_queries:
        tq, nq, out_rows = dims.l, 1, dims.bn * dims.l
        q_row = lambda b, qi: ctx_blk0 + b
        o_row = lambda b, qi: b
        seg = [ctx_seg]
    else:
        nq, out_rows = dims.s // tq, dims.t_lat
        q_row = o_row = lambda b, qi: b * nq + qi
        seg = [(dims.s, lambda b: b), ctx_seg]
    kv_specs = lambda w: [pl.BlockSpec((rows, w), (lambda f: lambda b, hh, qi: (f(b), hh))(f)) for rows, f in seg]
    if kind == "mla":
        body = functools.partial(_mla_attn_kernel, n_seg=len(seg))
    else:
        body = functools.partial(_diff_attn_kernel, n_seg=len(seg), lam_init=lam_init)
    return pl.pallas_call(
        body,
        grid=(dims.bn, n_h, nq),
        in_specs=([_full_spec(e.shape) for e in extra]
                  + [pl.BlockSpec((tq, qw), lambda b, hh, qi: (q_row(b, qi), hh))] + kv_specs(qw) + kv_specs(vw)),
        out_specs=pl.BlockSpec((tq, vw), lambda b, hh, qi: (o_row(b, qi), hh)),
        out_shape=jax.ShapeDtypeStruct((out_rows, n_h * vw), BF),
        compiler_params=_cparams(("arbitrary", "arbitrary", "arbitrary"), 56),
    )(*extra, q, *([k] * len(seg)), *([v] * len(seg)))


def _full_attn(dims, kind, q, k, v, need_ctx, tq, extra=(), lam_init=0.0):
    lat = _full_attn_call(dims, kind, q, k, v, False, tq, extra, lam_init)
    if not need_ctx:
        return lat
    return jnp.concatenate([lat, _full_attn_call(dims, kind, q, k, v, True, tq, extra, lam_init)], axis=0)


def _route_rows(x, g, shift, scale, wr_ref, f_ref, r_ref, n_exp):
    f = _norm_mod(x, g, shift, scale)
    f_ref[...] = f
    logits = jnp.dot(f, wr_ref[...], preferred_element_type=F32, precision=lax.Precision.HIGHEST)
    lane = lax.broadcasted_iota(jnp.int32, logits.shape, 1).astype(F32)
    logits = jnp.where(lane < n_exp, logits, NEG)
    m1 = jnp.max(logits, axis=-1, keepdims=True)
    i1 = jnp.min(jnp.where(logits == m1, lane, float(LANES)), axis=-1, keepdims=True)
    rest = jnp.where(lane == i1, NEG, logits)
    m2 = jnp.max(rest, axis=-1, keepdims=True)
    i2 = jnp.min(jnp.where(rest == m2, lane, float(LANES)), axis=-1, keepdims=True)
    e2 = jnp.exp(m2 - m1)
    w1 = 1.0 / (1.0 + e2)
    w2 = e2 / (1.0 + e2)
    r_ref[...] = jnp.where(lane == 0, i1, jnp.where(lane == 1, i2, jnp.where(lane == 2, w1,
                           jnp.where(lane == 3, w2, 0.0))))


def _router_kernel(x_ref, g_ref, sh_ref, sc_ref, wr_ref, f_ref, r_ref, *, n_exp):
    _route_rows(x_ref[...], g_ref[0], sh_ref[0], sc_ref[0], wr_ref, f_ref, r_ref, n_exp)


def _router(dims, x, rows, mods3, gains3, w_router_pad, n_exp, tm):
    row = lambda width: pl.BlockSpec((tm, width), lambda i: (i, 0))
    return pl.pallas_call(
        functools.partial(_router_kernel, n_exp=n_exp),
        grid=(rows // tm,),
        in_specs=[row(dims.d), _gain_spec(dims, 2), _mod_spec(dims, tm, 3), _mod_spec(dims, tm, 4),
                  _full_spec(w_router_pad.shape)],
        out_specs=[row(dims.d), row(LANES)],
        out_shape=[jax.ShapeDtypeStruct((rows, dims.d), F32),
                   jax.ShapeDtypeStruct((rows, LANES), F32)],
        compiler_params=_cparams(("arbitrary",), 40),
    )(x, gains3, mods3, mods3, w_router_pad)


def _row_copy(src_hbm, dst, s, d, sem):
    return pltpu.make_async_copy(src_hbm.at[pl.ds(s, 1)], dst.at[pl.ds(d, 1)], sem)


def _expert_kernel(be_ref, nu_ref, src_cur, src_nxt, x_hbm, wg_ref, wu_ref, wd_ref, o_ref,
                   xbuf, sem, h_s, acc_s, *, tm, nj):
    blk, j = pl.program_id(0), pl.program_id(1)
    nblk = pl.num_programs(0)
    n_used = nu_ref[0]
    used = blk < n_used
    slot = blk % 2
    per_step = tm // nj
    head = tm - per_step * nj

    def issue(src_ref, dst_slot, r):
        _row_copy(x_hbm, xbuf.at[dst_slot], src_ref[0, 0, r], r, sem.at[dst_slot]).start()

    def wait_slot(s):
        pltpu.make_async_copy(x_hbm.at[pl.ds(0, tm)], xbuf.at[s], sem.at[s]).wait()

    @pl.when(j == 0)
    def _():
        @pl.when(blk == 0)
        def _():
            def first(r, c):
                issue(src_cur, 0, r)
                return c

            lax.fori_loop(0, tm, first, 0, unroll=8)

        @pl.when(blk <= n_used)
        def _():
            wait_slot(slot)

        @pl.when(used)
        def _():
            h_s[...] = xbuf[slot].astype(BF)
            for r in range(head):
                issue(src_nxt, 1 - slot, r)

        acc_s[...] = jnp.zeros_like(acc_s)

    @pl.when(used)
    def _():
        for r in range(per_step):
            issue(src_nxt, 1 - slot, head + j * per_step + r)
        h = h_s[...]
        act = (_silu(_dot(h, wg_ref[0, 0].astype(BF))) * _dot(h, wu_ref[0, 0].astype(BF))).astype(BF)
        acc_s[...] += _dot(act, wd_ref[0, 0].astype(BF))

    @pl.when(j == nj - 1)
    def _():
        o_ref[...] = acc_s[...]

        @pl.when(jnp.logical_and(blk == nblk - 1, used))
        def _():
            wait_slot(1 - slot)


def _experts(f_in, src, block_e, n_used, w_gu, w_down, layer_i, tm, tn):
    d = f_in.shape[1]
    ff = w_down.shape[2]
    nj = ff // tn
    nblk = src.shape[0] // tm
    jj = lambda blk, j, nu: jnp.where(blk < nu[0], j, nj - 1)
    src3 = src.reshape(nblk, 1, tm)
    return pl.pallas_call(
        functools.partial(_expert_kernel, tm=tm, nj=nj),
        grid_spec=pltpu.PrefetchScalarGridSpec(
            num_scalar_prefetch=2,
            grid=(nblk, nj),
            in_specs=[pl.BlockSpec((1, 1, tm), lambda blk, j, be, nu: (blk, 0, 0), memory_space=pltpu.SMEM),
                      pl.BlockSpec((1, 1, tm), lambda blk, j, be, nu: (jnp.minimum(blk + 1, nblk - 1), 0, 0),
                                   memory_space=pltpu.SMEM),
                      pl.BlockSpec(memory_space=pl.ANY),
                      pl.BlockSpec((1, 1, d, tn), lambda blk, j, be, nu: (layer_i, be[blk], 0, jj(blk, j, nu))),
                      pl.BlockSpec((1, 1, d, tn), lambda blk, j, be, nu: (layer_i, be[blk], 0, nj + jj(blk, j, nu))),
                      pl.BlockSpec((1, 1, tn, d), lambda blk, j, be, nu: (layer_i, be[blk], jj(blk, j, nu), 0))],
            out_specs=pl.BlockSpec((tm, d), lambda blk, j, be, nu: (blk, 0)),
            scratch_shapes=[pltpu.VMEM((2, tm, d), F32), pltpu.SemaphoreType.DMA((2,)),
                            pltpu.VMEM((tm, d), BF), pltpu.VMEM((tm, d), F32)]),
        out_shape=jax.ShapeDtypeStruct((nblk * tm, d), F32),
        compiler_params=_cparams(("arbitrary", "arbitrary"), 56),
    )(block_e, n_used, src3, src3, f_in, w_gu, w_gu, w_down)


def _combine_kernel(dest_ref, r_ref, x_ref, gate_ref, g_ref, y_hbm, o_ref, buf, sem, *, tm):
    def issue(t, c):
        for k in range(TOP_K):
            _row_copy(y_hbm, buf.at[k], dest_ref[0, 0, TOP_K * t + k], t, sem.at[k]).start()
        return c

    lax.fori_loop(0, tm, issue, 0, unroll=4)
    for k in range(TOP_K):
        pltpu.make_async_copy(y_hbm.at[pl.ds(0, tm)], buf.at[k], sem.at[k]).wait()
    r = r_ref[...]
    y = buf[0] * r[:, 2:3] + buf[1] * r[:, 3:4]
    o_ref[...] = x_ref[...] + gate_ref[0] * _rms(y, g_ref[0])


def _combine(dims, y, dest, route, x, mods3, gains3, tm):
    rows = route.shape[0]
    nt = rows // tm
    row = lambda width: pl.BlockSpec((tm, width), lambda i: (i, 0))
    return pl.pallas_call(
        functools.partial(_combine_kernel, tm=tm),
        grid=(nt,),
        in_specs=[pl.BlockSpec((1, 1, TOP_K * tm), lambda i: (i, 0, 0), memory_space=pltpu.SMEM),
                  row(LANES), row(dims.d), _mod_spec(dims, tm, 5), _gain_spec(dims, 3),
                  pl.BlockSpec(memory_space=pl.ANY)],
        out_specs=row(dims.d),
        out_shape=jax.ShapeDtypeStruct((rows, dims.d), F32),
        scratch_shapes=[pltpu.VMEM((TOP_K, tm, dims.d), F32), pltpu.SemaphoreType.DMA((TOP_K,))],
        input_output_aliases={2: 0} if rows == x.shape[0] else {},
        compiler_params=_cparams(("arbitrary",)),
    )(dest.reshape(nt, 1, TOP_K * tm), route, x, mods3, gains3, y)


def _routing_tables(route, n_exp, tm_e):
    t = route.shape[0]
    a = t * TOP_K
    flat_e = route[:, :TOP_K].astype(jnp.int32).reshape(a)
    onehot = (flat_e[:, None] == jnp.arange(n_exp)[None, :]).astype(jnp.int32)
    csum = jnp.cumsum(onehot, axis=0)
    rank = jnp.take_along_axis(csum, flat_e[:, None], axis=1)[:, 0] - 1
    counts = csum[-1]
    padded = (counts + tm_e - 1) // tm_e * tm_e
    ends = jnp.cumsum(padded)
    dest = (ends - padded)[flat_e] + rank
    n_blocks = -(-(a + n_exp * (tm_e - 1)) // tm_e)
    src = jnp.zeros((n_blocks * tm_e,), jnp.int32).at[dest].set(
        jnp.arange(a, dtype=jnp.int32) // TOP_K, unique_indices=True, mode="promise_in_bounds")
    block_start = jnp.arange(n_blocks, dtype=jnp.int32) * tm_e
    block_e = jnp.minimum(jnp.sum((ends[None, :] <= block_start[:, None]).astype(jnp.int32), axis=1), n_exp - 1)
    n_used = (ends[-1] // tm_e).reshape(1)
    return dest.astype(jnp.int32), src, block_e.astype(jnp.int32), n_used.astype(jnp.int32)


def _even_weights(w_in):
    kw = SWA_KV_HEADS * SWA_HEAD_DIM
    c0 = EVEN_BCU + EVEN_Q
    wk, wv = w_in[:, c0:c0 + kw], w_in[:, c0 + kw:c0 + 2 * kw]
    dup = lambda w: jnp.concatenate(
        [w[:, g * SWA_HEAD_DIM:(g + 1) * SWA_HEAD_DIM] for g in range(SWA_KV_HEADS) for _ in range(2)], axis=1)
    return jnp.concatenate([w_in[:, :c0], dup(wk), dup(wv)], axis=1).astype(BF)


def _odd_weights(w_in, w_q_up, w_kv_up):
    d = w_in.shape[0]
    c = 0
    mq = w_in[:, c:c + MLA_Q_RANK]; c += MLA_Q_RANK
    dq = w_in[:, c:c + DIFF_W]; c += DIFF_W
    kvd = w_in[:, c:c + MLA_KV_RANK]; c += MLA_KV_RANK
    kpe = w_in[:, c:c + MLA_ROPE]; c += MLA_ROPE
    dk = w_in[:, c:c + DIFF_W]; c += DIFF_W
    dv = w_in[:, c:]
    pad_tail = LANES - MLA_NOPE - MLA_ROPE
    kpe_chunk = jnp.concatenate([jnp.zeros((d, MLA_NOPE), F32), kpe, jnp.zeros((d, pad_tail), F32)], axis=1)
    w = jnp.concatenate([mq, kvd, kpe_chunk, dq, dk, dv], axis=1).astype(BF)
    qh = w_q_up.reshape(MLA_Q_RANK, MLA_HEADS, MLA_NOPE + MLA_ROPE)
    wq = jnp.pad(qh, ((0, 0), (0, 0), (0, pad_tail))).reshape(MLA_Q_RANK, MLA_QK).astype(BF)
    kvh = w_kv_up.reshape(MLA_KV_RANK, MLA_HEADS, MLA_NOPE + MLA_V)
    wk = jnp.pad(kvh[:, :, :MLA_NOPE], ((0, 0), (0, 0), (0, LANES - MLA_NOPE))).reshape(MLA_KV_RANK, MLA_QK)
    wv = kvh[:, :, MLA_NOPE:].reshape(MLA_KV_RANK, MLA_VW)
    return w, wq, jnp.concatenate([wk, wv], axis=1).astype(BF)


def kernel(x, c, ctx, c_ctx, w_mod, b_mod, norm_g, w_in_even, conv_w, sink, w_out_even, w_in_odd, mla_q_norm_g,
           mla_kv_norm_g, w_q_up, w_kv_up, diff_lambda, diff_subln_g, w_out_odd, w_ff_gu, w_ff_down, w_router,
           w_exp_gu, w_exp_down):
    bn, s, d = x.shape
    l = ctx.shape[1]
    depth = w_mod.shape[0]
    n_exp = w_router.shape[-1]
    dims = _Dims(bn, s, l, d)
    assert bn < MOD_ROWS and s % l == 0 and l % BLOCK == 0 and s % GRID_W == 0

    tm = _tile(math.gcd(s, bn * l), 512)
    tm_e = 1024 if dims.t_all >= 8192 else 256
    tq = _tile(s, 1024)

    cc = jnp.zeros((MOD_ROWS, d), F32).at[:bn].set(c).at[bn].set(c_ctx)
    mods = _modulation(cc, w_mod, b_mod)
    rope64 = _rope_tables(s, tm, SWA_HEAD_DIM, SWA_HEAD_DIM, 0)
    rope32 = _rope_tables(s, tm, MLA_ROPE, LANES, MLA_NOPE)

    xa = None
    for layer in range(depth):
        i = layer // 2
        mods3 = mods[layer].reshape(MOD_ROWS, 1, N_MOD * d)
        gains3 = norm_g[layer].reshape(4, 1, d)
        if layer % 2 == 0:
            xs = (x.reshape(bn * s, d), ctx.reshape(bn * l, d)) if xa is None else (xa,)
            bcu, q, k, v, *merged = _even_proj(dims, xs, mods3, gains3, _even_weights(w_in_even[i]), rope64, tm)
            xa = merged[0] if merged else xa
            conv = _gated_conv(dims, bcu, conv_w[i])
            attn = _win_attn(dims, q, k, v, sink[i])
            wo = w_out_even[i].astype(BF)
            xa = _out_proj(dims, conv, attn, wo[:CONV_WIDTH], wo[CONV_WIDTH:], xa, mods3, gains3, tm)
            act_lo, act_hi = _ffn_up(dims, xa, mods3, gains3, w_ff_gu[i].astype(BF), tm)
            wd = w_ff_down[i].astype(BF)
            xa = _out_proj(dims, act_lo, act_hi, wd[:act_lo.shape[1]], wd[act_lo.shape[1]:], xa, mods3, gains3, tm,
                           gate_col=5, gain_row=3)
        else:
            w, wq, wkv = _odd_weights(w_in_odd[i], w_q_up[i], w_kv_up[i])
            qm, km, vm, dq, dk, dv = _odd_proj(dims, xa, mods3, gains3, w, wq, wkv,
                                               mla_q_norm_g[i].reshape(1, -1), mla_kv_norm_g[i].reshape(1, -1),
                                               rope32, rope64, tm)
            lam_init = 0.8 - 0.6 * math.exp(-0.3 * layer)
            extra = (diff_lambda[i], diff_subln_g[i].reshape(1, -1))
            need_ctx = layer < depth - 1
            o_m = _full_attn(dims, "mla", qm, km, vm, need_ctx, tq)
            o_d = _full_attn(dims, "diff", dq, dk, dv, need_ctx, tq, extra, lam_init)
            wo = w_out_odd[i].astype(BF)
            xa = _out_proj(dims, o_m, o_d, wo[:MLA_VW], wo[MLA_VW:], xa, mods3, gains3, tm)
            wr = jnp.pad(w_router[i], ((0, 0), (0, LANES - n_exp)))
            f_in, route = _router(dims, xa, o_m.shape[0], mods3, gains3, wr, n_exp, tm)
            dest, src, block_e, n_used = _routing_tables(route, n_exp, tm_e)
            y = _experts(f_in, src, block_e, n_used, w_exp_gu, w_exp_down, i, tm_e,
                         _tile(w_exp_down.shape[2], 512))
            xa = _combine(dims, y, dest, route, xa, mods3, gains3, tm)
    return xa[:bn * s].reshape(bn, s, d)
```

```python
import functools
import math

import jax
import jax.numpy as jnp
from jax import lax
from jax.experimental import pallas as pl
from jax.experimental.pallas import tpu as pltpu

F32 = jnp.float32
BF = jnp.bfloat16

EPS = 1e-6
ROPE_THETA = 10000.0
GRID_W = 64
BLOCK = 128
N_MOD = 6
LANES = 128
MXU_N = 256
MOD_ROWS = 16

CONV_WIDTH = 512
SWA_HEADS = 8
SWA_KV_HEADS = 2
SWA_HEAD_DIM = 64
MLA_HEADS = 8
MLA_Q_RANK = 256
MLA_KV_RANK = 128
MLA_NOPE = 64
MLA_ROPE = 32
MLA_V = 64
DIFF_HEADS = 4
DIFF_DIM = 64
DIFF_V = 2 * DIFF_DIM
TOP_K = 2

NEG = -1e30
LOG2E = math.log2(math.e)
KEY_CHUNK = 1024


def _tile(n, pref):
    if n <= pref:
        return n
    t = pref - pref % LANES
    while t >= LANES:
        if n % t == 0:
            return t
        t -= LANES
    raise ValueError((n, pref))


def _cparams(sem, vmem_mb=None):
    kw = dict(dimension_semantics=sem)
    if vmem_mb is not None:
        kw["vmem_limit_bytes"] = vmem_mb << 20
    return pltpu.CompilerParams(**kw)


def _dot(a, b):
    return jnp.dot(a, b, preferred_element_type=F32)


def _dot_nt(a, b):
    return lax.dot_general(a, b, (((1,), (1,)), ((), ())), preferred_element_type=F32)


def _rms(x, g):
    return x * lax.rsqrt(jnp.mean(x * x, axis=-1, keepdims=True) + EPS) * g


def _norm_mod(x, g, shift, scale):
    return _rms(x, g) * (1.0 + scale) + shift


def _silu(x):
    return x / (1.0 + jnp.exp(-x))


def _rope(x, cos, sa, sb, half):
    out = []
    for c in range(x.shape[1] // LANES):
        xc = x[:, c * LANES:(c + 1) * LANES]
        out.append(xc * cos + pltpu.roll(xc, LANES - half, 1) * sa + pltpu.roll(xc, half, 1) * sb)
    return out[0] if len(out) == 1 else jnp.concatenate(out, axis=1)


def _mod_kernel(cc_ref, w_ref, b_ref, o_ref):
    a = _silu(cc_ref[...]).astype(BF)
    o_ref[0] = _dot(a, w_ref[0].astype(BF)) + b_ref[0]


def _modulation(cc, w_mod, b_mod):
    depth, d, n = w_mod.shape
    tn = _tile(n, 1536)
    return pl.pallas_call(
        _mod_kernel,
        grid=(depth, n // tn),
        in_specs=[
            pl.BlockSpec((MOD_ROWS, d), lambda l, j: (0, 0)),
            pl.BlockSpec((1, d, tn), lambda l, j: (l, 0, j)),
            pl.BlockSpec((1, 1, tn), lambda l, j: (l, 0, j)),
        ],
        out_specs=pl.BlockSpec((1, MOD_ROWS, tn), lambda l, j: (l, 0, j)),
        out_shape=jax.ShapeDtypeStruct((depth, MOD_ROWS, n), F32),
        compiler_params=_cparams(("arbitrary", "arbitrary"), 40),
    )(cc, w_mod, b_mod.reshape(depth, 1, n))


class _Dims:
    def __init__(self, bn, s, l, d):
        self.bn, self.s, self.l, self.d = bn, s, l, d
        self.t_lat = bn * s
        self.t_all = bn * s + bn * l

    def mod_row(self, i, tm):
        return jnp.minimum(i * tm // self.s, self.bn)


def _mod_spec(dims, tm, col):
    return pl.BlockSpec((1, 1, dims.d), lambda i, *_: (dims.mod_row(i, tm), 0, col))


def _gain_spec(dims, k):
    return pl.BlockSpec((1, 1, dims.d), lambda i, *_: (k, 0, 0))


def _rope_spec(dims, tm):
    n_lat, per_seq = dims.t_lat // tm, dims.s // tm
    return pl.BlockSpec((tm, LANES), lambda i: (jnp.where(i < n_lat, i % per_seq, per_seq), 0))


def _full_spec(shape):
    return pl.BlockSpec(shape, lambda *_: (0,) * len(shape))


def _rope_tables(s, pad_rows, rot_dim, period, lane_off):
    axis_dim = rot_dim // 2
    half = axis_dim // 2
    inv = 1.0 / (ROPE_THETA ** (jnp.arange(0, axis_dim, 2, dtype=F32) / axis_dim))
    pos = jnp.arange(s)
    rows = (pos // GRID_W).astype(F32)[:, None]
    cols = (pos % GRID_W).astype(F32)[:, None]
    lane = jnp.arange(LANES)
    dd = lane % period - lane_off
    active = (dd >= 0) & (dd < rot_dim)
    dd = jnp.clip(dd, 0, rot_dim - 1)
    j = dd % axis_dim
    ang = jnp.where((dd // axis_dim == 0)[None, :], rows, cols) * inv[j % half][None, :]
    first = (j < half)[None, :]
    act = active[None, :]
    cos = jnp.where(act, jnp.cos(ang), 1.0)
    sin = jnp.where(act, jnp.sin(ang), 0.0)
    sa = jnp.where(first, -sin, 0.0)
    sb = jnp.where(first, 0.0, sin)
    pad = lambda t, v: jnp.concatenate([t, jnp.full((pad_rows, LANES), v, F32)], axis=0)
    return pad(cos, 1.0), pad(sa, 0.0), pad(sb, 0.0), half


EVEN_BCU = 3 * CONV_WIDTH
EVEN_Q = SWA_HEADS * SWA_HEAD_DIM
EVEN_KD = SWA_KV_HEADS * LANES


def _even_proj_kernel(*refs, half, q_scale, n_lat_tiles):
    if n_lat_tiles is None:
        x_ref, g_ref, sh_ref, sc_ref, w_ref, cos_ref, sa_ref, sb_ref, bcu_ref, q_ref, k_ref, v_ref = refs
        x = x_ref[...]
    else:
        (xl_ref, xc_ref, g_ref, sh_ref, sc_ref, w_ref, cos_ref, sa_ref, sb_ref,
         bcu_ref, q_ref, k_ref, v_ref, xa_ref) = refs
        x = jnp.where(pl.program_id(0) < n_lat_tiles, xl_ref[...], xc_ref[...])
        xa_ref[...] = x
    h = _norm_mod(x, g_ref[0], sh_ref[0], sc_ref[0]).astype(BF)
    c0, c1, c2 = EVEN_BCU, EVEN_BCU + EVEN_Q, EVEN_BCU + EVEN_Q + EVEN_KD
    bcu_ref[...] = _dot(h, w_ref[:, :c0]).astype(BF)
    qk = _rope(_dot(h, w_ref[:, c0:c2]), cos_ref[...], sa_ref[...], sb_ref[...], half)
    q_ref[...] = (qk[:, :EVEN_Q] * q_scale).astype(BF)
    k_ref[...] = qk[:, EVEN_Q:].astype(BF)
    v_ref[...] = _dot(h, w_ref[:, c2:]).astype(BF)


def _even_proj(dims, xs, mods3, gains3, w, rope, tm):
    n = w.shape[1]
    cos, sa, sb, half = rope
    row = lambda width: pl.BlockSpec((tm, width), lambda i: (i, 0))
    out_w = (EVEN_BCU, EVEN_Q, EVEN_KD, EVEN_KD)
    out_specs = [row(wd) for wd in out_w]
    out_shape = [jax.ShapeDtypeStruct((dims.t_all, wd), BF) for wd in out_w]
    if len(xs) == 1:
        n_lat, x_specs = None, [row(dims.d)]
    else:
        n_lat = dims.t_lat // tm
        x_specs = [pl.BlockSpec((tm, dims.d), lambda i: (jnp.minimum(i, n_lat - 1), 0)),
                   pl.BlockSpec((tm, dims.d), lambda i: (jnp.maximum(i - n_lat, 0), 0))]
        out_specs.append(row(dims.d))
        out_shape.append(jax.ShapeDtypeStruct((dims.t_all, dims.d), F32))
    return pl.pallas_call(
        functools.partial(_even_proj_kernel, half=half, q_scale=SWA_HEAD_DIM ** -0.5 * LOG2E, n_lat_tiles=n_lat),
        grid=(dims.t_all // tm,),
        in_specs=x_specs + [_gain_spec(dims, 0), _mod_spec(dims, tm, 0), _mod_spec(dims, tm, 1),
                            _full_spec((dims.d, n)), _rope_spec(dims, tm), _rope_spec(dims, tm),
                            _rope_spec(dims, tm)],
        out_specs=out_specs,
        out_shape=out_shape,
        compiler_params=_cparams(("arbitrary",), 48),
    )(*xs, gains3, mods3, mods3, w, cos, sa, sb)


def _conv_kernel(b_ref, c_ref, u_ref, cp_ref, up_ref, cn_ref, un_ref, w_ref, o_ref, *, n_lat_blocks, per_seq):
    i = pl.program_id(0)
    rows = b_ref.shape[0]
    hr = cp_ref.shape[0]
    pos = i % per_seq
    is_lat = i < n_lat_blocks
    has_prev = jnp.logical_and(is_lat, pos > 0).astype(F32)
    has_next = jnp.logical_and(is_lat, pos < per_seq - 1).astype(F32)
    cu = c_ref[...].astype(F32) * u_ref[...].astype(F32)
    cu_p = (cp_ref[...].astype(F32) * up_ref[...].astype(F32))[hr - 1:hr] * has_prev
    cu_n = (cn_ref[...].astype(F32) * un_ref[...].astype(F32))[0:1] * has_next
    r = lax.broadcasted_iota(jnp.int32, cu.shape, 0)
    prev = jnp.where(r == 0, cu_p, pltpu.roll(cu, 1, 0))
    nxt = jnp.where(r == rows - 1, cu_n, pltpu.roll(cu, rows - 1, 0))
    w = w_ref[...]
    o_ref[...] = (b_ref[...].astype(F32) * (prev * w[0:1] + cu * w[1:2] + nxt * w[2:3])).astype(BF)


def _gated_conv(dims, bcu, conv_w):
    rows = dims.l
    hr = 16
    nblk = dims.t_all // rows
    per = rows // hr
    last = dims.t_all // hr - 1
    cw = CONV_WIDTH
    main = lambda col: pl.BlockSpec((rows, cw), lambda i: (i, col))
    prev = lambda col: pl.BlockSpec((hr, cw), lambda i: (jnp.maximum(i * per - 1, 0), col))
    nxt = lambda col: pl.BlockSpec((hr, cw), lambda i: (jnp.minimum((i + 1) * per, last), col))
    return pl.pallas_call(
        functools.partial(_conv_kernel, n_lat_blocks=dims.t_lat // rows, per_seq=dims.s // rows),
        grid=(nblk,),
        in_specs=[main(0), main(1), main(2), prev(1), prev(2), nxt(1), nxt(2), _full_spec(conv_w.shape)],
        out_specs=pl.BlockSpec((rows, cw), lambda i: (i, 0)),
        out_shape=jax.ShapeDtypeStruct((dims.t_all, cw), BF),
        compiler_params=_cparams(("arbitrary",)),
    )(bcu, bcu, bcu, bcu, bcu, bcu, bcu, conv_w)


def _win_attn_kernel(sink_ref, q_ref, kp_ref, ko_ref, kn_ref, kc_ref, vp_ref, vo_ref, vn_ref, vc_ref,
                     o_ref, *, nb):
    n = pl.program_id(1)
    blk = q_ref.shape[0]
    n_ctx = kc_ref.shape[0]
    is_lat = n < nb
    lo_s = jnp.where(is_lat, jnp.where(n >= 1, 0, blk), 3 * blk)
    hi_s = jnp.where(is_lat, jnp.where(n + 1 < nb, 3 * blk, 2 * blk), 0)
    shape = (blk, 3 * blk + n_ctx)
    c = lax.broadcasted_iota(jnp.int32, shape, 1)
    r = lax.broadcasted_iota(jnp.int32, shape, 0)
    ok = (c >= 3 * blk) | ((c >= r) & (c - 2 * blk <= r) & (c >= lo_s) & (c < hi_s))
    bias = jnp.where(ok, 0.0, NEG)
    kall = jnp.concatenate([kp_ref[...], ko_ref[...], kn_ref[...], kc_ref[...]], axis=0)
    vall = jnp.concatenate([vp_ref[...], vo_ref[...], vn_ref[...], vc_ref[...]], axis=0)
    lane = lax.broadcasted_iota(jnp.int32, (1, LANES), 1)
    keep = ((lane < LANES // 2).astype(F32).astype(BF), (lane >= LANES // 2).astype(F32).astype(BF))
    lo = lax.broadcasted_iota(jnp.int32, (blk, LANES), 1) < LANES // 2
    per_group = SWA_HEADS // SWA_KV_HEADS

    def score(head):
        pair, sub, g = head // 2, head % 2, head // per_group
        q2 = q_ref[:, pair * LANES:(pair + 1) * LANES]
        return _dot_nt(q2 * keep[sub], kall[:, g * LANES:(g + 1) * LANES]) + bias

    s_next = score(0)
    outs = []
    for head in range(SWA_HEADS):
        s = s_next
        if head + 1 < SWA_HEADS:
            s_next = score(head + 1)
        g = head // per_group
        sk = jnp.full((blk, 1), sink_ref[head], F32) * LOG2E
        m = jnp.maximum(jnp.max(s, axis=-1, keepdims=True), sk)
        p = jnp.exp2(s - m)
        den = jnp.sum(p, axis=-1, keepdims=True) + jnp.exp2(sk - m)
        outs.append(_dot(p.astype(BF), vall[:, g * LANES:(g + 1) * LANES]) / den)
        if head % 2 == 1:
            pair = head // 2
            o_ref[:, pair * LANES:(pair + 1) * LANES] = jnp.where(lo, outs[-2], outs[-1]).astype(BF)


def _win_attn(dims, q, k, v, sink):
    nb = dims.s // BLOCK
    nc = dims.l // BLOCK
    lat_blocks = dims.t_lat // BLOCK
    ctx_blk0 = dims.t_lat // dims.l

    def own(b, n):
        return jnp.where(n < nb, b * nb + n, lat_blocks + b * nc + (n - nb))

    def prev(b, n):
        return jnp.where(n < nb, b * nb + jnp.maximum(n - 1, 0), own(b, n))

    def nxt(b, n):
        return jnp.where(n < nb, b * nb + jnp.minimum(n + 1, nb - 1), own(b, n))

    kv = lambda f: pl.BlockSpec((BLOCK, EVEN_KD), lambda b, n: (f(b, n), 0))
    ctx = pl.BlockSpec((dims.l, EVEN_KD), lambda b, n: (ctx_blk0 + b, 0))
    qo = pl.BlockSpec((BLOCK, EVEN_Q), lambda b, n: (own(b, n), 0))
    return pl.pallas_call(
        functools.partial(_win_attn_kernel, nb=nb),
        grid=(dims.bn, nb + nc),
        in_specs=[pl.BlockSpec(memory_space=pltpu.SMEM), qo, kv(prev), kv(own), kv(nxt), ctx,
                  kv(prev), kv(own), kv(nxt), ctx],
        out_specs=qo,
        out_shape=jax.ShapeDtypeStruct((dims.t_all, EVEN_Q), BF),
        compiler_params=_cparams(("arbitrary", "arbitrary")),
    )(sink, q, k, k, k, k, v, v, v, v)


def _out_proj_kernel(a1_ref, a2_ref, w1_ref, w2_ref, x_ref, gate_ref, g_ref, o_ref):
    y = _dot(a1_ref[...], w1_ref[...]) + _dot(a2_ref[...], w2_ref[...])
    o_ref[...] = x_ref[...] + gate_ref[0] * _rms(y, g_ref[0])


def _out_proj(dims, a1, a2, w1, w2, x, mods3, gains3, tm, gate_col=2, gain_row=1):
    row = lambda width: pl.BlockSpec((tm, width), lambda i: (i, 0))
    return pl.pallas_call(
        _out_proj_kernel,
        grid=(a1.shape[0] // tm,),
        in_specs=[row(a1.shape[1]), row(a2.shape[1]), _full_spec(w1.shape), _full_spec(w2.shape),
                  row(dims.d), _mod_spec(dims, tm, gate_col), _gain_spec(dims, gain_row)],
        out_specs=row(dims.d),
        out_shape=jax.ShapeDtypeStruct(x.shape, F32),
        input_output_aliases={4: 0},
        compiler_params=_cparams(("arbitrary",), 40),
    )(a1, a2, w1, w2, x, mods3, gains3)


def _ffn_up_kernel(x_ref, g_ref, sh_ref, sc_ref, w_ref, lo_ref, hi_ref, *, ff):
    h = _norm_mod(x_ref[...], g_ref[0], sh_ref[0], sc_ref[0]).astype(BF)
    c0 = 0
    for ref in (lo_ref, hi_ref):
        c1 = c0 + ref.shape[1]
        ref[...] = (_silu(_dot(h, w_ref[:, c0:c1])) * _dot(h, w_ref[:, ff + c0:ff + c1])).astype(BF)
        c0 = c1


def _ffn_split(ff):
    lo = (ff // 2 + MXU_N - 1) // MXU_N * MXU_N
    return (lo, ff - lo) if 0 < lo < ff else (ff // 2, ff - ff // 2)


def _ffn_up(dims, x, mods3, gains3, w_gu, tm):
    ff = w_gu.shape[1] // 2
    row = lambda width: pl.BlockSpec((tm, width), lambda i: (i, 0))
    widths = _ffn_split(ff)
    return pl.pallas_call(
        functools.partial(_ffn_up_kernel, ff=ff),
        grid=(dims.t_all // tm,),
        in_specs=[row(dims.d), _gain_spec(dims, 2), _mod_spec(dims, tm, 3), _mod_spec(dims, tm, 4),
                  _full_spec(w_gu.shape)],
        out_specs=[row(wd) for wd in widths],
        out_shape=[jax.ShapeDtypeStruct((dims.t_all, wd), BF) for wd in widths],
        compiler_params=_cparams(("arbitrary",), 56),
    )(x, gains3, mods3, mods3, w_gu)


ODD_STAGE1 = MLA_Q_RANK + MLA_KV_RANK + LANES
MLA_QK = MLA_HEADS * LANES
MLA_VW = MLA_HEADS * MLA_V
DIFF_W = DIFF_HEADS * 2 * DIFF_DIM


def _odd_proj_kernel(x_ref, g_ref, sh_ref, sc_ref, w_ref, wq_ref, wkv_ref, qg_ref, kvg_ref,
                     c32_ref, a32_ref, b32_ref, c64_ref, a64_ref, b64_ref,
                     q_ref, k_ref, v_ref, dq_ref, dk_ref, dv_ref, *, half32, half64, mla_scale, diff_scale):
    h = _norm_mod(x_ref[...], g_ref[0], sh_ref[0], sc_ref[0]).astype(BF)
    r32 = (c32_ref[...], a32_ref[...], b32_ref[...], half32)
    r64 = (c64_ref[...], a64_ref[...], b64_ref[...], half64)
    s1 = _dot(h, w_ref[:, :ODD_STAGE1])
    qn = _rms(s1[:, :MLA_Q_RANK], qg_ref[...]).astype(BF)
    kvn = _rms(s1[:, MLA_Q_RANK:MLA_Q_RANK + MLA_KV_RANK], kvg_ref[...]).astype(BF)
    kpe = _rope(s1[:, MLA_Q_RANK + MLA_KV_RANK:], *r32)
    q_ref[...] = (_rope(_dot(qn, wq_ref[...]), *r32) * mla_scale).astype(BF)
    kn = _dot(kvn, wkv_ref[:, :MLA_QK])
    k_ref[...] = (kn + jnp.concatenate([kpe] * MLA_HEADS, axis=1)).astype(BF)
    v_ref[...] = _dot(kvn, wkv_ref[:, MLA_QK:]).astype(BF)
    c0 = ODD_STAGE1
    dq_ref[...] = (_rope(_dot(h, w_ref[:, c0:c0 + DIFF_W]), *r64) * diff_scale).astype(BF)
    dk_ref[...] = _rope(_dot(h, w_ref[:, c0 + DIFF_W:c0 + 2 * DIFF_W]), *r64).astype(BF)
    dv_ref[...] = _dot(h, w_ref[:, c0 + 2 * DIFF_W:]).astype(BF)


def _odd_proj(dims, x, mods3, gains3, w, wq, wkv, qg, kvg, rope32, rope64, tm):
    row = lambda width: pl.BlockSpec((tm, width), lambda i: (i, 0))
    out_w = (MLA_QK, MLA_QK, MLA_VW, DIFF_W, DIFF_W, DIFF_HEADS * DIFF_V)
    rs = _rope_spec(dims, tm)
    return pl.pallas_call(
        functools.partial(_odd_proj_kernel, half32=rope32[3], half64=rope64[3],
                          mla_scale=(MLA_NOPE + MLA_ROPE) ** -0.5 * LOG2E, diff_scale=DIFF_DIM ** -0.5 * LOG2E),
        grid=(dims.t_all // tm,),
        in_specs=[row(dims.d), _gain_spec(dims, 0), _mod_spec(dims, tm, 0), _mod_spec(dims, tm, 1),
                  _full_spec(w.shape), _full_spec(wq.shape), _full_spec(wkv.shape),
                  _full_spec(qg.shape), _full_spec(kvg.shape), rs, rs, rs, rs, rs, rs],
        out_specs=[row(wd) for wd in out_w],
        out_shape=[jax.ShapeDtypeStruct((dims.t_all, wd), BF) for wd in out_w],
        compiler_params=_cparams(("arbitrary",), 48),
    )(x, gains3, mods3, mods3, w, wq, wkv, qg, kvg, *rope32[:3], *rope64[:3])


def _softmax_pv(streams, k_refs, v_refs):
    pieces = []
    for kr, vr in zip(k_refs, v_refs):
        n = kr.shape[0]
        if pieces and n <= KEY_CHUNK // 2:
            pieces[-1].append((kr, vr, 0, n))
        else:
            pieces.extend([[(kr, vr, c0, min(c0 + KEY_CHUNK, n))] for c0 in range(0, n, KEY_CHUNK)])
    tasks = [(si, q, cols_k, cols_v, pc) for si, (q, cols_k, cols_v) in enumerate(streams) for pc in pieces]

    def score(t):
        parts = [_dot_nt(t[1], kr[c0:c1, t[2]]) for kr, _, c0, c1 in t[4]]
        return parts[0] if len(parts) == 1 else jnp.concatenate(parts, axis=1)

    state = [None] * len(streams)
    s_next = score(tasks[0])
    for ti, t in enumerate(tasks):
        s = s_next
        if ti + 1 < len(tasks):
            s_next = score(tasks[ti + 1])
        si, _, _, cols_v, pc = t
        mc = jnp.max(s, axis=-1, keepdims=True)
        if state[si] is None:
            m_new = mc
        else:
            m, l, acc = state[si]
            m_new = jnp.maximum(m, mc)
        p = jnp.exp2(s - m_new)
        ps = jnp.sum(p, axis=-1, keepdims=True)
        pb = p.astype(BF)
        pv, off = None, 0
        for _, vr, c0, c1 in pc:
            term = _dot(pb[:, off:off + c1 - c0], vr[c0:c1, cols_v])
            pv = term if pv is None else pv + term
            off += c1 - c0
        if state[si] is None:
            state[si] = (m_new, ps, pv)
        else:
            a = jnp.exp2(m - m_new)
            state[si] = (m_new, a * l + ps, a * acc + pv)
    return [acc / l for _, l, acc in state]


def _mla_attn_kernel(q_ref, *refs, n_seg):
    k_refs, v_refs, o_ref = refs[:n_seg], refs[n_seg:2 * n_seg], refs[2 * n_seg]
    cols = [slice(sub * LANES, (sub + 1) * LANES) for sub in range(2)]
    outs = _softmax_pv([(q_ref[:, c], c, slice(None)) for c in cols], k_refs, v_refs)
    lo = lax.broadcasted_iota(jnp.int32, outs[0].shape, 1) < MLA_V
    o_ref[...] = jnp.where(lo, outs[0], outs[1]).astype(BF)


def _diff_attn_kernel(lam_ref, sg_ref, q_ref, *refs, n_seg, lam_init):
    k_refs, v_refs, o_ref = refs[:n_seg], refs[n_seg:2 * n_seg], refs[2 * n_seg]
    lp = lam_ref[...]
    lam = (jnp.exp(jnp.sum(lp[0:1] * lp[1:2], axis=-1, keepdims=True))
           - jnp.exp(jnp.sum(lp[2:3] * lp[3:4], axis=-1, keepdims=True)) + lam_init)
    lane = lax.broadcasted_iota(jnp.int32, (1, LANES), 1)
    q = q_ref[...]
    full = slice(None)
    o1, o2 = _softmax_pv([(q * (lane < DIFF_DIM).astype(F32).astype(BF), full, full),
                          (q * (lane >= DIFF_DIM).astype(F32).astype(BF), full, full)], k_refs, v_refs)
    o_ref[...] = (_rms(o1 - lam * o2, sg_ref[...]) * (1.0 - lam_init)).astype(BF)


def _full_attn_call(dims, kind, q, k, v, ctx_queries, tq, extra, lam_init):
    if kind == "mla":
        n_h, qw, vw = MLA_HEADS // 2, 2 * LANES, LANES
    else:
        n_h, qw, vw = DIFF_HEADS, LANES, LANES
    ctx_blk0 = dims.t_lat // dims.l
    ctx_seg = (dims.l, lambda b: ctx_blk0 + b)
    if ctx_queries:
        tq, nq, out_rows = dims.l, 1, dims.bn * dims.l
        q_row = lambda b, qi: ctx_blk0 + b
        o_row = lambda b, qi: b
        seg = [ctx_seg]
    else:
        nq, out_rows = dims.s // tq, dims.t_lat
        q_row = o_row = lambda b, qi: b * nq + qi
        seg = [(dims.s, lambda b: b), ctx_seg]
    kv_specs = lambda w: [pl.BlockSpec((rows, w), (lambda f: lambda b, hh, qi: (f(b), hh))(f)) for rows, f in seg]
    if kind == "mla":
        body = functools.partial(_mla_attn_kernel, n_seg=len(seg))
    else:
        body = functools.partial(_diff_attn_kernel, n_seg=len(seg), lam_init=lam_init)
    return pl.pallas_call(
        body,
        grid=(dims.bn, n_h, nq),
        in_specs=([_full_spec(e.shape) for e in extra]
                  + [pl.BlockSpec((tq, qw), lambda b, hh, qi: (q_row(b, qi), hh))] + kv_specs(qw) + kv_specs(vw)),
        out_specs=pl.BlockSpec((tq, vw), lambda b, hh, qi: (o_row(b, qi), hh)),
        out_shape=jax.ShapeDtypeStruct((out_rows, n_h * vw), BF),
        compiler_params=_cparams(("arbitrary", "arbitrary", "arbitrary"), 56),
    )(*extra, q, *([k] * len(seg)), *([v] * len(seg)))


def _full_attn(dims, kind, q, k, v, need_ctx, tq, extra=(), lam_init=0.0):
    lat = _full_attn_call(dims, kind, q, k, v, False, tq, extra, lam_init)
    if not need_ctx:
        return lat
    return jnp.concatenate([lat, _full_attn_call(dims, kind, q, k, v, True, tq, extra, lam_init)], axis=0)


def _route_rows(x, g, shift, scale, wr_ref, f_ref, r_ref, n_exp):
    f = _norm_mod(x, g, shift, scale)
    f_ref[...] = f
    logits = jnp.dot(f, wr_ref[...], preferred_element_type=F32, precision=lax.Precision.HIGHEST)
    lane = lax.broadcasted_iota(jnp.int32, logits.shape, 1).astype(F32)
    logits = jnp.where(lane < n_exp, logits, NEG)
    m1 = jnp.max(logits, axis=-1, keepdims=True)
    i1 = jnp.min(jnp.where(logits == m1, lane, float(LANES)), axis=-1, keepdims=True)
    rest = jnp.where(lane == i1, NEG, logits)
    m2 = jnp.max(rest, axis=-1, keepdims=True)
    i2 = jnp.min(jnp.where(rest == m2, lane, float(LANES)), axis=-1, keepdims=True)
    e2 = jnp.exp(m2 - m1)
    w1 = 1.0 / (1.0 + e2)
    w2 = e2 / (1.0 + e2)
    r_ref[...] = jnp.where(lane == 0, i1, jnp.where(lane == 1, i2, jnp.where(lane == 2, w1,
                           jnp.where(lane == 3, w2, 0.0))))


def _router_kernel(x_ref, g_ref, sh_ref, sc_ref, wr_ref, f_ref, r_ref, *, n_exp):
    _route_rows(x_ref[...], g_ref[0], sh_ref[0], sc_ref[0], wr_ref, f_ref, r_ref, n_exp)


def _router(dims, x, rows, mods3, gains3, w_router_pad, n_exp, tm):
    row = lambda width: pl.BlockSpec((tm, width), lambda i: (i, 0))
    return pl.pallas_call(
        functools.partial(_router_kernel, n_exp=n_exp),
        grid=(rows // tm,),
        in_specs=[row(dims.d), _gain_spec(dims, 2), _mod_spec(dims, tm, 3), _mod_spec(dims, tm, 4),
                  _full_spec(w_router_pad.shape)],
        out_specs=[row(dims.d), row(LANES)],
        out_shape=[jax.ShapeDtypeStruct((rows, dims.d), F32),
                   jax.ShapeDtypeStruct((rows, LANES), F32)],
        compiler_params=_cparams(("arbitrary",), 40),
    )(x, gains3, mods3, mods3, w_router_pad)


def _row_copy(src_hbm, dst, s, d, sem):
    return pltpu.make_async_copy(src_hbm.at[pl.ds(s, 1)], dst.at[pl.ds(d, 1)], sem)


def _expert_kernel(be_ref, nu_ref, src_cur, src_nxt, x_hbm, wg_ref, wu_ref, wd_ref, o_ref,
                   xbuf, sem, h_s, acc_s, *, tm, nj):
    blk, j = pl.program_id(0), pl.program_id(1)
    nblk = pl.num_programs(0)
    n_used = nu_ref[0]
    used = blk < n_used
    slot = blk % 2
    per_step = tm // nj
    head = tm - per_step * nj

    def issue(src_ref, dst_slot, r):
        _row_copy(x_hbm, xbuf.at[dst_slot], src_ref[0, 0, r], r, sem.at[dst_slot]).start()

    def wait_slot(s):
        pltpu.make_async_copy(x_hbm.at[pl.ds(0, tm)], xbuf.at[s], sem.at[s]).wait()

    @pl.when(j == 0)
    def _():
        @pl.when(blk == 0)
        def _():
            def first(r, c):
                issue(src_cur, 0, r)
                return c

            lax.fori_loop(0, tm, first, 0, unroll=8)

        @pl.when(blk <= n_used)
        def _():
            wait_slot(slot)

        @pl.when(used)
        def _():
            h_s[...] = xbuf[slot].astype(BF)
            for r in range(head):
                issue(src_nxt, 1 - slot, r)

        acc_s[...] = jnp.zeros_like(acc_s)

    @pl.when(used)
    def _():
        for r in range(per_step):
            issue(src_nxt, 1 - slot, head + j * per_step + r)
        h = h_s[...]
        act = (_silu(_dot(h, wg_ref[0, 0].astype(BF))) * _dot(h, wu_ref[0, 0].astype(BF))).astype(BF)
        acc_s[...] += _dot(act, wd_ref[0, 0].astype(BF))

    @pl.when(j == nj - 1)
    def _():
        o_ref[...] = acc_s[...]

        @pl.when(jnp.logical_and(blk == nblk - 1, used))
        def _():
            wait_slot(1 - slot)


def _experts(f_in, src, block_e, n_used, w_gu, w_down, layer_i, tm, tn):
    d = f_in.shape[1]
    ff = w_down.shape[2]
    nj = ff // tn
    nblk = src.shape[0] // tm
    jj = lambda blk, j, nu: jnp.where(blk < nu[0], j, nj - 1)
    src3 = src.reshape(nblk, 1, tm)
    return pl.pallas_call(
        functools.partial(_expert_kernel, tm=tm, nj=nj),
        grid_spec=pltpu.PrefetchScalarGridSpec(
            num_scalar_prefetch=2,
            grid=(nblk, nj),
            in_specs=[pl.BlockSpec((1, 1, tm), lambda blk, j, be, nu: (blk, 0, 0), memory_space=pltpu.SMEM),
                      pl.BlockSpec((1, 1, tm), lambda blk, j, be, nu: (jnp.minimum(blk + 1, nblk - 1), 0, 0),
                                   memory_space=pltpu.SMEM),
                      pl.BlockSpec(memory_space=pl.ANY),
                      pl.BlockSpec((1, 1, d, tn), lambda blk, j, be, nu: (layer_i, be[blk], 0, jj(blk, j, nu))),
                      pl.BlockSpec((1, 1, d, tn), lambda blk, j, be, nu: (layer_i, be[blk], 0, nj + jj(blk, j, nu))),
                      pl.BlockSpec((1, 1, tn, d), lambda blk, j, be, nu: (layer_i, be[blk], jj(blk, j, nu), 0))],
            out_specs=pl.BlockSpec((tm, d), lambda blk, j, be, nu: (blk, 0)),
            scratch_shapes=[pltpu.VMEM((2, tm, d), F32), pltpu.SemaphoreType.DMA((2,)),
                            pltpu.VMEM((tm, d), BF), pltpu.VMEM((tm, d), F32)]),
        out_shape=jax.ShapeDtypeStruct((nblk * tm, d), F32),
        compiler_params=_cparams(("arbitrary", "arbitrary"), 56),
    )(block_e, n_used, src3, src3, f_in, w_gu, w_gu, w_down)


def _combine_kernel(dest_ref, r_ref, x_ref, gate_ref, g_ref, y_hbm, o_ref, buf, sem, *, tm):
    def issue(t, c):
        for k in range(TOP_K):
            _row_copy(y_hbm, buf.at[k], dest_ref[0, 0, TOP_K * t + k], t, sem.at[k]).start()
        return c

    lax.fori_loop(0, tm, issue, 0, unroll=4)
    for k in range(TOP_K):
        pltpu.make_async_copy(y_hbm.at[pl.ds(0, tm)], buf.at[k], sem.at[k]).wait()
    r = r_ref[...]
    y = buf[0] * r[:, 2:3] + buf[1] * r[:, 3:4]
    o_ref[...] = x_ref[...] + gate_ref[0] * _rms(y, g_ref[0])


def _combine(dims, y, dest, route, x, mods3, gains3, tm):
    rows = route.shape[0]
    nt = rows // tm
    row = lambda width: pl.BlockSpec((tm, width), lambda i: (i, 0))
    return pl.pallas_call(
        functools.partial(_combine_kernel, tm=tm),
        grid=(nt,),
        in_specs=[pl.BlockSpec((1, 1, TOP_K * tm), lambda i: (i, 0, 0), memory_space=pltpu.SMEM),
                  row(LANES), row(dims.d), _mod_spec(dims, tm, 5), _gain_spec(dims, 3),
                  pl.BlockSpec(memory_space=pl.ANY)],
        out_specs=row(dims.d),
        out_shape=jax.ShapeDtypeStruct((rows, dims.d), F32),
        scratch_shapes=[pltpu.VMEM((TOP_K, tm, dims.d), F32), pltpu.SemaphoreType.DMA((TOP_K,))],
        input_output_aliases={2: 0} if rows == x.shape[0] else {},
        compiler_params=_cparams(("arbitrary",)),
    )(dest.reshape(nt, 1, TOP_K * tm), route, x, mods3, gains3, y)


def _routing_tables(route, n_exp, tm_e):
    t = route.shape[0]
    a = t * TOP_K
    flat_e = route[:, :TOP_K].astype(jnp.int32).reshape(a)
    onehot = (flat_e[:, None] == jnp.arange(n_exp)[None, :]).astype(jnp.int32)
    csum = jnp.cumsum(onehot, axis=0)
    rank = jnp.take_along_axis(csum, flat_e[:, None], axis=1)[:, 0] - 1
    counts = csum[-1]
    padded = (counts + tm_e - 1) // tm_e * tm_e
    ends = jnp.cumsum(padded)
    dest = (ends - padded)[flat_e] + rank
    n_blocks = -(-(a + n_exp * (tm_e - 1)) // tm_e)
    block_start = jnp.arange(n_blocks, dtype=jnp.int32) * tm_e
    block_e = jnp.minimum(jnp.sum((ends[None, :] <= block_start[:, None]).astype(jnp.int32), axis=1), n_exp - 1)
    n_used = (ends[-1] // tm_e).reshape(1)
    order = jnp.argsort(flat_e, stable=True).astype(jnp.int32)
    row = jnp.arange(n_blocks * tm_e, dtype=jnp.int32)
    e_row = jnp.repeat(block_e, tm_e)
    r = row - (ends - padded)[e_row]
    idx = jnp.clip((jnp.cumsum(counts) - counts)[e_row] + r, 0, a - 1)
    src = jnp.where(r < counts[e_row], order[idx] // TOP_K, 0).astype(jnp.int32)
    return dest.astype(jnp.int32), src, block_e.astype(jnp.int32), n_used.astype(jnp.int32)


def _even_weights(w_in):
    kw = SWA_KV_HEADS * SWA_HEAD_DIM
    c0 = EVEN_BCU + EVEN_Q
    wk, wv = w_in[:, c0:c0 + kw], w_in[:, c0 + kw:c0 + 2 * kw]
    dup = lambda w: jnp.concatenate(
        [w[:, g * SWA_HEAD_DIM:(g + 1) * SWA_HEAD_DIM] for g in range(SWA_KV_HEADS) for _ in range(2)], axis=1)
    return jnp.concatenate([w_in[:, :c0], dup(wk), dup(wv)], axis=1).astype(BF)


def _odd_weights(w_in, w_q_up, w_kv_up):
    d = w_in.shape[0]
    c = 0
    mq = w_in[:, c:c + MLA_Q_RANK]; c += MLA_Q_RANK
    dq = w_in[:, c:c + DIFF_W]; c += DIFF_W
    kvd = w_in[:, c:c + MLA_KV_RANK]; c += MLA_KV_RANK
    kpe = w_in[:, c:c + MLA_ROPE]; c += MLA_ROPE
    dk = w_in[:, c:c + DIFF_W]; c += DIFF_W
    dv = w_in[:, c:]
    pad_tail = LANES - MLA_NOPE - MLA_ROPE
    kpe_chunk = jnp.concatenate([jnp.zeros((d, MLA_NOPE), F32), kpe, jnp.zeros((d, pad_tail), F32)], axis=1)
    w = jnp.concatenate([mq, kvd, kpe_chunk, dq, dk, dv], axis=1).astype(BF)
    qh = w_q_up.reshape(MLA_Q_RANK, MLA_HEADS, MLA_NOPE + MLA_ROPE)
    wq = jnp.pad(qh, ((0, 0), (0, 0), (0, pad_tail))).reshape(MLA_Q_RANK, MLA_QK).astype(BF)
    kvh = w_kv_up.reshape(MLA_KV_RANK, MLA_HEADS, MLA_NOPE + MLA_V)
    wk = jnp.pad(kvh[:, :, :MLA_NOPE], ((0, 0), (0, 0), (0, LANES - MLA_NOPE))).reshape(MLA_KV_RANK, MLA_QK)
    wv = kvh[:, :, MLA_NOPE:].reshape(MLA_KV_RANK, MLA_VW)
    return w, wq, jnp.concatenate([wk, wv], axis=1).astype(BF)


def kernel(x, c, ctx, c_ctx, w_mod, b_mod, norm_g, w_in_even, conv_w, sink, w_out_even, w_in_odd, mla_q_norm_g,
           mla_kv_norm_g, w_q_up, w_kv_up, diff_lambda, diff_subln_g, w_out_odd, w_ff_gu, w_ff_down, w_router,
           w_exp_gu, w_exp_down):
    bn, s, d = x.shape
    l = ctx.shape[1]
    depth = w_mod.shape[0]
    n_exp = w_router.shape[-1]
    dims = _Dims(bn, s, l, d)
    assert bn < MOD_ROWS and s % l == 0 and l % BLOCK == 0 and s % GRID_W == 0

    tm = _tile(math.gcd(s, bn * l), 512)
    tm_e = 1024 if dims.t_all >= 8192 else 256
    tq = _tile(s, 1024)

    cc = jnp.zeros((MOD_ROWS, d), F32).at[:bn].set(c).at[bn].set(c_ctx)
    mods = _modulation(cc, w_mod, b_mod)
    rope64 = _rope_tables(s, tm, SWA_HEAD_DIM, SWA_HEAD_DIM, 0)
    rope32 = _rope_tables(s, tm, MLA_ROPE, LANES, MLA_NOPE)

    xa = None
    for layer in range(depth):
        i = layer // 2
        mods3 = mods[layer].reshape(MOD_ROWS, 1, N_MOD * d)
        gains3 = norm_g[layer].reshape(4, 1, d)
        if layer % 2 == 0:
            xs = (x.reshape(bn * s, d), ctx.reshape(bn * l, d)) if xa is None else (xa,)
            bcu, q, k, v, *merged = _even_proj(dims, xs, mods3, gains3, _even_weights(w_in_even[i]), rope64, tm)
            xa = merged[0] if merged else xa
            conv = _gated_conv(dims, bcu, conv_w[i])
            attn = _win_attn(dims, q, k, v, sink[i])
            wo = w_out_even[i].astype(BF)
            xa = _out_proj(dims, conv, attn, wo[:CONV_WIDTH], wo[CONV_WIDTH:], xa, mods3, gains3, tm)
            act_lo, act_hi = _ffn_up(dims, xa, mods3, gains3, w_ff_gu[i].astype(BF), tm)
            wd = w_ff_down[i].astype(BF)
            xa = _out_proj(dims, act_lo, act_hi, wd[:act_lo.shape[1]], wd[act_lo.shape[1]:], xa, mods3, gains3, tm,
                           gate_col=5, gain_row=3)
        else:
            w, wq, wkv = _odd_weights(w_in_odd[i], w_q_up[i], w_kv_up[i])
            qm, km, vm, dq, dk, dv = _odd_proj(dims, xa, mods3, gains3, w, wq, wkv,
                                               mla_q_norm_g[i].reshape(1, -1), mla_kv_norm_g[i].reshape(1, -1),
                                               rope32, rope64, tm)
            lam_init = 0.8 - 0.6 * math.exp(-0.3 * layer)
            extra = (diff_lambda[i], diff_subln_g[i].reshape(1, -1))
            need_ctx = layer < depth - 1
            o_m = _full_attn(dims, "mla", qm, km, vm, need_ctx, tq)
            o_d = _full_attn(dims, "diff", dq, dk, dv, need_ctx, tq, extra, lam_init)
            wo = w_out_odd[i].astype(BF)
            xa = _out_proj(dims, o_m, o_d, wo[:MLA_VW], wo[MLA_VW:], xa, mods3, gains3, tm)
            wr = jnp.pad(w_router[i], ((0, 0), (0, LANES - n_exp)))
            f_in, route = _router(dims, xa, o_m.shape[0], mods3, gains3, wr, n_exp, tm)
            dest, src, block_e, n_used = _routing_tables(route, n_exp, tm_e)
            y = _experts(f_in, src, block_e, n_used, w_exp_gu, w_exp_down, i, tm_e,
                         _tile(w_exp_down.shape[2], 512))
            xa = _combine(dims, y, dest, route, xa, mods3, gains3, tm)
    return xa[:bn * s].reshape(bn, s, d)
```

```python
import functools
import math

import jax
import jax.numpy as jnp
from jax import lax
from jax.experimental import pallas as pl
from jax.experimental.pallas import tpu as pltpu

F32 = jnp.float32
BF = jnp.bfloat16

EPS = 1e-6
ROPE_THETA = 10000.0
GRID_W = 64
BLOCK = 128
N_MOD = 6
LANES = 128
MXU_N = 256
MOD_ROWS = 16

CONV_WIDTH = 512
SWA_HEADS = 8
SWA_KV_HEADS = 2
SWA_HEAD_DIM = 64
MLA_HEADS = 8
MLA_Q_RANK = 256
MLA_KV_RANK = 128
MLA_NOPE = 64
MLA_ROPE = 32
MLA_V = 64
DIFF_HEADS = 4
DIFF_DIM = 64
DIFF_V = 2 * DIFF_DIM
TOP_K = 2

NEG = -1e30
LOG2E = math.log2(math.e)
KEY_CHUNK = 1024


def _tile(n, pref):
    if n <= pref:
        return n
    t = pref - pref % LANES
    while t >= LANES:
        if n % t == 0:
            return t
        t -= LANES
    raise ValueError((n, pref))


def _cparams(sem, vmem_mb=None):
    kw = dict(dimension_semantics=sem)
    if vmem_mb is not None:
        kw["vmem_limit_bytes"] = vmem_mb << 20
    return pltpu.CompilerParams(**kw)


def _dot(a, b):
    return jnp.dot(a, b, preferred_element_type=F32)


def _dot_nt(a, b):
    return lax.dot_general(a, b, (((1,), (1,)), ((), ())), preferred_element_type=F32)


def _rms(x, g):
    return x * lax.rsqrt(jnp.mean(x * x, axis=-1, keepdims=True) + EPS) * g


def _norm_mod(x, g, shift, scale):
    return _rms(x, g) * (1.0 + scale) + shift


def _silu(x):
    return x / (1.0 + jnp.exp(-x))


def _rope(x, cos, sa, sb, half):
    out = []
    for c in range(x.shape[1] // LANES):
        xc = x[:, c * LANES:(c + 1) * LANES]
        out.append(xc * cos + pltpu.roll(xc, LANES - half, 1) * sa + pltpu.roll(xc, half, 1) * sb)
    return out[0] if len(out) == 1 else jnp.concatenate(out, axis=1)


def _mod_kernel(cc_ref, w_ref, b_ref, o_ref):
    a = _silu(cc_ref[...]).astype(BF)
    o_ref[0] = _dot(a, w_ref[0].astype(BF)) + b_ref[0]


def _modulation(cc, w_mod, b_mod):
    depth, d, n = w_mod.shape
    tn = _tile(n, 1536)
    return pl.pallas_call(
        _mod_kernel,
        grid=(depth, n // tn),
        in_specs=[
            pl.BlockSpec((MOD_ROWS, d), lambda l, j: (0, 0)),
            pl.BlockSpec((1, d, tn), lambda l, j: (l, 0, j)),
            pl.BlockSpec((1, 1, tn), lambda l, j: (l, 0, j)),
        ],
        out_specs=pl.BlockSpec((1, MOD_ROWS, tn), lambda l, j: (l, 0, j)),
        out_shape=jax.ShapeDtypeStruct((depth, MOD_ROWS, n), F32),
        compiler_params=_cparams(("arbitrary", "arbitrary"), 40),
    )(cc, w_mod, b_mod.reshape(depth, 1, n))


class _Dims:
    def __init__(self, bn, s, l, d):
        self.bn, self.s, self.l, self.d = bn, s, l, d
        self.t_lat = bn * s
        self.t_all = bn * s + bn * l

    def mod_row(self, i, tm):
        return jnp.minimum(i * tm // self.s, self.bn)


def _mod_spec(dims, tm, col):
    return pl.BlockSpec((1, 1, dims.d), lambda i, *_: (dims.mod_row(i, tm), 0, col))


def _gain_spec(dims, k):
    return pl.BlockSpec((1, 1, dims.d), lambda i, *_: (k, 0, 0))


def _rope_spec(dims, tm):
    n_lat, per_seq = dims.t_lat // tm, dims.s // tm
    return pl.BlockSpec((tm, LANES), lambda i: (jnp.where(i < n_lat, i % per_seq, per_seq), 0))


def _full_spec(shape):
    return pl.BlockSpec(shape, lambda *_: (0,) * len(shape))


def _rope_tables(s, pad_rows, rot_dim, period, lane_off):
    axis_dim = rot_dim // 2
    half = axis_dim // 2
    inv = 1.0 / (ROPE_THETA ** (jnp.arange(0, axis_dim, 2, dtype=F32) / axis_dim))
    pos = jnp.arange(s)
    rows = (pos // GRID_W).astype(F32)[:, None]
    cols = (pos % GRID_W).astype(F32)[:, None]
    lane = jnp.arange(LANES)
    dd = lane % period - lane_off
    active = (dd >= 0) & (dd < rot_dim)
    dd = jnp.clip(dd, 0, rot_dim - 1)
    j = dd % axis_dim
    ang = jnp.where((dd // axis_dim == 0)[None, :], rows, cols) * inv[j % half][None, :]
    first = (j < half)[None, :]
    act = active[None, :]
    cos = jnp.where(act, jnp.cos(ang), 1.0)
    sin = jnp.where(act, jnp.sin(ang), 0.0)
    sa = jnp.where(first, -sin, 0.0)
    sb = jnp.where(first, 0.0, sin)
    pad = lambda t, v: jnp.concatenate([t, jnp.full((pad_rows, LANES), v, F32)], axis=0)
    return pad(cos, 1.0), pad(sa, 0.0), pad(sb, 0.0), half


EVEN_BCU = 3 * CONV_WIDTH
EVEN_Q = SWA_HEADS * SWA_HEAD_DIM
EVEN_KD = SWA_KV_HEADS * LANES


def _even_proj_kernel(*refs, half, q_scale, n_lat_tiles):
    if n_lat_tiles is None:
        x_ref, g_ref, sh_ref, sc_ref, w_ref, cos_ref, sa_ref, sb_ref, bcu_ref, q_ref, k_ref, v_ref = refs
        x = x_ref[...]
    else:
        (xl_ref, xc_ref, g_ref, sh_ref, sc_ref, w_ref, cos_ref, sa_ref, sb_ref,
         bcu_ref, q_ref, k_ref, v_ref, xa_ref) = refs
        x = jnp.where(pl.program_id(0) < n_lat_tiles, xl_ref[...], xc_ref[...])
        xa_ref[...] = x
    h = _norm_mod(x, g_ref[0], sh_ref[0], sc_ref[0]).astype(BF)
    c0, c1, c2 = EVEN_BCU, EVEN_BCU + EVEN_Q, EVEN_BCU + EVEN_Q + EVEN_KD
    bcu_ref[...] = _dot(h, w_ref[:, :c0]).astype(BF)
    qk = _rope(_dot(h, w_ref[:, c0:c2]), cos_ref[...], sa_ref[...], sb_ref[...], half)
    q_ref[...] = (qk[:, :EVEN_Q] * q_scale).astype(BF)
    k_ref[...] = qk[:, EVEN_Q:].astype(BF)
    v_ref[...] = _dot(h, w_ref[:, c2:]).astype(BF)


def _even_proj(dims, xs, mods3, gains3, w, rope, tm):
    n = w.shape[1]
    cos, sa, sb, half = rope
    row = lambda width: pl.BlockSpec((tm, width), lambda i: (i, 0))
    out_w = (EVEN_BCU, EVEN_Q, EVEN_KD, EVEN_KD)
    out_specs = [row(wd) for wd in out_w]
    out_shape = [jax.ShapeDtypeStruct((dims.t_all, wd), BF) for wd in out_w]
    if len(xs) == 1:
        n_lat, x_specs = None, [row(dims.d)]
    else:
        n_lat = dims.t_lat // tm
        x_specs = [pl.BlockSpec((tm, dims.d), lambda i: (jnp.minimum(i, n_lat - 1), 0)),
                   pl.BlockSpec((tm, dims.d), lambda i: (jnp.maximum(i - n_lat, 0), 0))]
        out_specs.append(row(dims.d))
        out_shape.append(jax.ShapeDtypeStruct((dims.t_all, dims.d), F32))
    return pl.pallas_call(
        functools.partial(_even_proj_kernel, half=half, q_scale=SWA_HEAD_DIM ** -0.5 * LOG2E, n_lat_tiles=n_lat),
        grid=(dims.t_all // tm,),
        in_specs=x_specs + [_gain_spec(dims, 0), _mod_spec(dims, tm, 0), _mod_spec(dims, tm, 1),
                            _full_spec((dims.d, n)), _rope_spec(dims, tm), _rope_spec(dims, tm),
                            _rope_spec(dims, tm)],
        out_specs=out_specs,
        out_shape=out_shape,
        compiler_params=_cparams(("arbitrary",), 48),
    )(*xs, gains3, mods3, mods3, w, cos, sa, sb)


def _conv_kernel(b_ref, c_ref, u_ref, cp_ref, up_ref, cn_ref, un_ref, w_ref, o_ref, *, n_lat_blocks, per_seq):
    i = pl.program_id(0)
    rows = b_ref.shape[0]
    hr = cp_ref.shape[0]
    pos = i % per_seq
    is_lat = i < n_lat_blocks
    has_prev = jnp.logical_and(is_lat, pos > 0).astype(F32)
    has_next = jnp.logical_and(is_lat, pos < per_seq - 1).astype(F32)
    cu = c_ref[...].astype(F32) * u_ref[...].astype(F32)
    cu_p = (cp_ref[...].astype(F32) * up_ref[...].astype(F32))[hr - 1:hr] * has_prev
    cu_n = (cn_ref[...].astype(F32) * un_ref[...].astype(F32))[0:1] * has_next
    r = lax.broadcasted_iota(jnp.int32, cu.shape, 0)
    prev = jnp.where(r == 0, cu_p, pltpu.roll(cu, 1, 0))
    nxt = jnp.where(r == rows - 1, cu_n, pltpu.roll(cu, rows - 1, 0))
    w = w_ref[...]
    o_ref[...] = (b_ref[...].astype(F32) * (prev * w[0:1] + cu * w[1:2] + nxt * w[2:3])).astype(BF)


def _gated_conv(dims, bcu, conv_w):
    rows = dims.l
    hr = 16
    nblk = dims.t_all // rows
    per = rows // hr
    last = dims.t_all // hr - 1
    cw = CONV_WIDTH
    main = lambda col: pl.BlockSpec((rows, cw), lambda i: (i, col))
    prev = lambda col: pl.BlockSpec((hr, cw), lambda i: (jnp.maximum(i * per - 1, 0), col))
    nxt = lambda col: pl.BlockSpec((hr, cw), lambda i: (jnp.minimum((i + 1) * per, last), col))
    return pl.pallas_call(
        functools.partial(_conv_kernel, n_lat_blocks=dims.t_lat // rows, per_seq=dims.s // rows),
        grid=(nblk,),
        in_specs=[main(0), main(1), main(2), prev(1), prev(2), nxt(1), nxt(2), _full_spec(conv_w.shape)],
        out_specs=pl.BlockSpec((rows, cw), lambda i: (i, 0)),
        out_shape=jax.ShapeDtypeStruct((dims.t_all, cw), BF),
        compiler_params=_cparams(("arbitrary",)),
    )(bcu, bcu, bcu, bcu, bcu, bcu, bcu, conv_w)


def _win_attn_kernel(sink_ref, q_ref, kp_ref, ko_ref, kn_ref, kc_ref, vp_ref, vo_ref, vn_ref, vc_ref,
                     o_ref, *, nb):
    n = pl.program_id(1)
    blk = q_ref.shape[0]
    n_ctx = kc_ref.shape[0]
    is_lat = n < nb
    lo_s = jnp.where(is_lat, jnp.where(n >= 1, 0, blk), 3 * blk)
    hi_s = jnp.where(is_lat, jnp.where(n + 1 < nb, 3 * blk, 2 * blk), 0)
    shape = (blk, 3 * blk + n_ctx)
    c = lax.broadcasted_iota(jnp.int32, shape, 1)
    r = lax.broadcasted_iota(jnp.int32, shape, 0)
    ok = (c >= 3 * blk) | ((c >= r) & (c - 2 * blk <= r) & (c >= lo_s) & (c < hi_s))
    bias = jnp.where(ok, 0.0, NEG)
    kall = jnp.concatenate([kp_ref[...], ko_ref[...], kn_ref[...], kc_ref[...]], axis=0)
    vall = jnp.concatenate([vp_ref[...], vo_ref[...], vn_ref[...], vc_ref[...]], axis=0)
    lane = lax.broadcasted_iota(jnp.int32, (1, LANES), 1)
    keep = ((lane < LANES // 2).astype(F32).astype(BF), (lane >= LANES // 2).astype(F32).astype(BF))
    lo = lax.broadcasted_iota(jnp.int32, (blk, LANES), 1) < LANES // 2
    per_group = SWA_HEADS // SWA_KV_HEADS

    def score(head):
        pair, sub, g = head // 2, head % 2, head // per_group
        q2 = q_ref[:, pair * LANES:(pair + 1) * LANES]
        return _dot_nt(q2 * keep[sub], kall[:, g * LANES:(g + 1) * LANES]) + bias

    s_next = score(0)
    outs = []
    for head in range(SWA_HEADS):
        s = s_next
        if head + 1 < SWA_HEADS:
            s_next = score(head + 1)
        g = head // per_group
        sk = jnp.full((blk, 1), sink_ref[head], F32) * LOG2E
        m = jnp.maximum(jnp.max(s, axis=-1, keepdims=True), sk)
        p = jnp.exp2(s - m)
        den = jnp.sum(p, axis=-1, keepdims=True) + jnp.exp2(sk - m)
        outs.append(_dot(p.astype(BF), vall[:, g * LANES:(g + 1) * LANES]) / den)
        if head % 2 == 1:
            pair = head // 2
            o_ref[:, pair * LANES:(pair + 1) * LANES] = jnp.where(lo, outs[-2], outs[-1]).astype(BF)


def _win_attn(dims, q, k, v, sink):
    nb = dims.s // BLOCK
    nc = dims.l // BLOCK
    lat_blocks = dims.t_lat // BLOCK
    ctx_blk0 = dims.t_lat // dims.l

    def own(b, n):
        return jnp.where(n < nb, b * nb + n, lat_blocks + b * nc + (n - nb))

    def prev(b, n):
        return jnp.where(n < nb, b * nb + jnp.maximum(n - 1, 0), own(b, n))

    def nxt(b, n):
        return jnp.where(n < nb, b * nb + jnp.minimum(n + 1, nb - 1), own(b, n))

    kv = lambda f: pl.BlockSpec((BLOCK, EVEN_KD), lambda b, n: (f(b, n), 0))
    ctx = pl.BlockSpec((dims.l, EVEN_KD), lambda b, n: (ctx_blk0 + b, 0))
    qo = pl.BlockSpec((BLOCK, EVEN_Q), lambda b, n: (own(b, n), 0))
    return pl.pallas_call(
        functools.partial(_win_attn_kernel, nb=nb),
        grid=(dims.bn, nb + nc),
        in_specs=[pl.BlockSpec(memory_space=pltpu.SMEM), qo, kv(prev), kv(own), kv(nxt), ctx,
                  kv(prev), kv(own), kv(nxt), ctx],
        out_specs=qo,
        out_shape=jax.ShapeDtypeStruct((dims.t_all, EVEN_Q), BF),
        compiler_params=_cparams(("arbitrary", "arbitrary")),
    )(sink, q, k, k, k, k, v, v, v, v)


def _out_proj_kernel(a1_ref, a2_ref, w1_ref, w2_ref, x_ref, gate_ref, g_ref, o_ref):
    y = _dot(a1_ref[...], w1_ref[...]) + _dot(a2_ref[...], w2_ref[...])
    o_ref[...] = x_ref[...] + gate_ref[0] * _rms(y, g_ref[0])


def _out_proj(dims, a1, a2, w1, w2, x, mods3, gains3, tm, gate_col=2, gain_row=1):
    row = lambda width: pl.BlockSpec((tm, width), lambda i: (i, 0))
    return pl.pallas_call(
        _out_proj_kernel,
        grid=(a1.shape[0] // tm,),
        in_specs=[row(a1.shape[1]), row(a2.shape[1]), _full_spec(w1.shape), _full_spec(w2.shape),
                  row(dims.d), _mod_spec(dims, tm, gate_col), _gain_spec(dims, gain_row)],
        out_specs=row(dims.d),
        out_shape=jax.ShapeDtypeStruct(x.shape, F32),
        input_output_aliases={4: 0},
        compiler_params=_cparams(("arbitrary",), 48),
    )(a1, a2, w1, w2, x, mods3, gains3)


def _ffn_up_kernel(x_ref, g_ref, sh_ref, sc_ref, w_ref, lo_ref, hi_ref, *, ff):
    h = _norm_mod(x_ref[...], g_ref[0], sh_ref[0], sc_ref[0]).astype(BF)
    c0 = 0
    for ref in (lo_ref, hi_ref):
        c1 = c0 + ref.shape[1]
        ref[...] = (_silu(_dot(h, w_ref[:, c0:c1])) * _dot(h, w_ref[:, ff + c0:ff + c1])).astype(BF)
        c0 = c1


def _ffn_split(ff):
    lo = (ff // 2 + MXU_N - 1) // MXU_N * MXU_N
    return (lo, ff - lo) if 0 < lo < ff else (ff // 2, ff - ff // 2)


def _ffn_up(dims, x, mods3, gains3, w_gu, tm):
    ff = w_gu.shape[1] // 2
    row = lambda width: pl.BlockSpec((tm, width), lambda i: (i, 0))
    widths = _ffn_split(ff)
    return pl.pallas_call(
        functools.partial(_ffn_up_kernel, ff=ff),
        grid=(dims.t_all // tm,),
        in_specs=[row(dims.d), _gain_spec(dims, 2), _mod_spec(dims, tm, 3), _mod_spec(dims, tm, 4),
                  _full_spec(w_gu.shape)],
        out_specs=[row(wd) for wd in widths],
        out_shape=[jax.ShapeDtypeStruct((dims.t_all, wd), BF) for wd in widths],
        compiler_params=_cparams(("arbitrary",), 56),
    )(x, gains3, mods3, mods3, w_gu)


ODD_STAGE1 = MLA_Q_RANK + MLA_KV_RANK + LANES
MLA_QK = MLA_HEADS * LANES
MLA_VW = MLA_HEADS * MLA_V
DIFF_W = DIFF_HEADS * 2 * DIFF_DIM


def _odd_proj_kernel(x_ref, g_ref, sh_ref, sc_ref, w_ref, wq_ref, wkv_ref, qg_ref, kvg_ref,
                     c32_ref, a32_ref, b32_ref, c64_ref, a64_ref, b64_ref,
                     q_ref, k_ref, v_ref, dq_ref, dk_ref, dv_ref, *, half32, half64, mla_scale, diff_scale):
    h = _norm_mod(x_ref[...], g_ref[0], sh_ref[0], sc_ref[0]).astype(BF)
    r32 = (c32_ref[...], a32_ref[...], b32_ref[...], half32)
    r64 = (c64_ref[...], a64_ref[...], b64_ref[...], half64)
    s1 = _dot(h, w_ref[:, :ODD_STAGE1])
    qn = _rms(s1[:, :MLA_Q_RANK], qg_ref[...]).astype(BF)
    kvn = _rms(s1[:, MLA_Q_RANK:MLA_Q_RANK + MLA_KV_RANK], kvg_ref[...]).astype(BF)
    kpe = _rope(s1[:, MLA_Q_RANK + MLA_KV_RANK:], *r32)
    q_ref[...] = (_rope(_dot(qn, wq_ref[...]), *r32) * mla_scale).astype(BF)
    kn = _dot(kvn, wkv_ref[:, :MLA_QK])
    k_ref[...] = (kn + jnp.concatenate([kpe] * MLA_HEADS, axis=1)).astype(BF)
    v_ref[...] = _dot(kvn, wkv_ref[:, MLA_QK:]).astype(BF)
    c0 = ODD_STAGE1
    dq_ref[...] = (_rope(_dot(h, w_ref[:, c0:c0 + DIFF_W]), *r64) * diff_scale).astype(BF)
    dk_ref[...] = _rope(_dot(h, w_ref[:, c0 + DIFF_W:c0 + 2 * DIFF_W]), *r64).astype(BF)
    dv_ref[...] = _dot(h, w_ref[:, c0 + 2 * DIFF_W:]).astype(BF)


def _odd_proj(dims, x, mods3, gains3, w, wq, wkv, qg, kvg, rope32, rope64, tm):
    row = lambda width: pl.BlockSpec((tm, width), lambda i: (i, 0))
    out_w = (MLA_QK, MLA_QK, MLA_VW, DIFF_W, DIFF_W, DIFF_HEADS * DIFF_V)
    rs = _rope_spec(dims, tm)
    return pl.pallas_call(
        functools.partial(_odd_proj_kernel, half32=rope32[3], half64=rope64[3],
                          mla_scale=(MLA_NOPE + MLA_ROPE) ** -0.5 * LOG2E, diff_scale=DIFF_DIM ** -0.5 * LOG2E),
        grid=(dims.t_all // tm,),
        in_specs=[row(dims.d), _gain_spec(dims, 0), _mod_spec(dims, tm, 0), _mod_spec(dims, tm, 1),
                  _full_spec(w.shape), _full_spec(wq.shape), _full_spec(wkv.shape),
                  _full_spec(qg.shape), _full_spec(kvg.shape), rs, rs, rs, rs, rs, rs],
        out_specs=[row(wd) for wd in out_w],
        out_shape=[jax.ShapeDtypeStruct((dims.t_all, wd), BF) for wd in out_w],
        compiler_params=_cparams(("arbitrary",), 48),
    )(x, gains3, mods3, mods3, w, wq, wkv, qg, kvg, *rope32[:3], *rope64[:3])


def _softmax_pv(streams, k_refs, v_refs):
    pieces = []
    for kr, vr in zip(k_refs, v_refs):
        n = kr.shape[0]
        if pieces and n <= KEY_CHUNK // 2:
            pieces[-1].append((kr, vr, 0, n))
        else:
            pieces.extend([[(kr, vr, c0, min(c0 + KEY_CHUNK, n))] for c0 in range(0, n, KEY_CHUNK)])
    tasks = [(si, q, cols_k, cols_v, pc) for si, (q, cols_k, cols_v) in enumerate(streams) for pc in pieces]

    def score(t):
        parts = [_dot_nt(t[1], kr[c0:c1, t[2]]) for kr, _, c0, c1 in t[4]]
        return parts[0] if len(parts) == 1 else jnp.concatenate(parts, axis=1)

    state = [None] * len(streams)
    s_next = score(tasks[0])
    for ti, t in enumerate(tasks):
        s = s_next
        if ti + 1 < len(tasks):
            s_next = score(tasks[ti + 1])
        si, _, _, cols_v, pc = t
        mc = jnp.max(s, axis=-1, keepdims=True)
        if state[si] is None:
            m_new = mc
        else:
            m, l, acc = state[si]
            m_new = jnp.maximum(m, mc)
        p = jnp.exp2(s - m_new)
        ps = jnp.sum(p, axis=-1, keepdims=True)
        pb = p.astype(BF)
        pv, off = None, 0
        for _, vr, c0, c1 in pc:
            term = _dot(pb[:, off:off + c1 - c0], vr[c0:c1, cols_v])
            pv = term if pv is None else pv + term
            off += c1 - c0
        if state[si] is None:
            state[si] = (m_new, ps, pv)
        else:
            a = jnp.exp2(m - m_new)
            state[si] = (m_new, a * l + ps, a * acc + pv)
    return [acc / l for _, l, acc in state]


def _mla_attn_kernel(q_ref, *refs, n_seg):
    k_refs, v_refs, o_ref = refs[:n_seg], refs[n_seg:2 * n_seg], refs[2 * n_seg]
    cols = [slice(sub * LANES, (sub + 1) * LANES) for sub in range(2)]
    outs = _softmax_pv([(q_ref[:, c], c, slice(None)) for c in cols], k_refs, v_refs)
    lo = lax.broadcasted_iota(jnp.int32, outs[0].shape, 1) < MLA_V
    o_ref[...] = jnp.where(lo, outs[0], outs[1]).astype(BF)


def _diff_attn_kernel(lam_ref, sg_ref, q_ref, *refs, n_seg, lam_init):
    k_refs, v_refs, o_ref = refs[:n_seg], refs[n_seg:2 * n_seg], refs[2 * n_seg]
    lp = lam_ref[...]
    lam = (jnp.exp(jnp.sum(lp[0:1] * lp[1:2], axis=-1, keepdims=True))
           - jnp.exp(jnp.sum(lp[2:3] * lp[3:4], axis=-1, keepdims=True)) + lam_init)
    lane = lax.broadcasted_iota(jnp.int32, (1, LANES), 1)
    q = q_ref[...]
    full = slice(None)
    o1, o2 = _softmax_pv([(q * (lane < DIFF_DIM).astype(F32).astype(BF), full, full),
                          (q * (lane >= DIFF_DIM).astype(F32).astype(BF), full, full)], k_refs, v_refs)
    o_ref[...] = (_rms(o1 - lam * o2, sg_ref[...]) * (1.0 - lam_init)).astype(BF)


def _full_attn_call(dims, kind, q, k, v, ctx_queries, tq, extra, lam_init):
    if kind == "mla":
        n_h, qw, vw = MLA_HEADS // 2, 2 * LANES, LANES
    else:
        n_h, qw, vw = DIFF_HEADS, LANES, LANES
    ctx_blk0 = dims.t_lat // dims.l
    ctx_seg = (dims.l, lambda b: ctx_blk0 + b)
    if ctx_queries:
        tq, nq, out_rows = dims.l, 1, dims.bn * dims.l
        q_row = lambda b, qi: ctx_blk0 + b
        o_row = lambda b, qi: b
        seg = [ctx_seg]
    else:
        nq, out_rows = dims.s // tq, dims.t_lat
        q_row = o_row = lambda b, qi: b * nq + qi
        seg = [(dims.s, lambda b: b), ctx_seg]
    kv_specs = lambda w: [pl.BlockSpec((rows, w), (lambda f: lambda b, hh, qi: (f(b), hh))(f)) for rows, f in seg]
    if kind == "mla":
        body = functools.partial(_mla_attn_kernel, n_seg=len(seg))
    else:
        body = functools.partial(_diff_attn_kernel, n_seg=len(seg), lam_init=lam_init)
    return pl.pallas_call(
        body,
        grid=(dims.bn, n_h, nq),
        in_specs=([_full_spec(e.shape) for e in extra]
                  + [pl.BlockSpec((tq, qw), lambda b, hh, qi: (q_row(b, qi), hh))] + kv_specs(qw) + kv_specs(vw)),
        out_specs=pl.BlockSpec((tq, vw), lambda b, hh, qi: (o_row(b, qi), hh)),
        out_shape=jax.ShapeDtypeStruct((out_rows, n_h * vw), BF),
        compiler_params=_cparams(("arbitrary", "arbitrary", "arbitrary"), 56),
    )(*extra, q, *([k] * len(seg)), *([v] * len(seg)))


def _full_attn(dims, kind, q, k, v, need_ctx, tq, extra=(), lam_init=0.0):
    lat = _full_attn_call(dims, kind, q, k, v, False, tq, extra, lam_init)
    if not need_ctx:
        return lat
    return jnp.concatenate([lat, _full_attn_call(dims, kind, q, k, v, True, tq, extra, lam_init)], axis=0)


def _route_rows(x, g, shift, scale, wr_ref, f_ref, r_ref, n_exp):
    f = _norm_mod(x, g, shift, scale)
    f_ref[...] = f
    logits = jnp.dot(f, wr_ref[...], preferred_element_type=F32, precision=lax.Precision.HIGHEST)
    lane = lax.broadcasted_iota(jnp.int32, logits.shape, 1).astype(F32)
    logits = jnp.where(lane < n_exp, logits, NEG)
    m1 = jnp.max(logits, axis=-1, keepdims=True)
    i1 = jnp.min(jnp.where(logits == m1, lane, float(LANES)), axis=-1, keepdims=True)
    rest = jnp.where(lane == i1, NEG, logits)
    m2 = jnp.max(rest, axis=-1, keepdims=True)
    i2 = jnp.min(jnp.where(rest == m2, lane, float(LANES)), axis=-1, keepdims=True)
    e2 = jnp.exp(m2 - m1)
    w1 = 1.0 / (1.0 + e2)
    w2 = e2 / (1.0 + e2)
    r_ref[...] = jnp.where(lane == 0, i1, jnp.where(lane == 1, i2, jnp.where(lane == 2, w1,
                           jnp.where(lane == 3, w2, 0.0))))


def _router_kernel(x_ref, g_ref, sh_ref, sc_ref, wr_ref, f_ref, r_ref, *, n_exp):
    _route_rows(x_ref[...], g_ref[0], sh_ref[0], sc_ref[0], wr_ref, f_ref, r_ref, n_exp)


def _router(dims, x, rows, mods3, gains3, w_router_pad, n_exp, tm):
    row = lambda width: pl.BlockSpec((tm, width), lambda i: (i, 0))
    return pl.pallas_call(
        functools.partial(_router_kernel, n_exp=n_exp),
        grid=(rows // tm,),
        in_specs=[row(dims.d), _gain_spec(dims, 2), _mod_spec(dims, tm, 3), _mod_spec(dims, tm, 4),
                  _full_spec(w_router_pad.shape)],
        out_specs=[row(dims.d), row(LANES)],
        out_shape=[jax.ShapeDtypeStruct((rows, dims.d), F32),
                   jax.ShapeDtypeStruct((rows, LANES), F32)],
        compiler_params=_cparams(("arbitrary",), 48),
    )(x, gains3, mods3, mods3, w_router_pad)


def _row_copy(src_hbm, dst, s, d, sem):
    return pltpu.make_async_copy(src_hbm.at[pl.ds(s, 1)], dst.at[pl.ds(d, 1)], sem)


def _expert_kernel(be_ref, nu_ref, src_cur, src_nxt, x_hbm, wg_ref, wu_ref, wd_ref, o_ref,
                   xbuf, sem, h_s, acc_s, *, tm, nj):
    blk, j = pl.program_id(0), pl.program_id(1)
    nblk = pl.num_programs(0)
    n_used = nu_ref[0]
    used = blk < n_used
    slot = blk % 2
    per_step = tm // nj
    head = tm - per_step * nj

    def issue(src_ref, dst_slot, r):
        _row_copy(x_hbm, xbuf.at[dst_slot], src_ref[0, 0, r], r, sem.at[dst_slot]).start()

    def wait_slot(s):
        pltpu.make_async_copy(x_hbm.at[pl.ds(0, tm)], xbuf.at[s], sem.at[s]).wait()

    @pl.when(j == 0)
    def _():
        @pl.when(blk == 0)
        def _():
            def first(r, c):
                issue(src_cur, 0, r)
                return c

            lax.fori_loop(0, tm, first, 0, unroll=8)

        @pl.when(blk <= n_used)
        def _():
            wait_slot(slot)

        @pl.when(used)
        def _():
            h_s[...] = xbuf[slot].astype(BF)
            for r in range(head):
                issue(src_nxt, 1 - slot, r)

        acc_s[...] = jnp.zeros_like(acc_s)

    @pl.when(used)
    def _():
        for r in range(per_step):
            issue(src_nxt, 1 - slot, head + j * per_step + r)
        h = h_s[...]
        act = (_silu(_dot(h, wg_ref[0, 0].astype(BF))) * _dot(h, wu_ref[0, 0].astype(BF))).astype(BF)
        acc_s[...] += _dot(act, wd_ref[0, 0].astype(BF))

    @pl.when(j == nj - 1)
    def _():
        o_ref[...] = acc_s[...]

        @pl.when(jnp.logical_and(blk == nblk - 1, used))
        def _():
            wait_slot(1 - slot)


def _experts(f_in, src, block_e, n_used, w_gu, w_down, layer_i, tm, tn):
    d = f_in.shape[1]
    ff = w_down.shape[2]
    nj = ff // tn
    nblk = src.shape[0] // tm
    jj = lambda blk, j, nu: jnp.where(blk < nu[0], j, nj - 1)
    src3 = src.reshape(nblk, 1, tm)
    return pl.pallas_call(
        functools.partial(_expert_kernel, tm=tm, nj=nj),
        grid_spec=pltpu.PrefetchScalarGridSpec(
            num_scalar_prefetch=2,
            grid=(nblk, nj),
            in_specs=[pl.BlockSpec((1, 1, tm), lambda blk, j, be, nu: (blk, 0, 0), memory_space=pltpu.SMEM),
                      pl.BlockSpec((1, 1, tm), lambda blk, j, be, nu: (jnp.minimum(blk + 1, nblk - 1), 0, 0),
                                   memory_space=pltpu.SMEM),
                      pl.BlockSpec(memory_space=pl.ANY),
                      pl.BlockSpec((1, 1, d, tn), lambda blk, j, be, nu: (layer_i, be[blk], 0, jj(blk, j, nu))),
                      pl.BlockSpec((1, 1, d, tn), lambda blk, j, be, nu: (layer_i, be[blk], 0, nj + jj(blk, j, nu))),
                      pl.BlockSpec((1, 1, tn, d), lambda blk, j, be, nu: (layer_i, be[blk], jj(blk, j, nu), 0))],
            out_specs=pl.BlockSpec((tm, d), lambda blk, j, be, nu: (blk, 0)),
            scratch_shapes=[pltpu.VMEM((2, tm, d), F32), pltpu.SemaphoreType.DMA((2,)),
                            pltpu.VMEM((tm, d), BF), pltpu.VMEM((tm, d), F32)]),
        out_shape=jax.ShapeDtypeStruct((nblk * tm, d), F32),
        compiler_params=_cparams(("arbitrary", "arbitrary"), 56),
    )(block_e, n_used, src3, src3, f_in, w_gu, w_gu, w_down)


def _combine_kernel(dest_ref, r_ref, x_ref, gate_ref, g_ref, y_hbm, o_ref, buf, sem, *, tm):
    def issue(t, c):
        for k in range(TOP_K):
            _row_copy(y_hbm, buf.at[k], dest_ref[0, 0, TOP_K * t + k], t, sem.at[k]).start()
        return c

    lax.fori_loop(0, tm, issue, 0, unroll=4)
    for k in range(TOP_K):
        pltpu.make_async_copy(y_hbm.at[pl.ds(0, tm)], buf.at[k], sem.at[k]).wait()
    r = r_ref[...]
    y = buf[0] * r[:, 2:3] + buf[1] * r[:, 3:4]
    o_ref[...] = x_ref[...] + gate_ref[0] * _rms(y, g_ref[0])


def _combine(dims, y, dest, route, x, mods3, gains3, tm):
    rows = route.shape[0]
    nt = rows // tm
    row = lambda width: pl.BlockSpec((tm, width), lambda i: (i, 0))
    return pl.pallas_call(
        functools.partial(_combine_kernel, tm=tm),
        grid=(nt,),
        in_specs=[pl.BlockSpec((1, 1, TOP_K * tm), lambda i: (i, 0, 0), memory_space=pltpu.SMEM),
                  row(LANES), row(dims.d), _mod_spec(dims, tm, 5), _gain_spec(dims, 3),
                  pl.BlockSpec(memory_space=pl.ANY)],
        out_specs=row(dims.d),
        out_shape=jax.ShapeDtypeStruct((rows, dims.d), F32),
        scratch_shapes=[pltpu.VMEM((TOP_K, tm, dims.d), F32), pltpu.SemaphoreType.DMA((TOP_K,))],
        input_output_aliases={2: 0} if rows == x.shape[0] else {},
        compiler_params=_cparams(("arbitrary",), 48),
    )(dest.reshape(nt, 1, TOP_K * tm), route, x, mods3, gains3, y)


def _routing_tables(route, n_exp, tm_e):
    t = route.shape[0]
    a = t * TOP_K
    flat_e = route[:, :TOP_K].astype(jnp.int32).reshape(a)
    onehot = (flat_e[:, None] == jnp.arange(n_exp)[None, :]).astype(jnp.int32)
    csum = jnp.cumsum(onehot, axis=0)
    rank = jnp.take_along_axis(csum, flat_e[:, None], axis=1)[:, 0] - 1
    counts = csum[-1]
    padded = (counts + tm_e - 1) // tm_e * tm_e
    ends = jnp.cumsum(padded)
    dest = (ends - padded)[flat_e] + rank
    n_blocks = -(-(a + n_exp * (tm_e - 1)) // tm_e)
    block_start = jnp.arange(n_blocks, dtype=jnp.int32) * tm_e
    block_e = jnp.minimum(jnp.sum((ends[None, :] <= block_start[:, None]).astype(jnp.int32), axis=1), n_exp - 1)
    n_used = (ends[-1] // tm_e).reshape(1)
    order = jnp.argsort(flat_e, stable=True).astype(jnp.int32)
    row = jnp.arange(n_blocks * tm_e, dtype=jnp.int32)
    e_row = jnp.repeat(block_e, tm_e)
    r = row - (ends - padded)[e_row]
    idx = jnp.clip((jnp.cumsum(counts) - counts)[e_row] + r, 0, a - 1)
    src = jnp.where(r < counts[e_row], order[idx] // TOP_K, 0).astype(jnp.int32)
    return dest.astype(jnp.int32), src, block_e.astype(jnp.int32), n_used.astype(jnp.int32)


def _even_weights(w_in):
    kw = SWA_KV_HEADS * SWA_HEAD_DIM
    c0 = EVEN_BCU + EVEN_Q
    wk, wv = w_in[:, c0:c0 + kw], w_in[:, c0 + kw:c0 + 2 * kw]
    dup = lambda w: jnp.concatenate(
        [w[:, g * SWA_HEAD_DIM:(g + 1) * SWA_HEAD_DIM] for g in range(SWA_KV_HEADS) for _ in range(2)], axis=1)
    return jnp.concatenate([w_in[:, :c0], dup(wk), dup(wv)], axis=1).astype(BF)


def _odd_weights(w_in, w_q_up, w_kv_up):
    d = w_in.shape[0]
    c = 0
    mq = w_in[:, c:c + MLA_Q_RANK]; c += MLA_Q_RANK
    dq = w_in[:, c:c + DIFF_W]; c += DIFF_W
    kvd = w_in[:, c:c + MLA_KV_RANK]; c += MLA_KV_RANK
    kpe = w_in[:, c:c + MLA_ROPE]; c += MLA_ROPE
    dk = w_in[:, c:c + DIFF_W]; c += DIFF_W
    dv = w_in[:, c:]
    pad_tail = LANES - MLA_NOPE - MLA_ROPE
    kpe_chunk = jnp.concatenate([jnp.zeros((d, MLA_NOPE), F32), kpe, jnp.zeros((d, pad_tail), F32)], axis=1)
    w = jnp.concatenate([mq, kvd, kpe_chunk, dq, dk, dv], axis=1).astype(BF)
    qh = w_q_up.reshape(MLA_Q_RANK, MLA_HEADS, MLA_NOPE + MLA_ROPE)
    wq = jnp.pad(qh, ((0, 0), (0, 0), (0, pad_tail))).reshape(MLA_Q_RANK, MLA_QK).astype(BF)
    kvh = w_kv_up.reshape(MLA_KV_RANK, MLA_HEADS, MLA_NOPE + MLA_V)
    wk = jnp.pad(kvh[:, :, :MLA_NOPE], ((0, 0), (0, 0), (0, LANES - MLA_NOPE))).reshape(MLA_KV_RANK, MLA_QK)
    wv = kvh[:, :, MLA_NOPE:].reshape(MLA_KV_RANK, MLA_VW)
    return w, wq, jnp.concatenate([wk, wv], axis=1).astype(BF)


def kernel(x, c, ctx, c_ctx, w_mod, b_mod, norm_g, w_in_even, conv_w, sink, w_out_even, w_in_odd, mla_q_norm_g,
           mla_kv_norm_g, w_q_up, w_kv_up, diff_lambda, diff_subln_g, w_out_odd, w_ff_gu, w_ff_down, w_router,
           w_exp_gu, w_exp_down):
    bn, s, d = x.shape
    l = ctx.shape[1]
    depth = w_mod.shape[0]
    n_exp = w_router.shape[-1]
    dims = _Dims(bn, s, l, d)
    assert bn < MOD_ROWS and s % l == 0 and l % BLOCK == 0 and s % GRID_W == 0

    tm = _tile(math.gcd(s, bn * l), 512)
    tm_e = 1024 if dims.t_all >= 8192 else 256
    tm_io = _tile(math.gcd(s, bn * l), 1024)
    tq = _tile(s, 1024)

    cc = jnp.zeros((MOD_ROWS, d), F32).at[:bn].set(c).at[bn].set(c_ctx)
    mods = _modulation(cc, w_mod, b_mod)
    rope64 = _rope_tables(s, tm, SWA_HEAD_DIM, SWA_HEAD_DIM, 0)
    rope32 = _rope_tables(s, tm, MLA_ROPE, LANES, MLA_NOPE)

    xa = None
    for layer in range(depth):
        i = layer // 2
        mods3 = mods[layer].reshape(MOD_ROWS, 1, N_MOD * d)
        gains3 = norm_g[layer].reshape(4, 1, d)
        if layer % 2 == 0:
            xs = (x.reshape(bn * s, d), ctx.reshape(bn * l, d)) if xa is None else (xa,)
            bcu, q, k, v, *merged = _even_proj(dims, xs, mods3, gains3, _even_weights(w_in_even[i]), rope64, tm)
            xa = merged[0] if merged else xa
            conv = _gated_conv(dims, bcu, conv_w[i])
            attn = _win_attn(dims, q, k, v, sink[i])
            wo = w_out_even[i].astype(BF)
            xa = _out_proj(dims, conv, attn, wo[:CONV_WIDTH], wo[CONV_WIDTH:], xa, mods3, gains3, tm_io)
            act_lo, act_hi = _ffn_up(dims, xa, mods3, gains3, w_ff_gu[i].astype(BF), tm)
            wd = w_ff_down[i].astype(BF)
            xa = _out_proj(dims, act_lo, act_hi, wd[:act_lo.shape[1]], wd[act_lo.shape[1]:], xa, mods3, gains3, tm,
                           gate_col=5, gain_row=3)
        else:
            w, wq, wkv = _odd_weights(w_in_odd[i], w_q_up[i], w_kv_up[i])
            qm, km, vm, dq, dk, dv = _odd_proj(dims, xa, mods3, gains3, w, wq, wkv,
                                               mla_q_norm_g[i].reshape(1, -1), mla_kv_norm_g[i].reshape(1, -1),
                                               rope32, rope64, tm)
            lam_init = 0.8 - 0.6 * math.exp(-0.3 * layer)
            extra = (diff_lambda[i], diff_subln_g[i].reshape(1, -1))
            need_ctx = layer < depth - 1
            o_m = _full_attn(dims, "mla", qm, km, vm, need_ctx, tq)
            o_d = _full_attn(dims, "diff", dq, dk, dv, need_ctx, tq, extra, lam_init)
            wo = w_out_odd[i].astype(BF)
            xa = _out_proj(dims, o_m, o_d, wo[:MLA_VW], wo[MLA_VW:], xa, mods3, gains3, tm_io)
            wr = jnp.pad(w_router[i], ((0, 0), (0, LANES - n_exp)))
            f_in, route = _router(dims, xa, o_m.shape[0], mods3, gains3, wr, n_exp, tm_io)
            dest, src, block_e, n_used = _routing_tables(route, n_exp, tm_e)
            y = _experts(f_in, src, block_e, n_used, w_exp_gu, w_exp_down, i, tm_e,
                         _tile(w_exp_down.shape[2], 512))
            xa = _combine(dims, y, dest, route, xa, mods3, gains3, tm_io)
    return xa[:bn * s].reshape(bn, s, d)
```
